```python
import math
import jax, jax.numpy as jnp
from jax import lax
import numpy as np

D_MODEL = 2048
BATCH = 4
SEQ = 2048
DEPTH = 1

MIX_WIDTH = D_MODEL
MLA_V = 128
MLA_HEADS = (MIX_WIDTH // 2) // MLA_V
MLA_NOPE = 128
MLA_ROPE = 64
MLA_QK = MLA_NOPE + MLA_ROPE
Q_LORA = D_MODEL // 4
KV_LORA = D_MODEL // 4
DIFF_V = 128
DIFF_HEADS = (MIX_WIDTH - MLA_HEADS * MLA_V) // DIFF_V
DIFF_QK = 64
DIFF_ROT = DIFF_QK // 4
ROPE_THETA = 500000.0
D_FF = -(-8 * D_MODEL // (3 * 256)) * 256
Q_BLOCK = 128
EPS = 1e-6

IN_SPLITS = (Q_LORA, KV_LORA, MLA_ROPE,
             DIFF_HEADS * 2 * DIFF_QK, DIFF_HEADS * 2 * DIFF_QK, DIFF_HEADS * DIFF_V)
IN_COLS = sum(IN_SPLITS)

kernel_name = "hybrid_mla_diffattn_parallel_heads"


def rmsnorm(x, g):
    xf = x.astype(jnp.float32)
    y = xf * lax.rsqrt(jnp.mean(xf * xf, axis=-1, keepdims=True) + EPS)
    return (y * g.astype(jnp.float32)).astype(x.dtype)


def rope(x, pos):
    r = x.shape[-1]
    half = r // 2
    freqs = 1.0 / (ROPE_THETA ** (jnp.arange(0, r, 2, dtype=jnp.float32) / r))
    ang = pos.astype(jnp.float32)[:, None] * freqs[None, :]
    extra = x.ndim - 3
    ang = ang.reshape((1, ang.shape[0]) + (1,) * extra + (half,))
    cos, sin = jnp.cos(ang), jnp.sin(ang)
    xf = x.astype(jnp.float32)
    x1, x2 = xf[..., :half], xf[..., half:]
    out = jnp.concatenate([x1 * cos - x2 * sin, x2 * cos + x1 * sin], axis=-1)
    return out.astype(x.dtype)


def causal_mask(i, seq):
    qpos = i * Q_BLOCK + jnp.arange(Q_BLOCK)
    return jnp.arange(seq)[None, :] <= qpos[:, None]


def mla_attend(q, k, v):
    B, H, S, D = q.shape
    nb = S // Q_BLOCK
    scale = 1.0 / math.sqrt(D)
    qb = q.reshape(B, H, nb, Q_BLOCK, D).transpose(2, 0, 1, 3, 4)

    def one_block(args):
        qi, i = args
        s = jnp.einsum('bhqd,bhkd->bhqk', qi, k).astype(jnp.float32) * scale
        s = jnp.where(causal_mask(i, S)[None, None], s, -jnp.inf)
        p = jax.nn.softmax(s, axis=-1)
        return jnp.einsum('bhqk,bhkd->bhqd', p.astype(v.dtype), v)

    out = lax.map(one_block, (qb, jnp.arange(nb)))
    return out.transpose(1, 0, 3, 2, 4).reshape(B, S, H, v.shape[-1])


def diff_attend(q, k, v, lam):
    B, H, _, S, D = q.shape
    nb = S // Q_BLOCK
    scale = 1.0 / math.sqrt(D)
    qb = q.reshape(B, H, 2, nb, Q_BLOCK, D).transpose(3, 0, 1, 2, 4, 5)

    def one_block(args):
        qi, i = args
        s = jnp.einsum('bhcqd,bhckd->bhcqk', qi, k).astype(jnp.float32) * scale
        s = jnp.where(causal_mask(i, S)[None, None, None], s, -jnp.inf)
        p = jax.nn.softmax(s, axis=-1)
        a = p[:, :, 0] - lam * p[:, :, 1]
        return jnp.einsum('bhqk,bhkd->bhqd', a.astype(v.dtype), v)

    out = lax.map(one_block, (qb, jnp.arange(nb)))
    return out.transpose(1, 0, 3, 2, 4).reshape(B, S, H, v.shape[-1])


def setup_inputs(seed: int = 0) -> dict:
    key = jax.random.key(seed)
    ks = jax.random.split(key, 24)
    f = jnp.float32

    def nrm(k, shape, fan_in):
        return jax.random.normal(k, shape, f) * (fan_in ** -0.5)

    def gain(k, n):
        return 1.0 + 0.02 * jax.random.normal(k, (DEPTH, n), f)

    return {
        "x": jax.random.normal(ks[0], (BATCH, SEQ, D_MODEL), f),
        "attn_norm": gain(ks[1], D_MODEL),
        "w_in": nrm(ks[2], (DEPTH, D_MODEL, IN_COLS), D_MODEL),
        "q_latent_norm": gain(ks[3], Q_LORA),
        "w_q_up": nrm(ks[4], (DEPTH, Q_LORA, MLA_HEADS * MLA_QK), Q_LORA),
        "kv_latent_norm": gain(ks[5], KV_LORA),
        "w_kv_up": nrm(ks[6], (DEPTH, KV_LORA, MLA_HEADS * (MLA_NOPE + MLA_V)), KV_LORA),
        "mla_q_norm": gain(ks[7], MLA_QK),
        "mla_k_norm": gain(ks[8], MLA_QK),
        "mla_out_norm": gain(ks[9], MLA_V),
        "diff_q_norm": gain(ks[10], DIFF_QK),
        "diff_k_norm": gain(ks[11], DIFF_QK),
        "lambda_q1": 0.1 * jax.random.normal(ks[12], (DEPTH, DIFF_QK), f),
        "lambda_k1": 0.1 * jax.random.normal(ks[13], (DEPTH, DIFF_QK), f),
        "lambda_q2": 0.1 * jax.random.normal(ks[14], (DEPTH, DIFF_QK), f),
        "lambda_k2": 0.1 * jax.random.normal(ks[15], (DEPTH, DIFF_QK), f),
        "diff_out_norm": gain(ks[16], DIFF_V),
        "w_o": nrm(ks[17], (DEPTH, MIX_WIDTH, D_MODEL), MIX_WIDTH),
        "ffn_norm": gain(ks[18], D_MODEL),
        "w_gate": nrm(ks[19], (DEPTH, D_MODEL, D_FF), D_MODEL),
        "w_up": nrm(ks[20], (DEPTH, D_MODEL, D_FF), D_MODEL),
        "w_down": nrm(ks[21], (DEPTH, D_FF, D_MODEL), D_FF),
    }


def reference(x, attn_norm, w_in, q_latent_norm, w_q_up, kv_latent_norm, w_kv_up,
              mla_q_norm, mla_k_norm, mla_out_norm, diff_q_norm, diff_k_norm,
              lambda_q1, lambda_k1, lambda_q2, lambda_k2, diff_out_norm, w_o,
              ffn_norm, w_gate, w_up, w_down):
    B, S, _ = x.shape
    pos = jnp.arange(S, dtype=jnp.int32)
    offs = list(np.cumsum(IN_SPLITS)[:-1])

    for l in range(DEPTH):
        lambda_init = 0.8 - 0.6 * math.exp(-0.3 * l)

        h = rmsnorm(x, attn_norm[l])
        proj = h @ w_in[l]
        c_q, c_kv, k_pe, dq, dk, dv = jnp.split(proj, offs, axis=-1)

        q = (rmsnorm(c_q, q_latent_norm[l]) @ w_q_up[l]).reshape(B, S, MLA_HEADS, MLA_QK)
        kv = (rmsnorm(c_kv, kv_latent_norm[l]) @ w_kv_up[l]).reshape(
            B, S, MLA_HEADS, MLA_NOPE + MLA_V)
        k_nope, v_mla = kv[..., :MLA_NOPE], kv[..., MLA_NOPE:]
        k_pe = jnp.broadcast_to(k_pe[:, :, None, :], (B, S, MLA_HEADS, MLA_ROPE))
        k = jnp.concatenate([k_nope, k_pe], axis=-1)
        q = rmsnorm(q, mla_q_norm[l])
        k = rmsnorm(k, mla_k_norm[l])
        q = jnp.concatenate([q[..., :MLA_NOPE], rope(q[..., MLA_NOPE:], pos)], axis=-1)
        k = jnp.concatenate([k[..., :MLA_NOPE], rope(k[..., MLA_NOPE:], pos)], axis=-1)
        o_mla = mla_attend(q.transpose(0, 2, 1, 3), k.transpose(0, 2, 1, 3),
                           v_mla.transpose(0, 2, 1, 3))
        o_mla = rmsnorm(o_mla, mla_out_norm[l]).reshape(B, S, MLA_HEADS * MLA_V)

        dq = rmsnorm(dq.reshape(B, S, DIFF_HEADS, 2, DIFF_QK), diff_q_norm[l])
        dk = rmsnorm(dk.reshape(B, S, DIFF_HEADS, 2, DIFF_QK), diff_k_norm[l])
        dq = jnp.concatenate([rope(dq[..., :DIFF_ROT], pos), dq[..., DIFF_ROT:]], axis=-1)
        dk = jnp.concatenate([rope(dk[..., :DIFF_ROT], pos), dk[..., DIFF_ROT:]], axis=-1)
        lam = (jnp.exp(jnp.sum(lambda_q1[l].astype(jnp.float32) * lambda_k1[l].astype(jnp.float32)))
               - jnp.exp(jnp.sum(lambda_q2[l].astype(jnp.float32) * lambda_k2[l].astype(jnp.float32)))
               + lambda_init)
        dv = dv.reshape(B, S, DIFF_HEADS, DIFF_V).transpose(0, 2, 1, 3)
        o_diff = diff_attend(dq.transpose(0, 2, 3, 1, 4), dk.transpose(0, 2, 3, 1, 4), dv, lam)
        o_diff = (rmsnorm(o_diff, diff_out_norm[l]) * (1.0 - lambda_init)).astype(x.dtype)
        o_diff = o_diff.reshape(B, S, DIFF_HEADS * DIFF_V)

        x = x + jnp.concatenate([o_mla, o_diff], axis=-1) @ w_o[l]

        h = rmsnorm(x, ffn_norm[l])
        x = x + (jax.nn.silu(h @ w_gate[l]) * (h @ w_up[l])) @ w_down[l]

    return x
```

```python
import functools
import math

import jax
import jax.numpy as jnp
import numpy as np
from jax import lax
from jax.experimental import pallas as pl
from jax.experimental.pallas import tpu as pltpu

F32 = jnp.float32
BF16 = jnp.bfloat16

D_MODEL = 2048
MLA_HEADS = 8
MLA_NOPE = 128
MLA_ROPE = 64
MLA_QK = MLA_NOPE + MLA_ROPE
MLA_V = 128
Q_LORA = 512
KV_LORA = 512
DIFF_HEADS = 8
DIFF_QK = 64
DIFF_ROT = 16
DIFF_V = 128
ROPE_THETA = 500000.0
D_FF = 5632
EPS = 1e-6
LAMBDA_INIT = 0.8 - 0.6 * math.exp(-0.3 * 0)

LANE = 128
HEAD_W = 2 * LANE
IN_COLS_PADDED = 4224
VMEM_LIMIT = 56 * 1024 * 1024


def _params(sem):
    return pltpu.CompilerParams(dimension_semantics=sem, vmem_limit_bytes=VMEM_LIMIT)


def _resident(shape):
    return pl.BlockSpec(shape, lambda *_: (0,) * len(shape), pipeline_mode=pl.Buffered(1))


def _inv_rms(xf, n):
    return lax.rsqrt(jnp.sum(xf * xf, axis=-1, keepdims=True) * (1.0 / n) + EPS)


def _in_proj_kernel(x_ref, g_ref, w_ref, o_ref, *, col_chunk):
    x = x_ref[...]
    h = ((x * _inv_rms(x, D_MODEL)) * g_ref[...]).astype(BF16)
    n = o_ref.shape[1]
    for c0 in range(0, n, col_chunk):
        c1 = min(c0 + col_chunk, n)
        o_ref[:, c0:c1] = jnp.dot(h, w_ref[:, c0:c1], preferred_element_type=F32).astype(BF16)


def _in_proj(x2d, g, w, *, tm=512):
    t = x2d.shape[0]
    n = w.shape[1]
    return pl.pallas_call(
        functools.partial(_in_proj_kernel, col_chunk=512),
        grid=(t // tm,),
        in_specs=[pl.BlockSpec((tm, D_MODEL), lambda i: (i, 0)),
                  _resident((1, D_MODEL)),
                  _resident((D_MODEL, n))],
        out_specs=pl.BlockSpec((tm, n), lambda i: (i, 0)),
        out_shape=jax.ShapeDtypeStruct((t, n), BF16),
        compiler_params=_params(("arbitrary",)),
    )(x2d, g, w)


def _rope_mla(y, cos, sin):
    partner = pltpu.roll(y, 96, 1) + pltpu.roll(y, 32, 1)
    return y * cos + partner * sin


def _prep_kernel(cq_ref, ckv_ref, kpe_ref, dq_ref, dk_ref,
                 glq_ref, glkv_ref, wq_ref, wkv_ref,
                 gqn_ref, gqr_ref, gkn_ref, gkr_ref, gdq_ref, gdk_ref,
                 cm_ref, sm_ref, cd_ref, sd_ref,
                 qm_ref, km_ref, vm_ref, qd_ref, kd_ref):
    tm = cq_ref.shape[0]
    cm, sm, cd, sd = cm_ref[...], sm_ref[...], cd_ref[...], sd_ref[...]

    cq = cq_ref[...].astype(F32)
    hq = ((cq * _inv_rms(cq, Q_LORA)) * glq_ref[...]).astype(BF16)
    q = jnp.dot(hq, wq_ref[...], preferred_element_type=F32)
    ckv = ckv_ref[...].astype(F32)
    hkv = ((ckv * _inv_rms(ckv, KV_LORA)) * glkv_ref[...]).astype(BF16)
    kv = jnp.dot(hkv, wkv_ref[...], preferred_element_type=F32)

    kpe = kpe_ref[...].astype(F32)
    kpe_ss = jnp.sum(kpe * kpe, axis=-1, keepdims=True)
    kpe_rot = _rope_mla(kpe * gkr_ref[...], cm, sm)

    gqn, gqr, gkn = gqn_ref[...], gqr_ref[...], gkn_ref[...]
    mla_scale = 1.0 / math.sqrt(MLA_QK)
    nope_w = MLA_HEADS * LANE
    for h in range(MLA_HEADS):
        qn = q[:, h * LANE:(h + 1) * LANE]
        qr = q[:, nope_w + h * LANE:nope_w + (h + 1) * LANE]
        ss = jnp.sum(qn * qn, axis=-1, keepdims=True) + jnp.sum(qr * qr, axis=-1, keepdims=True)
        sq = lax.rsqrt(ss * (1.0 / MLA_QK) + EPS)
        qm_ref[:, h * HEAD_W:h * HEAD_W + LANE] = (((qn * sq) * gqn) * mla_scale).astype(BF16)
        qm_ref[:, h * HEAD_W + LANE:(h + 1) * HEAD_W] = (
            _rope_mla((qr * sq) * gqr, cm, sm) * mla_scale).astype(BF16)

        kn = kv[:, h * LANE:(h + 1) * LANE]
        ssk = jnp.sum(kn * kn, axis=-1, keepdims=True) + kpe_ss
        sk = lax.rsqrt(ssk * (1.0 / MLA_QK) + EPS)
        km_ref[:, h * HEAD_W:h * HEAD_W + LANE] = ((kn * sk) * gkn).astype(BF16)
        km_ref[:, h * HEAD_W + LANE:(h + 1) * HEAD_W] = (kpe_rot * sk).astype(BF16)
        vm_ref[:, h * LANE:(h + 1) * LANE] = kv[:, nope_w + h * LANE:nope_w + (h + 1) * LANE].astype(BF16)

    lane = lax.broadcasted_iota(jnp.int32, (tm, LANE), 1)
    lo = lane < DIFF_QK
    is_x1 = (lane % DIFF_QK) < (DIFF_ROT // 2)

    def prep_diff(x, g, scale):
        sq = x * x
        s_lo = jnp.sum(jnp.where(lo, sq, 0.0), axis=-1, keepdims=True)
        s_hi = jnp.sum(jnp.where(lo, 0.0, sq), axis=-1, keepdims=True)
        inv = lax.rsqrt(jnp.where(lo, s_lo, s_hi) * (1.0 / DIFF_QK) + EPS)
        y = (x * inv) * g
        partner = jnp.where(is_x1, pltpu.roll(y, LANE - DIFF_ROT // 2, 1), pltpu.roll(y, DIFF_ROT // 2, 1))
        out = y * cd + partner * sd
        return out * scale if scale != 1.0 else out

    gdq, gdk = gdq_ref[...], gdk_ref[...]
    diff_scale = 1.0 / math.sqrt(DIFF_QK)
    for h in range(DIFF_HEADS):
        sl = slice(h * LANE, (h + 1) * LANE)
        qd_ref[:, sl] = prep_diff(dq_ref[:, sl].astype(F32), gdq, diff_scale).astype(BF16)
        kd_ref[:, sl] = prep_diff(dk_ref[:, sl].astype(F32), gdk, 1.0).astype(BF16)


def _qkv_prep(proj, glq, glkv, wq, wkv, gqn, gqr, gkn, gkr, gdq, gdk, cm, sm, cd, sd, *, seq, tm=256):
    t = proj.shape[0]
    spt = seq // tm
    row = lambda w, c: pl.BlockSpec((tm, w), lambda i, c=c: (i, c))
    tab = pl.BlockSpec((tm, LANE), lambda i: (i % spt, 0))
    vec = lambda n: _resident((1, n))
    return pl.pallas_call(
        _prep_kernel,
        grid=(t // tm,),
        in_specs=[row(512, 0), row(512, 1), row(LANE, 4096 // LANE), row(1024, 1), row(1024, 2),
                  vec(Q_LORA), vec(KV_LORA), _resident(wq.shape), _resident(wkv.shape),
                  vec(LANE), vec(LANE), vec(LANE), vec(LANE), vec(LANE), vec(LANE),
                  tab, tab, tab, tab],
        out_specs=[pl.BlockSpec((tm, MLA_HEADS * HEAD_W), lambda i: (i, 0)),
                   pl.BlockSpec((tm, MLA_HEADS * HEAD_W), lambda i: (i, 0)),
                   pl.BlockSpec((tm, MLA_HEADS * MLA_V), lambda i: (i, 0)),
                   pl.BlockSpec((tm, DIFF_HEADS * LANE), lambda i: (i, 0)),
                   pl.BlockSpec((tm, DIFF_HEADS * LANE), lambda i: (i, 0))],
        out_shape=[jax.ShapeDtypeStruct((t, MLA_HEADS * HEAD_W), BF16),
                   jax.ShapeDtypeStruct((t, MLA_HEADS * HEAD_W), BF16),
                   jax.ShapeDtypeStruct((t, MLA_HEADS * MLA_V), BF16),
                   jax.ShapeDtypeStruct((t, DIFF_HEADS * LANE), BF16),
                   jax.ShapeDtypeStruct((t, DIFF_HEADS * LANE), BF16)],
        compiler_params=_params(("arbitrary",)),
    )(proj, proj, proj, proj, proj, glq, glkv, wq, wkv, gqn, gqr, gkn, gkr, gdq, gdk, cm, sm, cd, sd)


_NT = (((1,), (1,)), ((), ()))


def _causal_keep(tq):
    row = lax.broadcasted_iota(jnp.int32, (tq, tq), 0)
    col = lax.broadcasted_iota(jnp.int32, (tq, tq), 1)
    return col <= row


def _softmax_pieces(q, k_ref, r0, tq, keep):
    s_d = lax.dot_general(q, k_ref[r0:r0 + tq, :], _NT, preferred_element_type=F32)
    s_d = jnp.where(keep, s_d, -jnp.inf)
    m = jnp.max(s_d, axis=-1, keepdims=True)
    if r0 > 0:
        s_o = lax.dot_general(q, k_ref[0:r0, :], _NT, preferred_element_type=F32)
        m = jnp.maximum(m, jnp.max(s_o, axis=-1, keepdims=True))
        p_o = jnp.exp(s_o - m)
        p_d = jnp.exp(s_d - m)
        l = jnp.sum(p_o, axis=-1, keepdims=True) + jnp.sum(p_d, axis=-1, keepdims=True)
        return p_o, p_d, l
    p_d = jnp.exp(s_d - m)
    return None, p_d, jnp.sum(p_d, axis=-1, keepdims=True)


def _mla_attn_kernel(q_ref, k_ref, v_ref, g_ref, o_ref, *, tq):
    seq = q_ref.shape[0]
    keep = _causal_keep(tq)
    g = g_ref[...]
    for r0 in range(0, seq, tq):
        q = q_ref[r0:r0 + tq, :]
        p_o, p_d, l = _softmax_pieces(q, k_ref, r0, tq, keep)
        o = jnp.dot(p_d.astype(BF16), v_ref[r0:r0 + tq, :], preferred_element_type=F32)
        if p_o is not None:
            o = o + jnp.dot(p_o.astype(BF16), v_ref[0:r0, :], preferred_element_type=F32)
        o = o / l
        o_ref[r0:r0 + tq, :] = ((o * _inv_rms(o, MLA_V)) * g).astype(BF16)


def _diff_attn_kernel(q_ref, k_ref, v_ref, g_ref, lam_ref, o_ref, *, tq):
    seq = q_ref.shape[0]
    keep = _causal_keep(tq)
    g = g_ref[...]
    lp = lam_ref[...]
    lam = (jnp.exp(jnp.sum(lp[0:1, :] * lp[1:2, :], axis=-1, keepdims=True))
           - jnp.exp(jnp.sum(lp[2:3, :] * lp[3:4, :], axis=-1, keepdims=True))
           + LAMBDA_INIT)
    lane = lax.broadcasted_iota(jnp.int32, (tq, LANE), 1)
    first = lane < DIFF_QK
    for r0 in range(0, seq, tq):
        q = q_ref[r0:r0 + tq, :]
        q1 = jnp.where(first, q, jnp.zeros_like(q))
        q2 = jnp.where(first, jnp.zeros_like(q), q)
        p1_o, p1_d, l1 = _softmax_pieces(q1, k_ref, r0, tq, keep)
        p2_o, p2_d, l2 = _softmax_pieces(q2, k_ref, r0, tq, keep)
        c = lam * (l1 / l2)
        o = jnp.dot((p1_d - c * p2_d).astype(BF16), v_ref[r0:r0 + tq, :], preferred_element_type=F32)
        if p1_o is not None:
            o = o + jnp.dot((p1_o - c * p2_o).astype(BF16), v_ref[0:r0, :], preferred_element_type=F32)
        o = o / l1
        o_ref[r0:r0 + tq, :] = (((o * _inv_rms(o, DIFF_V)) * g) * (1.0 - LAMBDA_INIT)).astype(BF16)


def _head_spec(seq, width, col0=0):
    return pl.BlockSpec((seq, width), lambda b, h: (b, col0 + h))


def _mla_attn(qm, km, vm, g, *, batch, seq, tq=256):
    t = qm.shape[0]
    return pl.pallas_call(
        functools.partial(_mla_attn_kernel, tq=tq),
        grid=(batch, MLA_HEADS),
        in_specs=[_head_spec(seq, HEAD_W), _head_spec(seq, HEAD_W), _head_spec(seq, MLA_V),
                  pl.BlockSpec((1, MLA_V), lambda b, h: (0, 0))],
        out_specs=_head_spec(seq, MLA_V),
        out_shape=jax.ShapeDtypeStruct((t, MLA_HEADS * MLA_V), BF16),
        compiler_params=_params(("arbitrary", "arbitrary")),
    )(qm, km, vm, g)


def _diff_attn(qd, kd, proj, g, lam_params, *, batch, seq, tq=256):
    t = qd.shape[0]
    dv_col0 = 3072 // LANE
    return pl.pallas_call(
        functools.partial(_diff_attn_kernel, tq=tq),
        grid=(batch, DIFF_HEADS),
        in_specs=[_head_spec(seq, LANE), _head_spec(seq, LANE), _head_spec(seq, DIFF_V, dv_col0),
                  pl.BlockSpec((1, DIFF_V), lambda b, h: (0, 0)),
                  pl.BlockSpec((4, DIFF_QK), lambda b, h: (0, 0))],
        out_specs=_head_spec(seq, DIFF_V),
        out_shape=jax.ShapeDtypeStruct((t, DIFF_HEADS * DIFF_V), BF16),
        compiler_params=_params(("arbitrary", "arbitrary")),
    )(qd, kd, proj, g, lam_params)


def _out_proj_kernel(x_ref, om_ref, od_ref, w_ref, o_ref):
    km = om_ref.shape[1]
    acc = jnp.dot(om_ref[...], w_ref[0:km, :], preferred_element_type=F32)
    acc = acc + jnp.dot(od_ref[...], w_ref[km:, :], preferred_element_type=F32)
    o_ref[...] = x_ref[...] + acc


def _out_proj(x2d, om, od, w, *, tm=512):
    t = x2d.shape[0]
    return pl.pallas_call(
        _out_proj_kernel,
        grid=(t // tm,),
        in_specs=[pl.BlockSpec((tm, D_MODEL), lambda i: (i, 0)),
                  pl.BlockSpec((tm, om.shape[1]), lambda i: (i, 0)),
                  pl.BlockSpec((tm, od.shape[1]), lambda i: (i, 0)),
                  _resident(w.shape)],
        out_specs=pl.BlockSpec((tm, D_MODEL), lambda i: (i, 0)),
        out_shape=jax.ShapeDtypeStruct((t, D_MODEL), F32),
        compiler_params=_params(("arbitrary",)),
    )(x2d, om, od, w)


def _ffn_kernel(x_ref, g_ref, wg_ref, wu_ref, wd_ref, o_ref, h_ref):
    f = pl.program_id(1)

    @pl.when(f == 0)
    def _():
        x = x_ref[...]
        h_ref[...] = ((x * _inv_rms(x, D_MODEL)) * g_ref[...]).astype(BF16)

    h = h_ref[...]
    gate = jnp.dot(h, wg_ref[...], preferred_element_type=F32)
    up = jnp.dot(h, wu_ref[...], preferred_element_type=F32)
    act = (gate * jax.nn.sigmoid(gate) * up).astype(BF16)
    part = jnp.dot(act, wd_ref[...], preferred_element_type=F32)

    @pl.when(f == 0)
    def _():
        o_ref[...] = x_ref[...] + part

    @pl.when(f > 0)
    def _():
        o_ref[...] += part


def _ffn(x1, g, wg, wu, wd, *, tm=512, tf=512):
    t = x1.shape[0]
    return pl.pallas_call(
        _ffn_kernel,
        grid=(t // tm, D_FF // tf),
        in_specs=[pl.BlockSpec((tm, D_MODEL), lambda i, f: (i, 0)),
                  pl.BlockSpec((1, D_MODEL), lambda i, f: (0, 0)),
                  pl.BlockSpec((D_MODEL, tf), lambda i, f: (0, f)),
                  pl.BlockSpec((D_MODEL, tf), lambda i, f: (0, f)),
                  pl.BlockSpec((tf, D_MODEL), lambda i, f: (f, 0))],
        out_specs=pl.BlockSpec((tm, D_MODEL), lambda i, f: (i, 0)),
        out_shape=jax.ShapeDtypeStruct((t, D_MODEL), F32),
        scratch_shapes=[pltpu.VMEM((tm, D_MODEL), BF16)],
        compiler_params=_params(("arbitrary", "arbitrary")),
    )(x1, g, wg, wu, wd)


def _rope_tables(seq):
    pos = jnp.arange(seq, dtype=jnp.int32).astype(F32)

    def angles(r):
        freqs = 1.0 / (ROPE_THETA ** (jnp.arange(0, r, 2, dtype=F32) / r))
        return pos[:, None] * freqs[None, :]

    am = angles(MLA_ROPE)
    cos, sin = jnp.cos(am), jnp.sin(am)
    pad = LANE - MLA_ROPE
    cm = jnp.concatenate([cos, cos, jnp.ones((seq, pad), F32)], axis=-1)
    sm = jnp.concatenate([-sin, sin, jnp.zeros((seq, pad), F32)], axis=-1)

    ad = angles(DIFF_ROT)
    cos, sin = jnp.cos(ad), jnp.sin(ad)
    rest = DIFF_QK - DIFF_ROT
    cd = jnp.concatenate([cos, cos, jnp.ones((seq, rest), F32)], axis=-1)
    sd = jnp.concatenate([-sin, sin, jnp.zeros((seq, rest), F32)], axis=-1)
    return cm, sm, jnp.tile(cd, (1, 2)), jnp.tile(sd, (1, 2))


def _pad_lanes(v, width):
    return jnp.pad(v, ((0, 0), (0, width - v.shape[-1])))


def kernel(x, attn_norm, w_in, q_latent_norm, w_q_up, kv_latent_norm, w_kv_up, mla_q_norm, mla_k_norm, mla_out_norm, diff_q_norm, diff_k_norm, lambda_q1, lambda_k1, lambda_q2, lambda_k2, diff_out_norm, w_o, ffn_norm, w_gate, w_up, w_down):
    batch, seq, d = x.shape
    assert d == D_MODEL and attn_norm.shape[0] == 1
    t = batch * seq
    x2d = x.reshape(t, d)
    l = 0

    wi = w_in[l].astype(BF16)
    o_kpe = Q_LORA + KV_LORA
    o_dq = o_kpe + MLA_ROPE
    wi = jnp.concatenate([wi[:, :o_kpe], wi[:, o_dq:], wi[:, o_kpe:o_dq],
                          jnp.zeros((d, IN_COLS_PADDED - wi.shape[1]), BF16)], axis=1)
    wq = w_q_up[l].astype(BF16).reshape(Q_LORA, MLA_HEADS, MLA_QK)
    wq = jnp.concatenate(
        [wq[:, :, :MLA_NOPE].reshape(Q_LORA, -1),
         jnp.pad(wq[:, :, MLA_NOPE:], ((0, 0), (0, 0), (0, LANE - MLA_ROPE))).reshape(Q_LORA, -1)], axis=1)
    wkv = w_kv_up[l].astype(BF16).reshape(KV_LORA, MLA_HEADS, MLA_NOPE + MLA_V)
    wkv = jnp.concatenate([wkv[:, :, :MLA_NOPE].reshape(KV_LORA, -1),
                           wkv[:, :, MLA_NOPE:].reshape(KV_LORA, -1)], axis=1)

    gqn, gqr = mla_q_norm[l:l + 1, :MLA_NOPE], _pad_lanes(mla_q_norm[l:l + 1, MLA_NOPE:], LANE)
    gkn, gkr = mla_k_norm[l:l + 1, :MLA_NOPE], _pad_lanes(mla_k_norm[l:l + 1, MLA_NOPE:], LANE)
    gdq = jnp.tile(diff_q_norm[l:l + 1], (1, 2))
    gdk = jnp.tile(diff_k_norm[l:l + 1], (1, 2))
    lam_params = jnp.concatenate([lambda_q1[l:l + 1], lambda_k1[l:l + 1],
                                  lambda_q2[l:l + 1], lambda_k2[l:l + 1]], axis=0)
    cm, sm, cd, sd = _rope_tables(seq)

    proj = _in_proj(x2d, attn_norm[l:l + 1], wi)
    qm, km, vm, qd, kd = _qkv_prep(proj, q_latent_norm[l:l + 1], kv_latent_norm[l:l + 1], wq, wkv,
                                   gqn, gqr, gkn, gkr, gdq, gdk, cm, sm, cd, sd, seq=seq)
    o_mla = _mla_attn(qm, km, vm, mla_out_norm[l:l + 1], batch=batch, seq=seq)
    o_diff = _diff_attn(qd, kd, proj, diff_out_norm[l:l + 1], lam_params, batch=batch, seq=seq)
    x1 = _out_proj(x2d, o_mla, o_diff, w_o[l].astype(BF16))
    out = _ffn(x1, ffn_norm[l:l + 1], w_gate[l].astype(BF16), w_up[l].astype(BF16), w_down[l].astype(BF16))
    return out.reshape(batch, seq, d)
```

```python
import functools
import math

import jax
import jax.numpy as jnp
import numpy as np
from jax import lax
from jax.experimental import pallas as pl
from jax.experimental.pallas import tpu as pltpu

F32 = jnp.float32
BF16 = jnp.bfloat16

D_MODEL = 2048
MLA_HEADS = 8
MLA_NOPE = 128
MLA_ROPE = 64
MLA_QK = MLA_NOPE + MLA_ROPE
MLA_V = 128
Q_LORA = 512
KV_LORA = 512
DIFF_HEADS = 8
DIFF_QK = 64
DIFF_ROT = 16
DIFF_V = 128
ROPE_THETA = 500000.0
D_FF = 5632
EPS = 1e-6
LAMBDA_INIT = 0.8 - 0.6 * math.exp(-0.3 * 0)
LOG2E = math.log2(math.e)

LANE = 128
HEAD_W = 2 * LANE
IN_COLS_PADDED = 4224
VMEM_LIMIT = 56 * 1024 * 1024


def _params(sem):
    return pltpu.CompilerParams(dimension_semantics=sem, vmem_limit_bytes=VMEM_LIMIT)


def _resident(shape):
    return pl.BlockSpec(shape, lambda *_: (0,) * len(shape), pipeline_mode=pl.Buffered(1))


def _inv_rms(xf, n):
    return lax.rsqrt(jnp.sum(xf * xf, axis=-1, keepdims=True) * (1.0 / n) + EPS)


def _in_proj_kernel(x_ref, g_ref, w_ref, o_ref, *, col_chunk):
    x = x_ref[...]
    h = ((x * _inv_rms(x, D_MODEL)) * g_ref[...]).astype(BF16)
    n = o_ref.shape[1]
    for c0 in range(0, n, col_chunk):
        c1 = min(c0 + col_chunk, n)
        o_ref[:, c0:c1] = jnp.dot(h, w_ref[:, c0:c1], preferred_element_type=F32).astype(BF16)


def _in_proj(x2d, g, w, *, tm=512):
    t = x2d.shape[0]
    n = w.shape[1]
    return pl.pallas_call(
        functools.partial(_in_proj_kernel, col_chunk=512),
        name="in_proj",
        grid=(t // tm,),
        in_specs=[pl.BlockSpec((tm, D_MODEL), lambda i: (i, 0)),
                  _resident((1, D_MODEL)),
                  _resident((D_MODEL, n))],
        out_specs=pl.BlockSpec((tm, n), lambda i: (i, 0)),
        out_shape=jax.ShapeDtypeStruct((t, n), BF16),
        compiler_params=_params(("arbitrary",)),
    )(x2d, g, w)


def _rope_mla(y, cos, sin):
    partner = pltpu.roll(y, 96, 1) + pltpu.roll(y, 32, 1)
    return y * cos + partner * sin


def _prep_kernel(cq_ref, ckv_ref, kpe_ref, dq_ref, dk_ref,
                 glq_ref, glkv_ref, wq_ref, wkv_ref,
                 gqn_ref, gqr_ref, gkn_ref, gkr_ref, gdq_ref, gdk_ref,
                 cm_ref, sm_ref, cd_ref, sd_ref,
                 qm_ref, km_ref, vm_ref, qd_ref, kd_ref):
    tm = cq_ref.shape[0]
    cm, sm, cd, sd = cm_ref[...], sm_ref[...], cd_ref[...], sd_ref[...]

    cq = cq_ref[...].astype(F32)
    hq = ((cq * _inv_rms(cq, Q_LORA)) * glq_ref[...]).astype(BF16)
    q = jnp.dot(hq, wq_ref[...], preferred_element_type=F32)
    ckv = ckv_ref[...].astype(F32)
    hkv = ((ckv * _inv_rms(ckv, KV_LORA)) * glkv_ref[...]).astype(BF16)
    kv = jnp.dot(hkv, wkv_ref[...], preferred_element_type=F32)

    kpe = kpe_ref[...].astype(F32)
    kpe_ss = jnp.sum(kpe * kpe, axis=-1, keepdims=True)
    kpe_rot = _rope_mla(kpe * gkr_ref[...], cm, sm)

    gqn, gqr, gkn = gqn_ref[...], gqr_ref[...], gkn_ref[...]
    mla_scale = LOG2E / math.sqrt(MLA_QK)
    nope_w = MLA_HEADS * LANE
    for h in range(MLA_HEADS):
        qn = q[:, h * LANE:(h + 1) * LANE]
        qr = q[:, nope_w + h * LANE:nope_w + (h + 1) * LANE]
        ss = jnp.sum(qn * qn, axis=-1, keepdims=True) + jnp.sum(qr * qr, axis=-1, keepdims=True)
        sq = lax.rsqrt(ss * (1.0 / MLA_QK) + EPS)
        qm_ref[:, h * HEAD_W:h * HEAD_W + LANE] = (((qn * sq) * gqn) * mla_scale).astype(BF16)
        qm_ref[:, h * HEAD_W + LANE:(h + 1) * HEAD_W] = (
            _rope_mla((qr * sq) * gqr, cm, sm) * mla_scale).astype(BF16)

        kn = kv[:, h * LANE:(h + 1) * LANE]
        ssk = jnp.sum(kn * kn, axis=-1, keepdims=True) + kpe_ss
        sk = lax.rsqrt(ssk * (1.0 / MLA_QK) + EPS)
        km_ref[:, h * HEAD_W:h * HEAD_W + LANE] = ((kn * sk) * gkn).astype(BF16)
        km_ref[:, h * HEAD_W + LANE:(h + 1) * HEAD_W] = (kpe_rot * sk).astype(BF16)
        vm_ref[:, h * LANE:(h + 1) * LANE] = kv[:, nope_w + h * LANE:nope_w + (h + 1) * LANE].astype(BF16)

    lane = lax.broadcasted_iota(jnp.int32, (tm, LANE), 1)
    lo = lane < DIFF_QK
    is_x1 = (lane % DIFF_QK) < (DIFF_ROT // 2)

    def prep_diff(x, g, scale):
        sq = x * x
        s_lo = jnp.sum(jnp.where(lo, sq, 0.0), axis=-1, keepdims=True)
        s_hi = jnp.sum(jnp.where(lo, 0.0, sq), axis=-1, keepdims=True)
        inv = lax.rsqrt(jnp.where(lo, s_lo, s_hi) * (1.0 / DIFF_QK) + EPS)
        y = (x * inv) * g
        partner = jnp.where(is_x1, pltpu.roll(y, LANE - DIFF_ROT // 2, 1), pltpu.roll(y, DIFF_ROT // 2, 1))
        out = y * cd + partner * sd
        return out * scale if scale != 1.0 else out

    gdq, gdk = gdq_ref[...], gdk_ref[...]
    diff_scale = LOG2E / math.sqrt(DIFF_QK)
    for h in range(DIFF_HEADS):
        sl = slice(h * LANE, (h + 1) * LANE)
        qd_ref[:, sl] = prep_diff(dq_ref[:, sl].astype(F32), gdq, diff_scale).astype(BF16)
        kd_ref[:, sl] = prep_diff(dk_ref[:, sl].astype(F32), gdk, 1.0).astype(BF16)


def _qkv_prep(proj, glq, glkv, wq, wkv, gqn, gqr, gkn, gkr, gdq, gdk, cm, sm, cd, sd, *, seq, tm=256):
    t = proj.shape[0]
    spt = seq // tm
    row = lambda w, c: pl.BlockSpec((tm, w), lambda i, c=c: (i, c))
    tab = pl.BlockSpec((tm, LANE), lambda i: (i % spt, 0))
    vec = lambda n: _resident((1, n))
    return pl.pallas_call(
        _prep_kernel,
        name="qkv_prep",
        grid=(t // tm,),
        in_specs=[row(512, 0), row(512, 1), row(LANE, 4096 // LANE), row(1024, 1), row(1024, 2),
                  vec(Q_LORA), vec(KV_LORA), _resident(wq.shape), _resident(wkv.shape),
                  vec(LANE), vec(LANE), vec(LANE), vec(LANE), vec(LANE), vec(LANE),
                  tab, tab, tab, tab],
        out_specs=[pl.BlockSpec((tm, MLA_HEADS * HEAD_W), lambda i: (i, 0)),
                   pl.BlockSpec((tm, MLA_HEADS * HEAD_W), lambda i: (i, 0)),
                   pl.BlockSpec((tm, MLA_HEADS * MLA_V), lambda i: (i, 0)),
                   pl.BlockSpec((tm, DIFF_HEADS * LANE), lambda i: (i, 0)),
                   pl.BlockSpec((tm, DIFF_HEADS * LANE), lambda i: (i, 0))],
        out_shape=[jax.ShapeDtypeStruct((t, MLA_HEADS * HEAD_W), BF16),
                   jax.ShapeDtypeStruct((t, MLA_HEADS * HEAD_W), BF16),
                   jax.ShapeDtypeStruct((t, MLA_HEADS * MLA_V), BF16),
                   jax.ShapeDtypeStruct((t, DIFF_HEADS * LANE), BF16),
                   jax.ShapeDtypeStruct((t, DIFF_HEADS * LANE), BF16)],
        compiler_params=_params(("arbitrary",)),
    )(proj, proj, proj, proj, proj, glq, glkv, wq, wkv, gqn, gqr, gkn, gkr, gdq, gdk, cm, sm, cd, sd)


_NT = (((1,), (1,)), ((), ()))


def _causal_keep(tq):
    row = lax.broadcasted_iota(jnp.int32, (tq, tq), 0)
    col = lax.broadcasted_iota(jnp.int32, (tq, tq), 1)
    return col <= row


def _scores(q, k_ref, r0, tq, keep):
    s_d = lax.dot_general(q, k_ref[r0:r0 + tq, :], _NT, preferred_element_type=F32)
    s_d = jnp.where(keep, s_d, -jnp.inf)
    s_o = None
    if r0 > 0:
        s_o = lax.dot_general(q, k_ref[0:r0, :], _NT, preferred_element_type=F32)
    return s_o, s_d


def _softmax_pieces(s):
    s_o, s_d = s
    m = jnp.max(s_d, axis=-1, keepdims=True)
    if s_o is None:
        p_d = jnp.exp2(s_d - m)
        return None, p_d, jnp.sum(p_d, axis=-1, keepdims=True)
    m = jnp.maximum(m, jnp.max(s_o, axis=-1, keepdims=True))
    p_o = jnp.exp2(s_o - m)
    p_d = jnp.exp2(s_d - m)
    return p_o, p_d, jnp.sum(p_o, axis=-1, keepdims=True) + jnp.sum(p_d, axis=-1, keepdims=True)


def _pv(p_o, p_d, v_ref, r0, tq):
    o = jnp.dot(p_d.astype(BF16), v_ref[r0:r0 + tq, :], preferred_element_type=F32)
    if p_o is not None:
        o = o + jnp.dot(p_o.astype(BF16), v_ref[0:r0, :], preferred_element_type=F32)
    return o


def _mla_attn_kernel(q_ref, k_ref, v_ref, g_ref, o_ref, *, tq):
    seq = q_ref.shape[0]
    keep = _causal_keep(tq)
    g = g_ref[...]
    starts = list(range(0, seq, tq))

    def scores(r0):
        return _scores(q_ref[r0:r0 + tq, :], k_ref, r0, tq, keep)

    s_next = scores(starts[0])
    for i, r0 in enumerate(starts):
        s = s_next
        if i + 1 < len(starts):
            s_next = scores(starts[i + 1])
        p_o, p_d, l = _softmax_pieces(s)
        o = _pv(p_o, p_d, v_ref, r0, tq) / l
        o_ref[r0:r0 + tq, :] = ((o * _inv_rms(o, MLA_V)) * g).astype(BF16)


def _diff_attn_kernel(q_ref, k_ref, v_ref, g_ref, lam_ref, o_ref, *, tq):
    seq = q_ref.shape[0]
    keep = _causal_keep(tq)
    g = g_ref[...]
    lp = lam_ref[...]
    lam = (jnp.exp(jnp.sum(lp[0:1, :] * lp[1:2, :], axis=-1, keepdims=True))
           - jnp.exp(jnp.sum(lp[2:3, :] * lp[3:4, :], axis=-1, keepdims=True))
           + LAMBDA_INIT)
    lane = lax.broadcasted_iota(jnp.int32, (tq, LANE), 1)
    first = lane < DIFF_QK
    starts = list(range(0, seq, tq))

    def scores(r0):
        q = q_ref[r0:r0 + tq, :]
        q1 = jnp.where(first, q, jnp.zeros_like(q))
        q2 = jnp.where(first, jnp.zeros_like(q), q)
        return _scores(q1, k_ref, r0, tq, keep), _scores(q2, k_ref, r0, tq, keep)

    s_next = scores(starts[0])
    for i, r0 in enumerate(starts):
        s1, s2 = s_next
        if i + 1 < len(starts):
            s_next = scores(starts[i + 1])
        p1_o, p1_d, l1 = _softmax_pieces(s1)
        p2_o, p2_d, l2 = _softmax_pieces(s2)
        c = lam * (l1 / l2)
        a_o = None if p1_o is None else p1_o - c * p2_o
        o = _pv(a_o, p1_d - c * p2_d, v_ref, r0, tq) / l1
        o_ref[r0:r0 + tq, :] = (((o * _inv_rms(o, DIFF_V)) * g) * (1.0 - LAMBDA_INIT)).astype(BF16)


def _head_spec(seq, width, col0=0):
    return pl.BlockSpec((seq, width), lambda b, h: (b, col0 + h))


def _mla_attn(qm, km, vm, g, *, batch, seq, tq=256):
    t = qm.shape[0]
    return pl.pallas_call(
        functools.partial(_mla_attn_kernel, tq=tq),
        name="mla_attn",
        grid=(batch, MLA_HEADS),
        in_specs=[_head_spec(seq, HEAD_W), _head_spec(seq, HEAD_W), _head_spec(seq, MLA_V),
                  pl.BlockSpec((1, MLA_V), lambda b, h: (0, 0))],
        out_specs=_head_spec(seq, MLA_V),
        out_shape=jax.ShapeDtypeStruct((t, MLA_HEADS * MLA_V), BF16),
        compiler_params=_params(("arbitrary", "arbitrary")),
    )(qm, km, vm, g)


def _diff_attn(qd, kd, proj, g, lam_params, *, batch, seq, tq=256):
    t = qd.shape[0]
    dv_col0 = 3072 // LANE
    return pl.pallas_call(
        functools.partial(_diff_attn_kernel, tq=tq),
        name="diff_attn",
        grid=(batch, DIFF_HEADS),
        in_specs=[_head_spec(seq, LANE), _head_spec(seq, LANE), _head_spec(seq, DIFF_V, dv_col0),
                  pl.BlockSpec((1, DIFF_V), lambda b, h: (0, 0)),
                  pl.BlockSpec((4, DIFF_QK), lambda b, h: (0, 0))],
        out_specs=_head_spec(seq, DIFF_V),
        out_shape=jax.ShapeDtypeStruct((t, DIFF_HEADS * DIFF_V), BF16),
        compiler_params=_params(("arbitrary", "arbitrary")),
    )(qd, kd, proj, g, lam_params)


def _out_proj_kernel(x_ref, om_ref, od_ref, w_ref, o_ref):
    km = om_ref.shape[1]
    acc = jnp.dot(om_ref[...], w_ref[0:km, :], preferred_element_type=F32)
    acc = acc + jnp.dot(od_ref[...], w_ref[km:, :], preferred_element_type=F32)
    o_ref[...] = x_ref[...] + acc


def _out_proj(x2d, om, od, w, *, tm=512):
    t = x2d.shape[0]
    return pl.pallas_call(
        _out_proj_kernel,
        name="out_proj",
        grid=(t // tm,),
        in_specs=[pl.BlockSpec((tm, D_MODEL), lambda i: (i, 0)),
                  pl.BlockSpec((tm, om.shape[1]), lambda i: (i, 0)),
                  pl.BlockSpec((tm, od.shape[1]), lambda i: (i, 0)),
                  _resident(w.shape)],
        out_specs=pl.BlockSpec((tm, D_MODEL), lambda i: (i, 0)),
        out_shape=jax.ShapeDtypeStruct((t, D_MODEL), F32),
        compiler_params=_params(("arbitrary",)),
    )(x2d, om, od, w)


def _ffn_kernel(x_ref, g_ref, wg_ref, wu_ref, wd_ref, o_ref, h_ref, *, out_chunk):
    @pl.when(pl.program_id(1) == 0)
    def _():
        x = x_ref[...]
        h_ref[...] = ((x * _inv_rms(x, D_MODEL)) * g_ref[...]).astype(BF16)
        o_ref[...] = x

    h = h_ref[...]
    gate = jnp.dot(h, wg_ref[...], preferred_element_type=F32)
    up = jnp.dot(h, wu_ref[...], preferred_element_type=F32)
    act = (gate * jax.nn.sigmoid(gate) * up).astype(BF16)
    for c0 in range(0, o_ref.shape[1], out_chunk):
        o_ref[:, c0:c0 + out_chunk] += jnp.dot(act, wd_ref[:, c0:c0 + out_chunk],
                                               preferred_element_type=F32)


def _ffn(x1, g, wg, wu, wd, *, tm=512, tf=512):
    t = x1.shape[0]
    return pl.pallas_call(
        functools.partial(_ffn_kernel, out_chunk=512),
        name="ffn",
        grid=(t // tm, D_FF // tf),
        in_specs=[pl.BlockSpec((tm, D_MODEL), lambda i, f: (i, 0)),
                  pl.BlockSpec((1, D_MODEL), lambda i, f: (0, 0)),
                  pl.BlockSpec((D_MODEL, tf), lambda i, f: (0, f)),
                  pl.BlockSpec((D_MODEL, tf), lambda i, f: (0, f)),
                  pl.BlockSpec((tf, D_MODEL), lambda i, f: (f, 0))],
        out_specs=pl.BlockSpec((tm, D_MODEL), lambda i, f: (i, 0)),
        out_shape=jax.ShapeDtypeStruct((t, D_MODEL), F32),
        scratch_shapes=[pltpu.VMEM((tm, D_MODEL), BF16)],
        compiler_params=_params(("arbitrary", "arbitrary")),
    )(x1, g, wg, wu, wd)


def _rope_tables(seq):
    pos = jnp.arange(seq, dtype=jnp.int32).astype(F32)

    def angles(r):
        freqs = 1.0 / (ROPE_THETA ** (jnp.arange(0, r, 2, dtype=F32) / r))
        return pos[:, None] * freqs[None, :]

    am = angles(MLA_ROPE)
    cos, sin = jnp.cos(am), jnp.sin(am)
    pad = LANE - MLA_ROPE
    cm = jnp.concatenate([cos, cos, jnp.ones((seq, pad), F32)], axis=-1)
    sm = jnp.concatenate([-sin, sin, jnp.zeros((seq, pad), F32)], axis=-1)

    ad = angles(DIFF_ROT)
    cos, sin = jnp.cos(ad), jnp.sin(ad)
    rest = DIFF_QK - DIFF_ROT
    cd = jnp.concatenate([cos, cos, jnp.ones((seq, rest), F32)], axis=-1)
    sd = jnp.concatenate([-sin, sin, jnp.zeros((seq, rest), F32)], axis=-1)
    return cm, sm, jnp.tile(cd, (1, 2)), jnp.tile(sd, (1, 2))


def _pad_lanes(v, width):
    return jnp.pad(v, ((0, 0), (0, width - v.shape[-1])))


def kernel(x, attn_norm, w_in, q_latent_norm, w_q_up, kv_latent_norm, w_kv_up, mla_q_norm, mla_k_norm, mla_out_norm, diff_q_norm, diff_k_norm, lambda_q1, lambda_k1, lambda_q2, lambda_k2, diff_out_norm, w_o, ffn_norm, w_gate, w_up, w_down):
    batch, seq, d = x.shape
    assert d == D_MODEL and attn_norm.shape[0] == 1
    t = batch * seq
    x2d = x.reshape(t, d)
    l = 0

    wi = w_in[l].astype(BF16)
    o_kpe = Q_LORA + KV_LORA
    o_dq = o_kpe + MLA_ROPE
    wi = jnp.concatenate([wi[:, :o_kpe], wi[:, o_dq:], wi[:, o_kpe:o_dq],
                          jnp.zeros((d, IN_COLS_PADDED - wi.shape[1]), BF16)], axis=1)
    wq = w_q_up[l].astype(BF16).reshape(Q_LORA, MLA_HEADS, MLA_QK)
    wq = jnp.concatenate(
        [wq[:, :, :MLA_NOPE].reshape(Q_LORA, -1),
         jnp.pad(wq[:, :, MLA_NOPE:], ((0, 0), (0, 0), (0, LANE - MLA_ROPE))).reshape(Q_LORA, -1)], axis=1)
    wkv = w_kv_up[l].astype(BF16).reshape(KV_LORA, MLA_HEADS, MLA_NOPE + MLA_V)
    wkv = jnp.concatenate([wkv[:, :, :MLA_NOPE].reshape(KV_LORA, -1),
                           wkv[:, :, MLA_NOPE:].reshape(KV_LORA, -1)], axis=1)

    gqn, gqr = mla_q_norm[l:l + 1, :MLA_NOPE], _pad_lanes(mla_q_norm[l:l + 1, MLA_NOPE:], LANE)
    gkn, gkr = mla_k_norm[l:l + 1, :MLA_NOPE], _pad_lanes(mla_k_norm[l:l + 1, MLA_NOPE:], LANE)
    gdq = jnp.tile(diff_q_norm[l:l + 1], (1, 2))
    gdk = jnp.tile(diff_k_norm[l:l + 1], (1, 2))
    lam_params = jnp.concatenate([lambda_q1[l:l + 1], lambda_k1[l:l + 1],
                                  lambda_q2[l:l + 1], lambda_k2[l:l + 1]], axis=0)
    cm, sm, cd, sd = _rope_tables(seq)

    proj = _in_proj(x2d, attn_norm[l:l + 1], wi)
    qm, km, vm, qd, kd = _qkv_prep(proj, q_latent_norm[l:l + 1], kv_latent_norm[l:l + 1], wq, wkv,
                                   gqn, gqr, gkn, gkr, gdq, gdk, cm, sm, cd, sd, seq=seq)
    o_mla = _mla_attn(qm, km, vm, mla_out_norm[l:l + 1], batch=batch, seq=seq)
    o_diff = _diff_attn(qd, kd, proj, diff_out_norm[l:l + 1], lam_params, batch=batch, seq=seq)
    x1 = _out_proj(x2d, o_mla, o_diff, w_o[l].astype(BF16))
    out = _ffn(x1, ffn_norm[l:l + 1], w_gate[l].astype(BF16), w_up[l].astype(BF16), w_down[l].astype(BF16))
    return out.reshape(batch, seq, d)
```

```python
import functools
import math

import jax
import jax.numpy as jnp
import numpy as np
from jax import lax
from jax.experimental import pallas as pl
from jax.experimental.pallas import tpu as pltpu

F32 = jnp.float32
BF16 = jnp.bfloat16

D_MODEL = 2048
MLA_HEADS = 8
MLA_NOPE = 128
MLA_ROPE = 64
MLA_QK = MLA_NOPE + MLA_ROPE
MLA_V = 128
Q_LORA = 512
KV_LORA = 512
DIFF_HEADS = 8
DIFF_QK = 64
DIFF_ROT = 16
DIFF_V = 128
ROPE_THETA = 500000.0
D_FF = 5632
EPS = 1e-6
LAMBDA_INIT = 0.8 - 0.6 * math.exp(-0.3 * 0)
LOG2E = math.log2(math.e)

LANE = 128
HEAD_W = 2 * LANE
IN_COLS_PADDED = 4224
VMEM_LIMIT = 56 * 1024 * 1024


def _params(sem):
    return pltpu.CompilerParams(dimension_semantics=sem, vmem_limit_bytes=VMEM_LIMIT)


def _resident(shape):
    return pl.BlockSpec(shape, lambda *_: (0,) * len(shape), pipeline_mode=pl.Buffered(1))


def _inv_rms(xf, n):
    return lax.rsqrt(jnp.sum(xf * xf, axis=-1, keepdims=True) * (1.0 / n) + EPS)


O_KPE = Q_LORA + KV_LORA
O_DQ = O_KPE + MLA_ROPE
IN_COLS = O_DQ + 3 * DIFF_HEADS * DIFF_V


def _regroup_w_in_kernel(w_ref, o_ref):
    rows = w_ref.shape[0]
    o_ref[:, 0:O_KPE] = w_ref[:, 0:O_KPE].astype(BF16)
    o_ref[:, O_KPE:IN_COLS - MLA_ROPE] = w_ref[:, O_DQ:IN_COLS].astype(BF16)
    o_ref[:, IN_COLS - MLA_ROPE:IN_COLS] = w_ref[:, O_KPE:O_DQ].astype(BF16)
    o_ref[:, IN_COLS:IN_COLS_PADDED] = jnp.zeros((rows, IN_COLS_PADDED - IN_COLS), BF16)


def _regroup_w_in(w, *, tr=256):
    d = w.shape[0]
    return pl.pallas_call(
        _regroup_w_in_kernel,
        name="regroup_w_in",
        grid=(d // tr,),
        in_specs=[pl.BlockSpec((tr, IN_COLS), lambda i: (i, 0))],
        out_specs=pl.BlockSpec((tr, IN_COLS_PADDED), lambda i: (i, 0)),
        out_shape=jax.ShapeDtypeStruct((d, IN_COLS_PADDED), BF16),
        compiler_params=_params(("arbitrary",)),
    )(w)


def _in_proj_kernel(x_ref, g_ref, w_ref, o_ref, *, col_chunk):
    x = x_ref[...]
    h = ((x * _inv_rms(x, D_MODEL)) * g_ref[...]).astype(BF16)
    n = o_ref.shape[1]
    for c0 in range(0, n, col_chunk):
        c1 = min(c0 + col_chunk, n)
        o_ref[:, c0:c1] = jnp.dot(h, w_ref[:, c0:c1], preferred_element_type=F32).astype(BF16)


def _in_proj(x2d, g, w, *, tm=512):
    t = x2d.shape[0]
    n = w.shape[1]
    return pl.pallas_call(
        functools.partial(_in_proj_kernel, col_chunk=512),
        name="in_proj",
        grid=(t // tm,),
        in_specs=[pl.BlockSpec((tm, D_MODEL), lambda i: (i, 0)),
                  _resident((1, D_MODEL)),
                  _resident((D_MODEL, n))],
        out_specs=pl.BlockSpec((tm, n), lambda i: (i, 0)),
        out_shape=jax.ShapeDtypeStruct((t, n), BF16),
        compiler_params=_params(("arbitrary",)),
    )(x2d, g, w)


def _rope_mla(y, cos, sin):
    partner = pltpu.roll(y, 96, 1) + pltpu.roll(y, 32, 1)
    return y * cos + partner * sin


def _prep_kernel(cq_ref, ckv_ref, kpe_ref, dq_ref, dk_ref,
                 glq_ref, glkv_ref, wq_ref, wkv_ref,
                 gqn_ref, gqr_ref, gkn_ref, gkr_ref, gdq_ref, gdk_ref,
                 cm_ref, sm_ref, cd_ref, sd_ref,
                 qm_ref, km_ref, vm_ref, qd_ref, kd_ref):
    tm = cq_ref.shape[0]
    cm, sm, cd, sd = cm_ref[...], sm_ref[...], cd_ref[...], sd_ref[...]

    cq = cq_ref[...].astype(F32)
    hq = ((cq * _inv_rms(cq, Q_LORA)) * glq_ref[...]).astype(BF16)
    q = jnp.dot(hq, wq_ref[...], preferred_element_type=F32)
    ckv = ckv_ref[...].astype(F32)
    hkv = ((ckv * _inv_rms(ckv, KV_LORA)) * glkv_ref[...]).astype(BF16)
    kv = jnp.dot(hkv, wkv_ref[...], preferred_element_type=F32)

    kpe = kpe_ref[...].astype(F32)
    kpe_ss = jnp.sum(kpe * kpe, axis=-1, keepdims=True)
    kpe_rot = _rope_mla(kpe * gkr_ref[...], cm, sm)

    gqn, gqr, gkn = gqn_ref[...], gqr_ref[...], gkn_ref[...]
    mla_scale = LOG2E / math.sqrt(MLA_QK)
    nope_w = MLA_HEADS * LANE
    for h in range(MLA_HEADS):
        qn = q[:, h * LANE:(h + 1) * LANE]
        qr = q[:, nope_w + h * LANE:nope_w + (h + 1) * LANE]
        ss = jnp.sum(qn * qn, axis=-1, keepdims=True) + jnp.sum(qr * qr, axis=-1, keepdims=True)
        sq = lax.rsqrt(ss * (1.0 / MLA_QK) + EPS)
        qm_ref[:, h * HEAD_W:h * HEAD_W + LANE] = (((qn * sq) * gqn) * mla_scale).astype(BF16)
        qm_ref[:, h * HEAD_W + LANE:(h + 1) * HEAD_W] = (
            _rope_mla((qr * sq) * gqr, cm, sm) * mla_scale).astype(BF16)

        kn = kv[:, h * LANE:(h + 1) * LANE]
        ssk = jnp.sum(kn * kn, axis=-1, keepdims=True) + kpe_ss
        sk = lax.rsqrt(ssk * (1.0 / MLA_QK) + EPS)
        km_ref[:, h * HEAD_W:h * HEAD_W + LANE] = ((kn * sk) * gkn).astype(BF16)
        km_ref[:, h * HEAD_W + LANE:(h + 1) * HEAD_W] = (kpe_rot * sk).astype(BF16)
        vm_ref[:, h * LANE:(h + 1) * LANE] = kv[:, nope_w + h * LANE:nope_w + (h + 1) * LANE].astype(BF16)

    lane = lax.broadcasted_iota(jnp.int32, (tm, LANE), 1)
    lo = lane < DIFF_QK
    is_x1 = (lane % DIFF_QK) < (DIFF_ROT // 2)

    def prep_diff(x, g, scale):
        sq = x * x
        s_lo = jnp.sum(jnp.where(lo, sq, 0.0), axis=-1, keepdims=True)
        s_hi = jnp.sum(jnp.where(lo, 0.0, sq), axis=-1, keepdims=True)
        inv = lax.rsqrt(jnp.where(lo, s_lo, s_hi) * (1.0 / DIFF_QK) + EPS)
        y = (x * inv) * g
        partner = jnp.where(is_x1, pltpu.roll(y, LANE - DIFF_ROT // 2, 1), pltpu.roll(y, DIFF_ROT // 2, 1))
        out = y * cd + partner * sd
        return out * scale if scale != 1.0 else out

    gdq, gdk = gdq_ref[...], gdk_ref[...]
    diff_scale = LOG2E / math.sqrt(DIFF_QK)
    for h in range(DIFF_HEADS):
        sl = slice(h * LANE, (h + 1) * LANE)
        qd_ref[:, sl] = prep_diff(dq_ref[:, sl].astype(F32), gdq, diff_scale).astype(BF16)
        kd_ref[:, sl] = prep_diff(dk_ref[:, sl].astype(F32), gdk, 1.0).astype(BF16)


def _qkv_prep(proj, glq, glkv, wq, wkv, gqn, gqr, gkn, gkr, gdq, gdk, cm, sm, cd, sd, *, seq, tm=256):
    t = proj.shape[0]
    spt = seq // tm
    row = lambda w, c: pl.BlockSpec((tm, w), lambda i, c=c: (i, c))
    tab = pl.BlockSpec((tm, LANE), lambda i: (i % spt, 0))
    vec = lambda n: _resident((1, n))
    return pl.pallas_call(
        _prep_kernel,
        name="qkv_prep",
        grid=(t // tm,),
        in_specs=[row(512, 0), row(512, 1), row(LANE, 4096 // LANE), row(1024, 1), row(1024, 2),
                  vec(Q_LORA), vec(KV_LORA), _resident(wq.shape), _resident(wkv.shape),
                  vec(LANE), vec(LANE), vec(LANE), vec(LANE), vec(LANE), vec(LANE),
                  tab, tab, tab, tab],
        out_specs=[pl.BlockSpec((tm, MLA_HEADS * HEAD_W), lambda i: (i, 0)),
                   pl.BlockSpec((tm, MLA_HEADS * HEAD_W), lambda i: (i, 0)),
                   pl.BlockSpec((tm, MLA_HEADS * MLA_V), lambda i: (i, 0)),
                   pl.BlockSpec((tm, DIFF_HEADS * LANE), lambda i: (i, 0)),
                   pl.BlockSpec((tm, DIFF_HEADS * LANE), lambda i: (i, 0))],
        out_shape=[jax.ShapeDtypeStruct((t, MLA_HEADS * HEAD_W), BF16),
                   jax.ShapeDtypeStruct((t, MLA_HEADS * HEAD_W), BF16),
                   jax.ShapeDtypeStruct((t, MLA_HEADS * MLA_V), BF16),
                   jax.ShapeDtypeStruct((t, DIFF_HEADS * LANE), BF16),
                   jax.ShapeDtypeStruct((t, DIFF_HEADS * LANE), BF16)],
        compiler_params=_params(("arbitrary",)),
    )(proj, proj, proj, proj, proj, glq, glkv, wq, wkv, gqn, gqr, gkn, gkr, gdq, gdk, cm, sm, cd, sd)


_NT = (((1,), (1,)), ((), ()))


def _causal_keep(tq):
    row = lax.broadcasted_iota(jnp.int32, (tq, tq), 0)
    col = lax.broadcasted_iota(jnp.int32, (tq, tq), 1)
    return col <= row


def _scores(q, k_ref, r0, tq, keep):
    s_d = lax.dot_general(q, k_ref[r0:r0 + tq, :], _NT, preferred_element_type=F32)
    s_d = jnp.where(keep, s_d, -jnp.inf)
    s_o = None
    if r0 > 0:
        s_o = lax.dot_general(q, k_ref[0:r0, :], _NT, preferred_element_type=F32)
    return s_o, s_d


def _softmax_pieces(s):
    s_o, s_d = s
    m = jnp.max(s_d, axis=-1, keepdims=True)
    if s_o is None:
        p_d = jnp.exp2(s_d - m)
        return None, p_d, jnp.sum(p_d, axis=-1, keepdims=True)
    m = jnp.maximum(m, jnp.max(s_o, axis=-1, keepdims=True))
    p_o = jnp.exp2(s_o - m)
    p_d = jnp.exp2(s_d - m)
    return p_o, p_d, jnp.sum(p_o, axis=-1, keepdims=True) + jnp.sum(p_d, axis=-1, keepdims=True)


def _pv(p_o, p_d, v_ref, r0, tq):
    o = jnp.dot(p_d.astype(BF16), v_ref[r0:r0 + tq, :], preferred_element_type=F32)
    if p_o is not None:
        o = o + jnp.dot(p_o.astype(BF16), v_ref[0:r0, :], preferred_element_type=F32)
    return o


def _mla_attn_kernel(q_ref, k_ref, v_ref, g_ref, o_ref, *, tq):
    seq = q_ref.shape[0]
    keep = _causal_keep(tq)
    g = g_ref[...]
    starts = list(range(0, seq, tq))

    def scores(r0):
        return _scores(q_ref[r0:r0 + tq, :], k_ref, r0, tq, keep)

    s_next = scores(starts[0])
    for i, r0 in enumerate(starts):
        s = s_next
        if i + 1 < len(starts):
            s_next = scores(starts[i + 1])
        p_o, p_d, l = _softmax_pieces(s)
        o = _pv(p_o, p_d, v_ref, r0, tq) / l
        o_ref[r0:r0 + tq, :] = ((o * _inv_rms(o, MLA_V)) * g).astype(BF16)


def _diff_attn_kernel(q_ref, k_ref, v_ref, g_ref, lam_ref, o_ref, *, tq):
    seq = q_ref.shape[0]
    keep = _causal_keep(tq)
    g = g_ref[...]
    lp = lam_ref[...]
    lam = (jnp.exp(jnp.sum(lp[0:1, :] * lp[1:2, :], axis=-1, keepdims=True))
           - jnp.exp(jnp.sum(lp[2:3, :] * lp[3:4, :], axis=-1, keepdims=True))
           + LAMBDA_INIT)
    lane = lax.broadcasted_iota(jnp.int32, (tq, LANE), 1)
    first = lane < DIFF_QK
    starts = list(range(0, seq, tq))

    def scores(r0):
        q = q_ref[r0:r0 + tq, :]
        q1 = jnp.where(first, q, jnp.zeros_like(q))
        q2 = jnp.where(first, jnp.zeros_like(q), q)
        return _scores(q1, k_ref, r0, tq, keep), _scores(q2, k_ref, r0, tq, keep)

    s_next = scores(starts[0])
    for i, r0 in enumerate(starts):
        s1, s2 = s_next
        if i + 1 < len(starts):
            s_next = scores(starts[i + 1])
        p1_o, p1_d, l1 = _softmax_pieces(s1)
        p2_o, p2_d, l2 = _softmax_pieces(s2)
        c = lam * (l1 / l2)
        a_o = None if p1_o is None else p1_o - c * p2_o
        o = _pv(a_o, p1_d - c * p2_d, v_ref, r0, tq) / l1
        o_ref[r0:r0 + tq, :] = (((o * _inv_rms(o, DIFF_V)) * g) * (1.0 - LAMBDA_INIT)).astype(BF16)


def _head_spec(seq, width, col0=0):
    return pl.BlockSpec((seq, width), lambda b, h: (b, col0 + h))


def _mla_attn(qm, km, vm, g, *, batch, seq, tq=256):
    t = qm.shape[0]
    return pl.pallas_call(
        functools.partial(_mla_attn_kernel, tq=tq),
        name="mla_attn",
        grid=(batch, MLA_HEADS),
        in_specs=[_head_spec(seq, HEAD_W), _head_spec(seq, HEAD_W), _head_spec(seq, MLA_V),
                  pl.BlockSpec((1, MLA_V), lambda b, h: (0, 0))],
        out_specs=_head_spec(seq, MLA_V),
        out_shape=jax.ShapeDtypeStruct((t, MLA_HEADS * MLA_V), BF16),
        compiler_params=_params(("arbitrary", "arbitrary")),
    )(qm, km, vm, g)


def _diff_attn(qd, kd, proj, g, lam_params, *, batch, seq, tq=256):
    t = qd.shape[0]
    dv_col0 = 3072 // LANE
    return pl.pallas_call(
        functools.partial(_diff_attn_kernel, tq=tq),
        name="diff_attn",
        grid=(batch, DIFF_HEADS),
        in_specs=[_head_spec(seq, LANE), _head_spec(seq, LANE), _head_spec(seq, DIFF_V, dv_col0),
                  pl.BlockSpec((1, DIFF_V), lambda b, h: (0, 0)),
                  pl.BlockSpec((4, DIFF_QK), lambda b, h: (0, 0))],
        out_specs=_head_spec(seq, DIFF_V),
        out_shape=jax.ShapeDtypeStruct((t, DIFF_HEADS * DIFF_V), BF16),
        compiler_params=_params(("arbitrary", "arbitrary")),
    )(qd, kd, proj, g, lam_params)


def _out_proj_kernel(x_ref, om_ref, od_ref, w_ref, o_ref):
    km = om_ref.shape[1]
    acc = jnp.dot(om_ref[...], w_ref[0:km, :], preferred_element_type=F32)
    acc = acc + jnp.dot(od_ref[...], w_ref[km:, :], preferred_element_type=F32)
    o_ref[...] = x_ref[...] + acc


def _out_proj(x2d, om, od, w, *, tm=512):
    t = x2d.shape[0]
    return pl.pallas_call(
        _out_proj_kernel,
        name="out_proj",
        grid=(t // tm,),
        in_specs=[pl.BlockSpec((tm, D_MODEL), lambda i: (i, 0)),
                  pl.BlockSpec((tm, om.shape[1]), lambda i: (i, 0)),
                  pl.BlockSpec((tm, od.shape[1]), lambda i: (i, 0)),
                  _resident(w.shape)],
        out_specs=pl.BlockSpec((tm, D_MODEL), lambda i: (i, 0)),
        out_shape=jax.ShapeDtypeStruct((t, D_MODEL), F32),
        compiler_params=_params(("arbitrary",)),
    )(x2d, om, od, w)


def _ffn_kernel(x_ref, g_ref, wg_ref, wu_ref, wd_ref, o_ref, h_ref, *, out_chunk):
    @pl.when(pl.program_id(1) == 0)
    def _():
        x = x_ref[...]
        h_ref[...] = ((x * _inv_rms(x, D_MODEL)) * g_ref[...]).astype(BF16)
        o_ref[...] = x

    h = h_ref[...]
    gate = jnp.dot(h, wg_ref[...], preferred_element_type=F32)
    up = jnp.dot(h, wu_ref[...], preferred_element_type=F32)
    act = (gate * jax.nn.sigmoid(gate) * up).astype(BF16)
    for c0 in range(0, o_ref.shape[1], out_chunk):
        o_ref[:, c0:c0 + out_chunk] += jnp.dot(act, wd_ref[:, c0:c0 + out_chunk],
                                               preferred_element_type=F32)


def _ffn(x1, g, wg, wu, wd, *, tm=512, tf=512):
    t = x1.shape[0]
    return pl.pallas_call(
        functools.partial(_ffn_kernel, out_chunk=512),
        name="ffn",
        grid=(t // tm, D_FF // tf),
        in_specs=[pl.BlockSpec((tm, D_MODEL), lambda i, f: (i, 0)),
                  pl.BlockSpec((1, D_MODEL), lambda i, f: (0, 0)),
                  pl.BlockSpec((D_MODEL, tf), lambda i, f: (0, f)),
                  pl.BlockSpec((D_MODEL, tf), lambda i, f: (0, f)),
                  pl.BlockSpec((tf, D_MODEL), lambda i, f: (f, 0))],
        out_specs=pl.BlockSpec((tm, D_MODEL), lambda i, f: (i, 0)),
        out_shape=jax.ShapeDtypeStruct((t, D_MODEL), F32),
        scratch_shapes=[pltpu.VMEM((tm, D_MODEL), BF16)],
        compiler_params=_params(("arbitrary", "arbitrary")),
    )(x1, g, wg, wu, wd)


def _rope_tables(seq):
    pos = jnp.arange(seq, dtype=jnp.int32).astype(F32)

    def angles(r):
        freqs = 1.0 / (ROPE_THETA ** (jnp.arange(0, r, 2, dtype=F32) / r))
        return pos[:, None] * freqs[None, :]

    am = angles(MLA_ROPE)
    cos, sin = jnp.cos(am), jnp.sin(am)
    pad = LANE - MLA_ROPE
    cm = jnp.concatenate([cos, cos, jnp.ones((seq, pad), F32)], axis=-1)
    sm = jnp.concatenate([-sin, sin, jnp.zeros((seq, pad), F32)], axis=-1)

    ad = angles(DIFF_ROT)
    cos, sin = jnp.cos(ad), jnp.sin(ad)
    rest = DIFF_QK - DIFF_ROT
    cd = jnp.concatenate([cos, cos, jnp.ones((seq, rest), F32)], axis=-1)
    sd = jnp.concatenate([-sin, sin, jnp.zeros((seq, rest), F32)], axis=-1)
    return cm, sm, jnp.tile(cd, (1, 2)), jnp.tile(sd, (1, 2))


def _pad_lanes(v, width):
    return jnp.pad(v, ((0, 0), (0, width - v.shape[-1])))


def kernel(x, attn_norm, w_in, q_latent_norm, w_q_up, kv_latent_norm, w_kv_up, mla_q_norm, mla_k_norm, mla_out_norm, diff_q_norm, diff_k_norm, lambda_q1, lambda_k1, lambda_q2, lambda_k2, diff_out_norm, w_o, ffn_norm, w_gate, w_up, w_down):
    batch, seq, d = x.shape
    assert d == D_MODEL and attn_norm.shape[0] == 1
    t = batch * seq
    x2d = x.reshape(t, d)
    l = 0

    wi = _regroup_w_in(w_in[l])
    wq = w_q_up[l].astype(BF16).reshape(Q_LORA, MLA_HEADS, MLA_QK)
    wq = jnp.concatenate(
        [wq[:, :, :MLA_NOPE].reshape(Q_LORA, -1),
         jnp.pad(wq[:, :, MLA_NOPE:], ((0, 0), (0, 0), (0, LANE - MLA_ROPE))).reshape(Q_LORA, -1)], axis=1)
    wkv = w_kv_up[l].astype(BF16).reshape(KV_LORA, MLA_HEADS, MLA_NOPE + MLA_V)
    wkv = jnp.concatenate([wkv[:, :, :MLA_NOPE].reshape(KV_LORA, -1),
                           wkv[:, :, MLA_NOPE:].reshape(KV_LORA, -1)], axis=1)

    gqn, gqr = mla_q_norm[l:l + 1, :MLA_NOPE], _pad_lanes(mla_q_norm[l:l + 1, MLA_NOPE:], LANE)
    gkn, gkr = mla_k_norm[l:l + 1, :MLA_NOPE], _pad_lanes(mla_k_norm[l:l + 1, MLA_NOPE:], LANE)
    gdq = jnp.tile(diff_q_norm[l:l + 1], (1, 2))
    gdk = jnp.tile(diff_k_norm[l:l + 1], (1, 2))
    lam_params = jnp.concatenate([lambda_q1[l:l + 1], lambda_k1[l:l + 1],
                                  lambda_q2[l:l + 1], lambda_k2[l:l + 1]], axis=0)
    cm, sm, cd, sd = _rope_tables(seq)

    proj = _in_proj(x2d, attn_norm[l:l + 1], wi)
    qm, km, vm, qd, kd = _qkv_prep(proj, q_latent_norm[l:l + 1], kv_latent_norm[l:l + 1], wq, wkv,
                                   gqn, gqr, gkn, gkr, gdq, gdk, cm, sm, cd, sd, seq=seq)
    o_mla = _mla_attn(qm, km, vm, mla_out_norm[l:l + 1], batch=batch, seq=seq)
    o_diff = _diff_attn(qd, kd, proj, diff_out_norm[l:l + 1], lam_params, batch=batch, seq=seq)
    x1 = _out_proj(x2d, o_mla, o_diff, w_o[l].astype(BF16))
    out = _ffn(x1, ffn_norm[l:l + 1], w_gate[l].astype(BF16), w_up[l].astype(BF16), w_down[l].astype(BF16))
    return out.reshape(batch, seq, d)
```

```python
import functools
import math

import jax
import jax.numpy as jnp
from jax import lax
from jax.experimental import pallas as pl
from jax.experimental.pallas import tpu as pltpu

F32 = jnp.float32
BF16 = jnp.bfloat16

D_MODEL = 2048
MLA_HEADS = 8
MLA_NOPE = 128
MLA_ROPE = 64
MLA_QK = MLA_NOPE + MLA_ROPE
MLA_V = 128
Q_LORA = 512
KV_LORA = 512
DIFF_HEADS = 8
DIFF_QK = 64
DIFF_ROT = 16
DIFF_V = 128
ROPE_THETA = 500000.0
D_FF = 5632
EPS = 1e-6
LAMBDA_INIT = 0.8 - 0.6 * math.exp(-0.3 * 0)
LOG2E = math.log2(math.e)

LANE = 128
HEAD_W = 2 * LANE
VMEM_LIMIT = 56 * 1024 * 1024

O_CKV = Q_LORA
O_KPE = O_CKV + KV_LORA
O_DQ = O_KPE + MLA_ROPE
O_DK = O_DQ + DIFF_HEADS * 2 * DIFF_QK
O_DV = O_DK + DIFF_HEADS * 2 * DIFF_QK
IN_COLS = O_DV + DIFF_HEADS * DIFF_V

_NT = (((1,), (1,)), ((), ()))


def _params(sem):
    return pltpu.CompilerParams(dimension_semantics=sem, vmem_limit_bytes=VMEM_LIMIT)


def _resident(shape):
    return pl.BlockSpec(shape, lambda *_: (0,) * len(shape), pipeline_mode=pl.Buffered(1))


def _inv_rms(xf, n):
    return lax.rsqrt(jnp.sum(xf * xf, axis=-1, keepdims=True) * (1.0 / n) + EPS)


def _dot_nt(a, b):
    return lax.dot_general(a, b, _NT, preferred_element_type=F32)


def _qkv_kernel(x_ref, ga_ref, wt_ref, wkpe_ref, glq_ref, glkv_ref, wq_ref, wkv_ref,
                gqn_ref, gqr_ref, gkn_ref, gkr_ref, gdq_ref, gdk_ref,
                cm_ref, sm_ref, cd_ref, sd_ref,
                qm_ref, km_ref, vm_ref, qd_ref, kd_ref, vd_ref):
    tm = x_ref.shape[0]
    cm, sm, cd, sd = cm_ref[...], sm_ref[...], cd_ref[...], sd_ref[...]
    x = x_ref[...]
    h = ((x * _inv_rms(x, D_MODEL)) * ga_ref[...]).astype(BF16)

    lane = lax.broadcasted_iota(jnp.int32, (tm, LANE), 1)
    lo = lane < DIFF_QK
    is_x1 = (lane % DIFF_QK) < (DIFF_ROT // 2)

    def prep_diff(xh, g, scale):
        sq = xh * xh
        s_lo = jnp.sum(jnp.where(lo, sq, 0.0), axis=-1, keepdims=True)
        s_hi = jnp.sum(jnp.where(lo, 0.0, sq), axis=-1, keepdims=True)
        inv = lax.rsqrt(jnp.where(lo, s_lo, s_hi) * (1.0 / DIFF_QK) + EPS)
        y = (xh * inv) * g
        partner = jnp.where(is_x1, pltpu.roll(y, LANE - DIFF_ROT // 2, 1), pltpu.roll(y, DIFF_ROT // 2, 1))
        out = y * cd + partner * sd
        return out * scale if scale != 1.0 else out

    def rope_mla(y):
        return y * cm + pltpu.roll(y, MLA_ROPE, 1) * sm

    half = DIFF_HEADS // 2 * LANE
    gdq, gdk = gdq_ref[...], gdk_ref[...]
    diff_scale = LOG2E / math.sqrt(DIFF_QK)

    def diff_heads(src, g, scale, dst_ref, c0):
        for j in range(DIFF_HEADS // 2):
            dst_ref[:, c0 + j * LANE:c0 + (j + 1) * LANE] = prep_diff(
                src[:, j * LANE:(j + 1) * LANE], g, scale).astype(BF16)

    dq0 = _dot_nt(h, wt_ref[O_DQ:O_DQ + half, :])
    dq1 = _dot_nt(h, wt_ref[O_DQ + half:O_DK, :])
    diff_heads(dq0, gdq, diff_scale, qd_ref, 0)
    dk0 = _dot_nt(h, wt_ref[O_DK:O_DK + half, :])
    diff_heads(dq1, gdq, diff_scale, qd_ref, half)
    dk1 = _dot_nt(h, wt_ref[O_DK + half:O_DV, :])
    diff_heads(dk0, gdk, 1.0, kd_ref, 0)
    cq = _dot_nt(h, wt_ref[0:O_CKV, :])
    ckv = _dot_nt(h, wt_ref[O_CKV:O_KPE, :])
    kpe = _dot_nt(h, wkpe_ref[...])
    diff_heads(dk1, gdk, 1.0, kd_ref, half)

    hq = ((cq * _inv_rms(cq, Q_LORA)) * glq_ref[...]).astype(BF16)
    q = jnp.dot(hq, wq_ref[...], preferred_element_type=F32)
    hkv = ((ckv * _inv_rms(ckv, KV_LORA)) * glkv_ref[...]).astype(BF16)
    kv = jnp.dot(hkv, wkv_ref[...], preferred_element_type=F32)
    dv0 = _dot_nt(h, wt_ref[O_DV:O_DV + half, :])
    dv1 = _dot_nt(h, wt_ref[O_DV + half:IN_COLS, :])

    gqn, gqr, gkn = gqn_ref[...], gqr_ref[...], gkn_ref[...]
    mla_scale = LOG2E / math.sqrt(MLA_QK)
    nope_w = MLA_HEADS * LANE
    for hd in range(MLA_HEADS):
        qn = q[:, hd * LANE:(hd + 1) * LANE]
        qr = q[:, nope_w + hd * LANE:nope_w + (hd + 1) * LANE]
        ss = jnp.sum(qn * qn + 0.5 * (qr * qr), axis=-1, keepdims=True)
        sq = lax.rsqrt(ss * (1.0 / MLA_QK) + EPS)
        qm_ref[:, hd * HEAD_W:hd * HEAD_W + LANE] = (((qn * sq) * gqn) * mla_scale).astype(BF16)
        qm_ref[:, hd * HEAD_W + LANE:(hd + 1) * HEAD_W] = (rope_mla((qr * sq) * gqr) * mla_scale).astype(BF16)

    vd_ref[:, 0:half] = dv0.astype(BF16)
    vd_ref[:, half:] = dv1.astype(BF16)

    kpe_sq = 0.5 * (kpe * kpe)
    kpe_rot = rope_mla(kpe * gkr_ref[...])
    for hd in range(MLA_HEADS):
        kn = kv[:, hd * LANE:(hd + 1) * LANE]
        ssk = jnp.sum(kn * kn + kpe_sq, axis=-1, keepdims=True)
        sk = lax.rsqrt(ssk * (1.0 / MLA_QK) + EPS)
        km_ref[:, hd * HEAD_W:hd * HEAD_W + LANE] = ((kn * sk) * gkn).astype(BF16)
        km_ref[:, hd * HEAD_W + LANE:(hd + 1) * HEAD_W] = (kpe_rot * sk).astype(BF16)
        vm_ref[:, hd * LANE:(hd + 1) * LANE] = kv[:, nope_w + hd * LANE:nope_w + (hd + 1) * LANE].astype(BF16)


def _qkv(x2d, ga, wt, wkpe, glq, glkv, wq, wkv, gqn, gqr, gkn, gkr, gdq, gdk, cm, sm, cd, sd, *, seq, tm=256):
    t = x2d.shape[0]
    spt = seq // tm
    tab = pl.BlockSpec((tm, LANE), lambda i: (i % spt, 0))
    vec = lambda n: _resident((1, n))
    widths = [MLA_HEADS * HEAD_W, MLA_HEADS * HEAD_W, MLA_HEADS * MLA_V,
              DIFF_HEADS * LANE, DIFF_HEADS * LANE, DIFF_HEADS * DIFF_V]
    return pl.pallas_call(
        _qkv_kernel,
        name="qkv",
        grid=(t // tm,),
        in_specs=[pl.BlockSpec((tm, D_MODEL), lambda i: (i, 0)), vec(D_MODEL),
                  _resident(wt.shape), _resident(wkpe.shape),
                  vec(Q_LORA), vec(KV_LORA), _resident(wq.shape), _resident(wkv.shape),
                  vec(LANE), vec(LANE), vec(LANE), vec(LANE), vec(LANE), vec(LANE),
                  tab, tab, tab, tab],
        out_specs=[pl.BlockSpec((tm, w), lambda i: (i, 0)) for w in widths],
        out_shape=[jax.ShapeDtypeStruct((t, w), BF16) for w in widths],
        compiler_params=_params(("arbitrary",)),
    )(x2d, ga, wt, wkpe, glq, glkv, wq, wkv, gqn, gqr, gkn, gkr, gdq, gdk, cm, sm, cd, sd)


def _causal_keep(tq):
    row = lax.broadcasted_iota(jnp.int32, (tq, tq), 0)
    col = lax.broadcasted_iota(jnp.int32, (tq, tq), 1)
    return col <= row


def _scores(q, k_ref, r0, tq, keep):
    s_d = _dot_nt(q, k_ref[r0:r0 + tq, :])
    s_d = jnp.where(keep, s_d, -jnp.inf)
    s_o = None
    if r0 > 0:
        s_o = _dot_nt(q, k_ref[0:r0, :])
    return s_o, s_d


def _softmax_pieces(s):
    s_o, s_d = s
    m = jnp.max(s_d, axis=-1, keepdims=True)
    if s_o is None:
        p_d = jnp.exp2(s_d - m)
        return None, p_d, jnp.sum(p_d, axis=-1, keepdims=True)
    m = jnp.maximum(m, jnp.max(s_o, axis=-1, keepdims=True))
    p_o = jnp.exp2(s_o - m)
    p_d = jnp.exp2(s_d - m)
    return p_o, p_d, jnp.sum(p_o, axis=-1, keepdims=True) + jnp.sum(p_d, axis=-1, keepdims=True)


def _pv(p_o, p_d, v_ref, r0, tq):
    o = jnp.dot(p_d.astype(BF16), v_ref[r0:r0 + tq, :], preferred_element_type=F32)
    if p_o is not None:
        o = o + jnp.dot(p_o.astype(BF16), v_ref[0:r0, :], preferred_element_type=F32)
    return o


def _mla_attn_kernel(q_ref, k_ref, v_ref, g_ref, o_ref, *, tq):
    seq = q_ref.shape[0]
    keep = _causal_keep(tq)
    g = g_ref[...]
    starts = list(range(0, seq, tq))

    def scores(r0):
        return _scores(q_ref[r0:r0 + tq, :], k_ref, r0, tq, keep)

    s_next = scores(starts[0])
    for i, r0 in enumerate(starts):
        s = s_next
        if i + 1 < len(starts):
            s_next = scores(starts[i + 1])
        p_o, p_d, l = _softmax_pieces(s)
        o = _pv(p_o, p_d, v_ref, r0, tq) / l
        o_ref[r0:r0 + tq, :] = ((o * _inv_rms(o, MLA_V)) * g).astype(BF16)


def _diff_attn_kernel(q_ref, k_ref, v_ref, g_ref, lam_ref, o_ref, *, tq):
    seq = q_ref.shape[0]
    keep = _causal_keep(tq)
    g = g_ref[...]
    lp = lam_ref[...]
    lam = (jnp.exp(jnp.sum(lp[0:1, :] * lp[1:2, :], axis=-1, keepdims=True))
           - jnp.exp(jnp.sum(lp[2:3, :] * lp[3:4, :], axis=-1, keepdims=True))
           + LAMBDA_INIT)
    lane = lax.broadcasted_iota(jnp.int32, (tq, LANE), 1)
    first = lane < DIFF_QK
    starts = list(range(0, seq, tq))

    def scores(r0):
        q = q_ref[r0:r0 + tq, :]
        q1 = jnp.where(first, q, jnp.zeros_like(q))
        q2 = jnp.where(first, jnp.zeros_like(q), q)
        return _scores(q1, k_ref, r0, tq, keep), _scores(q2, k_ref, r0, tq, keep)

    s_next = scores(starts[0])
    for i, r0 in enumerate(starts):
        s1, s2 = s_next
        if i + 1 < len(starts):
            s_next = scores(starts[i + 1])
        p1_o, p1_d, l1 = _softmax_pieces(s1)
        p2_o, p2_d, l2 = _softmax_pieces(s2)
        c = lam * (l1 / l2)
        a_o = None if p1_o is None else p1_o - c * p2_o
        o = _pv(a_o, p1_d - c * p2_d, v_ref, r0, tq) / l1
        o_ref[r0:r0 + tq, :] = (((o * _inv_rms(o, DIFF_V)) * g) * (1.0 - LAMBDA_INIT)).astype(BF16)


def _head_spec(seq, width):
    return pl.BlockSpec((seq, width), lambda b, h: (b, h))


def _mla_attn(qm, km, vm, g, *, batch, seq, tq=256):
    t = qm.shape[0]
    return pl.pallas_call(
        functools.partial(_mla_attn_kernel, tq=tq),
        name="mla_attn",
        grid=(batch, MLA_HEADS),
        in_specs=[_head_spec(seq, HEAD_W), _head_spec(seq, HEAD_W), _head_spec(seq, MLA_V),
                  pl.BlockSpec((1, MLA_V), lambda b, h: (0, 0))],
        out_specs=_head_spec(seq, MLA_V),
        out_shape=jax.ShapeDtypeStruct((t, MLA_HEADS * MLA_V), BF16),
        compiler_params=_params(("arbitrary", "arbitrary")),
    )(qm, km, vm, g)


def _diff_attn(qd, kd, vd, g, lam_params, *, batch, seq, tq=256):
    t = qd.shape[0]
    return pl.pallas_call(
        functools.partial(_diff_attn_kernel, tq=tq),
        name="diff_attn",
        grid=(batch, DIFF_HEADS),
        in_specs=[_head_spec(seq, LANE), _head_spec(seq, LANE), _head_spec(seq, DIFF_V),
                  pl.BlockSpec((1, DIFF_V), lambda b, h: (0, 0)),
                  pl.BlockSpec((4, DIFF_QK), lambda b, h: (0, 0))],
        out_specs=_head_spec(seq, DIFF_V),
        out_shape=jax.ShapeDtypeStruct((t, DIFF_HEADS * DIFF_V), BF16),
        compiler_params=_params(("arbitrary", "arbitrary")),
    )(qd, kd, vd, g, lam_params)


def _out_proj_kernel(x_ref, om_ref, od_ref, w_ref, o_ref):
    km = om_ref.shape[1]
    acc = jnp.dot(om_ref[...], w_ref[0:km, :], preferred_element_type=F32)
    acc = acc + jnp.dot(od_ref[...], w_ref[km:, :], preferred_element_type=F32)
    o_ref[...] = x_ref[...] + acc


def _out_proj(x2d, om, od, w, *, tm=512):
    t = x2d.shape[0]
    return pl.pallas_call(
        _out_proj_kernel,
        name="out_proj",
        grid=(t // tm,),
        in_specs=[pl.BlockSpec((tm, D_MODEL), lambda i: (i, 0)),
                  pl.BlockSpec((tm, om.shape[1]), lambda i: (i, 0)),
                  pl.BlockSpec((tm, od.shape[1]), lambda i: (i, 0)),
                  _resident(w.shape)],
        out_specs=pl.BlockSpec((tm, D_MODEL), lambda i: (i, 0)),
        out_shape=jax.ShapeDtypeStruct((t, D_MODEL), F32),
        compiler_params=_params(("arbitrary",)),
    )(x2d, om, od, w)


def _ffn_kernel(x_ref, g_ref, wg_ref, wu_ref, wd_ref, o_ref, h_ref, *, out_chunk):
    @pl.when(pl.program_id(1) == 0)
    def _():
        x = x_ref[...]
        h_ref[...] = ((x * _inv_rms(x, D_MODEL)) * g_ref[...]).astype(BF16)
        o_ref[...] = x

    h = h_ref[...]
    gate = jnp.dot(h, wg_ref[...], preferred_element_type=F32)
    up = jnp.dot(h, wu_ref[...], preferred_element_type=F32)
    act = (gate * jax.nn.sigmoid(gate) * up).astype(BF16)
    for c0 in range(0, o_ref.shape[1], out_chunk):
        o_ref[:, c0:c0 + out_chunk] += jnp.dot(act, wd_ref[:, c0:c0 + out_chunk],
                                               preferred_element_type=F32)


def _ffn(x1, g, wg, wu, wd, *, tm=512, tf=512):
    t = x1.shape[0]
    return pl.pallas_call(
        functools.partial(_ffn_kernel, out_chunk=512),
        name="ffn",
        grid=(t // tm, D_FF // tf),
        in_specs=[pl.BlockSpec((tm, D_MODEL), lambda i, f: (i, 0)),
                  pl.BlockSpec((1, D_MODEL), lambda i, f: (0, 0)),
                  pl.BlockSpec((D_MODEL, tf), lambda i, f: (0, f)),
                  pl.BlockSpec((D_MODEL, tf), lambda i, f: (0, f)),
                  pl.BlockSpec((tf, D_MODEL), lambda i, f: (f, 0))],
        out_specs=pl.BlockSpec((tm, D_MODEL), lambda i, f: (i, 0)),
        out_shape=jax.ShapeDtypeStruct((t, D_MODEL), F32),
        scratch_shapes=[pltpu.VMEM((tm, D_MODEL), BF16)],
        compiler_params=_params(("arbitrary", "arbitrary")),
    )(x1, g, wg, wu, wd)


def _rope_tables(seq):
    pos = jnp.arange(seq, dtype=jnp.int32).astype(F32)

    def angles(r):
        freqs = 1.0 / (ROPE_THETA ** (jnp.arange(0, r, 2, dtype=F32) / r))
        return pos[:, None] * freqs[None, :]

    am = angles(MLA_ROPE)
    cos, sin = jnp.cos(am), jnp.sin(am)
    zeros = jnp.zeros((seq, LANE - MLA_ROPE), F32)
    cm = jnp.concatenate([cos, cos, zeros], axis=-1)
    sm = jnp.concatenate([-sin, sin, zeros], axis=-1)

    ad = angles(DIFF_ROT)
    cos, sin = jnp.cos(ad), jnp.sin(ad)
    rest = DIFF_QK - DIFF_ROT
    cd = jnp.concatenate([cos, cos, jnp.ones((seq, rest), F32)], axis=-1)
    sd = jnp.concatenate([-sin, sin, jnp.zeros((seq, rest), F32)], axis=-1)
    return cm, sm, jnp.tile(cd, (1, 2)), jnp.tile(sd, (1, 2))


def _with_partner(v):
    return jnp.concatenate([v, jnp.roll(v, MLA_ROPE // 2, axis=-1)], axis=-1)


def kernel(x, attn_norm, w_in, q_latent_norm, w_q_up, kv_latent_norm, w_kv_up, mla_q_norm, mla_k_norm, mla_out_norm, diff_q_norm, diff_k_norm, lambda_q1, lambda_k1, lambda_q2, lambda_k2, diff_out_norm, w_o, ffn_norm, w_gate, w_up, w_down):
    batch, seq, d = x.shape
    assert d == D_MODEL and attn_norm.shape[0] == 1
    t = batch * seq
    x2d = x.reshape(t, d)
    l = 0

    wt = jnp.swapaxes(w_in[l], 0, 1).astype(BF16)
    wkpe = _with_partner(wt[O_KPE:O_DQ].T).T
    wq = w_q_up[l].astype(BF16).reshape(Q_LORA, MLA_HEADS, MLA_QK)
    wq = jnp.concatenate([wq[:, :, :MLA_NOPE].reshape(Q_LORA, -1),
                          _with_partner(wq[:, :, MLA_NOPE:]).reshape(Q_LORA, -1)], axis=1)
    wkv = w_kv_up[l].astype(BF16).reshape(KV_LORA, MLA_HEADS, MLA_NOPE + MLA_V)
    wkv = jnp.concatenate([wkv[:, :, :MLA_NOPE].reshape(KV_LORA, -1),
                           wkv[:, :, MLA_NOPE:].reshape(KV_LORA, -1)], axis=1)

    gqn, gqr = mla_q_norm[l:l + 1, :MLA_NOPE], _with_partner(mla_q_norm[l:l + 1, MLA_NOPE:])
    gkn, gkr = mla_k_norm[l:l + 1, :MLA_NOPE], _with_partner(mla_k_norm[l:l + 1, MLA_NOPE:])
    gdq = jnp.tile(diff_q_norm[l:l + 1], (1, 2))
    gdk = jnp.tile(diff_k_norm[l:l + 1], (1, 2))
    lam_params = jnp.concatenate([lambda_q1[l:l + 1], lambda_k1[l:l + 1],
                                  lambda_q2[l:l + 1], lambda_k2[l:l + 1]], axis=0)
    cm, sm, cd, sd = _rope_tables(seq)

    qm, km, vm, qd, kd, vd = _qkv(x2d, attn_norm[l:l + 1], wt, wkpe,
                                  q_latent_norm[l:l + 1], kv_latent_norm[l:l + 1], wq, wkv,
                                  gqn, gqr, gkn, gkr, gdq, gdk, cm, sm, cd, sd, seq=seq)
    o_mla = _mla_attn(qm, km, vm, mla_out_norm[l:l + 1], batch=batch, seq=seq)
    o_diff = _diff_attn(qd, kd, vd, diff_out_norm[l:l + 1], lam_params, batch=batch, seq=seq)
    x1 = _out_proj(x2d, o_mla, o_diff, w_o[l].astype(BF16))
    out = _ffn(x1, ffn_norm[l:l + 1], w_gate[l].astype(BF16), w_up[l].astype(BF16), w_down[l].astype(BF16))
    return out.reshape(batch, seq, d)
```

```python
import functools
import math

import jax
import jax.numpy as jnp
from jax import lax
from jax.experimental import pallas as pl
from jax.experimental.pallas import tpu as pltpu

F32 = jnp.float32
BF16 = jnp.bfloat16

D_MODEL = 2048
MLA_HEADS = 8
MLA_NOPE = 128
MLA_ROPE = 64
MLA_QK = MLA_NOPE + MLA_ROPE
MLA_V = 128
Q_LORA = 512
KV_LORA = 512
DIFF_HEADS = 8
DIFF_QK = 64
DIFF_ROT = 16
DIFF_V = 128
ROPE_THETA = 500000.0
D_FF = 5632
EPS = 1e-6
LAMBDA_INIT = 0.8 - 0.6 * math.exp(-0.3 * 0)
LOG2E = math.log2(math.e)

LANE = 128
HEAD_W = 2 * LANE
VMEM_LIMIT = 56 * 1024 * 1024

O_CKV = Q_LORA
O_KPE = O_CKV + KV_LORA
O_DQ = O_KPE + MLA_ROPE
O_DK = O_DQ + DIFF_HEADS * 2 * DIFF_QK
O_DV = O_DK + DIFF_HEADS * 2 * DIFF_QK
IN_COLS = O_DV + DIFF_HEADS * DIFF_V

_NT = (((1,), (1,)), ((), ()))


def _params(sem):
    return pltpu.CompilerParams(dimension_semantics=sem, vmem_limit_bytes=VMEM_LIMIT)


def _resident(shape):
    return pl.BlockSpec(shape, lambda *_: (0,) * len(shape), pipeline_mode=pl.Buffered(1))


def _inv_rms(xf, n):
    return lax.rsqrt(jnp.sum(xf * xf, axis=-1, keepdims=True) * (1.0 / n) + EPS)


def _dot_nt(a, b):
    return lax.dot_general(a, b, _NT, preferred_element_type=F32)


def _qkv_kernel(x_ref, ga_ref, wt_ref, wkpe_ref, glq_ref, glkv_ref, wq_ref, wkv_ref,
                gqn_ref, gqr_ref, gkn_ref, gkr_ref, gdq_ref, gdk_ref,
                cm_ref, sm_ref, cd_ref, sd_ref,
                wo_ref, wg_ref, wu_ref, wd_ref,
                qm_ref, km_ref, vm_ref, qd_ref, kd_ref, vd_ref,
                wo16_ref, wg16_ref, wu16_ref, wd16_ref):
    for src, dst in ((wo_ref, wo16_ref), (wg_ref, wg16_ref), (wu_ref, wu16_ref), (wd_ref, wd16_ref)):
        dst[...] = src[...].astype(BF16)

    tm = x_ref.shape[0]
    cm, sm, cd, sd = cm_ref[...], sm_ref[...], cd_ref[...], sd_ref[...]
    x = x_ref[...]
    h = ((x * _inv_rms(x, D_MODEL)) * ga_ref[...]).astype(BF16)

    lane = lax.broadcasted_iota(jnp.int32, (tm, LANE), 1)
    lo = lane < DIFF_QK
    is_x1 = (lane % DIFF_QK) < (DIFF_ROT // 2)

    def prep_diff(xh, g, scale):
        sq = xh * xh
        s_lo = jnp.sum(jnp.where(lo, sq, 0.0), axis=-1, keepdims=True)
        s_hi = jnp.sum(jnp.where(lo, 0.0, sq), axis=-1, keepdims=True)
        inv = lax.rsqrt(jnp.where(lo, s_lo, s_hi) * (1.0 / DIFF_QK) + EPS)
        y = (xh * inv) * g
        partner = jnp.where(is_x1, pltpu.roll(y, LANE - DIFF_ROT // 2, 1), pltpu.roll(y, DIFF_ROT // 2, 1))
        out = y * cd + partner * sd
        return out * scale if scale != 1.0 else out

    def rope_mla(y):
        return y * cm + pltpu.roll(y, MLA_ROPE, 1) * sm

    half = DIFF_HEADS // 2 * LANE
    gdq, gdk = gdq_ref[...], gdk_ref[...]
    diff_scale = LOG2E / math.sqrt(DIFF_QK)

    def diff_heads(src, g, scale, dst_ref, c0):
        for j in range(DIFF_HEADS // 2):
            dst_ref[:, c0 + j * LANE:c0 + (j + 1) * LANE] = prep_diff(
                src[:, j * LANE:(j + 1) * LANE], g, scale).astype(BF16)

    dq0 = _dot_nt(h, wt_ref[O_DQ:O_DQ + half, :])
    dq1 = _dot_nt(h, wt_ref[O_DQ + half:O_DK, :])
    diff_heads(dq0, gdq, diff_scale, qd_ref, 0)
    dk0 = _dot_nt(h, wt_ref[O_DK:O_DK + half, :])
    diff_heads(dq1, gdq, diff_scale, qd_ref, half)
    dk1 = _dot_nt(h, wt_ref[O_DK + half:O_DV, :])
    diff_heads(dk0, gdk, 1.0, kd_ref, 0)
    cq = _dot_nt(h, wt_ref[0:O_CKV, :])
    ckv = _dot_nt(h, wt_ref[O_CKV:O_KPE, :])
    kpe = _dot_nt(h, wkpe_ref[...])
    diff_heads(dk1, gdk, 1.0, kd_ref, half)

    hq = ((cq * _inv_rms(cq, Q_LORA)) * glq_ref[...]).astype(BF16)
    q = jnp.dot(hq, wq_ref[...], preferred_element_type=F32)
    hkv = ((ckv * _inv_rms(ckv, KV_LORA)) * glkv_ref[...]).astype(BF16)
    kv = jnp.dot(hkv, wkv_ref[...], preferred_element_type=F32)
    dv0 = _dot_nt(h, wt_ref[O_DV:O_DV + half, :])
    dv1 = _dot_nt(h, wt_ref[O_DV + half:IN_COLS, :])

    gqn, gqr, gkn = gqn_ref[...], gqr_ref[...], gkn_ref[...]
    mla_scale = LOG2E / math.sqrt(MLA_QK)
    nope_w = MLA_HEADS * LANE
    for hd in range(MLA_HEADS):
        qn = q[:, hd * LANE:(hd + 1) * LANE]
        qr = q[:, nope_w + hd * LANE:nope_w + (hd + 1) * LANE]
        ss = jnp.sum(qn * qn + 0.5 * (qr * qr), axis=-1, keepdims=True)
        sq = lax.rsqrt(ss * (1.0 / MLA_QK) + EPS)
        qm_ref[:, hd * HEAD_W:hd * HEAD_W + LANE] = (((qn * sq) * gqn) * mla_scale).astype(BF16)
        qm_ref[:, hd * HEAD_W + LANE:(hd + 1) * HEAD_W] = (rope_mla((qr * sq) * gqr) * mla_scale).astype(BF16)

    vd_ref[:, 0:half] = dv0.astype(BF16)
    vd_ref[:, half:] = dv1.astype(BF16)

    kpe_sq = 0.5 * (kpe * kpe)
    kpe_rot = rope_mla(kpe * gkr_ref[...])
    for hd in range(MLA_HEADS):
        kn = kv[:, hd * LANE:(hd + 1) * LANE]
        ssk = jnp.sum(kn * kn + kpe_sq, axis=-1, keepdims=True)
        sk = lax.rsqrt(ssk * (1.0 / MLA_QK) + EPS)
        km_ref[:, hd * HEAD_W:hd * HEAD_W + LANE] = ((kn * sk) * gkn).astype(BF16)
        km_ref[:, hd * HEAD_W + LANE:(hd + 1) * HEAD_W] = (kpe_rot * sk).astype(BF16)
        vm_ref[:, hd * LANE:(hd + 1) * LANE] = kv[:, nope_w + hd * LANE:nope_w + (hd + 1) * LANE].astype(BF16)


def _qkv(x2d, ga, wt, wkpe, glq, glkv, wq, wkv, gqn, gqr, gkn, gkr, gdq, gdk, cm, sm, cd, sd,
         later_weights, *, seq, tm=256):
    t = x2d.shape[0]
    steps = t // tm
    spt = seq // tm
    tab = pl.BlockSpec((tm, LANE), lambda i: (i % spt, 0))
    vec = lambda n: _resident((1, n))
    widths = [MLA_HEADS * HEAD_W, MLA_HEADS * HEAD_W, MLA_HEADS * MLA_V,
              DIFF_HEADS * LANE, DIFF_HEADS * LANE, DIFF_HEADS * DIFF_V]
    slabs = [pl.BlockSpec((w.shape[0] // steps, w.shape[1]), lambda i: (i, 0)) for w in later_weights]
    assert all(w.shape[0] % (16 * steps) == 0 for w in later_weights)
    outs = pl.pallas_call(
        _qkv_kernel,
        name="qkv",
        grid=(steps,),
        in_specs=[pl.BlockSpec((tm, D_MODEL), lambda i: (i, 0)), vec(D_MODEL),
                  _resident(wt.shape), _resident(wkpe.shape),
                  vec(Q_LORA), vec(KV_LORA), _resident(wq.shape), _resident(wkv.shape),
                  vec(LANE), vec(LANE), vec(LANE), vec(LANE), vec(LANE), vec(LANE),
                  tab, tab, tab, tab] + slabs,
        out_specs=[pl.BlockSpec((tm, w), lambda i: (i, 0)) for w in widths] + slabs,
        out_shape=([jax.ShapeDtypeStruct((t, w), BF16) for w in widths]
                   + [jax.ShapeDtypeStruct(w.shape, BF16) for w in later_weights]),
        compiler_params=_params(("arbitrary",)),
    )(x2d, ga, wt, wkpe, glq, glkv, wq, wkv, gqn, gqr, gkn, gkr, gdq, gdk, cm, sm, cd, sd, *later_weights)
    return outs[:len(widths)], outs[len(widths):]


def _causal_keep(tq):
    row = lax.broadcasted_iota(jnp.int32, (tq, tq), 0)
    col = lax.broadcasted_iota(jnp.int32, (tq, tq), 1)
    return col <= row


def _scores(q, k_ref, r0, tq, keep):
    s_d = _dot_nt(q, k_ref[r0:r0 + tq, :])
    s_d = jnp.where(keep, s_d, -jnp.inf)
    s_o = None
    if r0 > 0:
        s_o = _dot_nt(q, k_ref[0:r0, :])
    return s_o, s_d


def _softmax_pieces(s):
    s_o, s_d = s
    m = jnp.max(s_d, axis=-1, keepdims=True)
    if s_o is None:
        p_d = jnp.exp2(s_d - m)
        return None, p_d, jnp.sum(p_d, axis=-1, keepdims=True)
    m = jnp.maximum(m, jnp.max(s_o, axis=-1, keepdims=True))
    p_o = jnp.exp2(s_o - m)
    p_d = jnp.exp2(s_d - m)
    return p_o, p_d, jnp.sum(p_o, axis=-1, keepdims=True) + jnp.sum(p_d, axis=-1, keepdims=True)


def _pv(p_o, p_d, v_ref, r0, tq):
    o = jnp.dot(p_d.astype(BF16), v_ref[r0:r0 + tq, :], preferred_element_type=F32)
    if p_o is not None:
        o = o + jnp.dot(p_o.astype(BF16), v_ref[0:r0, :], preferred_element_type=F32)
    return o


def _mla_attn_kernel(q_ref, k_ref, v_ref, g_ref, o_ref, *, tq):
    seq = q_ref.shape[0]
    keep = _causal_keep(tq)
    g = g_ref[...]
    starts = list(range(0, seq, tq))

    def scores(r0):
        return _scores(q_ref[r0:r0 + tq, :], k_ref, r0, tq, keep)

    s_next = scores(starts[0])
    for i, r0 in enumerate(starts):
        s = s_next
        if i + 1 < len(starts):
            s_next = scores(starts[i + 1])
        p_o, p_d, l = _softmax_pieces(s)
        o = _pv(p_o, p_d, v_ref, r0, tq) / l
        o_ref[r0:r0 + tq, :] = ((o * _inv_rms(o, MLA_V)) * g).astype(BF16)


def _diff_attn_kernel(q_ref, k_ref, v_ref, g_ref, lam_ref, o_ref, *, tq):
    seq = q_ref.shape[0]
    keep = _causal_keep(tq)
    g = g_ref[...]
    lp = lam_ref[...]
    lam = (jnp.exp(jnp.sum(lp[0:1, :] * lp[1:2, :], axis=-1, keepdims=True))
           - jnp.exp(jnp.sum(lp[2:3, :] * lp[3:4, :], axis=-1, keepdims=True))
           + LAMBDA_INIT)
    lane = lax.broadcasted_iota(jnp.int32, (tq, LANE), 1)
    first = lane < DIFF_QK
    starts = list(range(0, seq, tq))

    def scores(r0):
        q = q_ref[r0:r0 + tq, :]
        q1 = jnp.where(first, q, jnp.zeros_like(q))
        q2 = jnp.where(first, jnp.zeros_like(q), q)
        return _scores(q1, k_ref, r0, tq, keep), _scores(q2, k_ref, r0, tq, keep)

    s_next = scores(starts[0])
    for i, r0 in enumerate(starts):
        s1, s2 = s_next
        if i + 1 < len(starts):
            s_next = scores(starts[i + 1])
        p1_o, p1_d, l1 = _softmax_pieces(s1)
        p2_o, p2_d, l2 = _softmax_pieces(s2)
        c = lam * (l1 / l2)
        a_o = None if p1_o is None else p1_o - c * p2_o
        o = _pv(a_o, p1_d - c * p2_d, v_ref, r0, tq) / l1
        o_ref[r0:r0 + tq, :] = (((o * _inv_rms(o, DIFF_V)) * g) * (1.0 - LAMBDA_INIT)).astype(BF16)


def _head_spec(seq, width):
    return pl.BlockSpec((seq, width), lambda b, h: (b, h))


def _mla_attn(qm, km, vm, g, *, batch, seq, tq=256):
    t = qm.shape[0]
    return pl.pallas_call(
        functools.partial(_mla_attn_kernel, tq=tq),
        name="mla_attn",
        grid=(batch, MLA_HEADS),
        in_specs=[_head_spec(seq, HEAD_W), _head_spec(seq, HEAD_W), _head_spec(seq, MLA_V),
                  pl.BlockSpec((1, MLA_V), lambda b, h: (0, 0))],
        out_specs=_head_spec(seq, MLA_V),
        out_shape=jax.ShapeDtypeStruct((t, MLA_HEADS * MLA_V), BF16),
        compiler_params=_params(("arbitrary", "arbitrary")),
    )(qm, km, vm, g)


def _diff_attn(qd, kd, vd, g, lam_params, *, batch, seq, tq=256):
    t = qd.shape[0]
    return pl.pallas_call(
        functools.partial(_diff_attn_kernel, tq=tq),
        name="diff_attn",
        grid=(batch, DIFF_HEADS),
        in_specs=[_head_spec(seq, LANE), _head_spec(seq, LANE), _head_spec(seq, DIFF_V),
                  pl.BlockSpec((1, DIFF_V), lambda b, h: (0, 0)),
                  pl.BlockSpec((4, DIFF_QK), lambda b, h: (0, 0))],
        out_specs=_head_spec(seq, DIFF_V),
        out_shape=jax.ShapeDtypeStruct((t, DIFF_HEADS * DIFF_V), BF16),
        compiler_params=_params(("arbitrary", "arbitrary")),
    )(qd, kd, vd, g, lam_params)


def _out_proj_kernel(x_ref, om_ref, od_ref, w_ref, o_ref):
    km = om_ref.shape[1]
    acc = jnp.dot(om_ref[...], w_ref[0:km, :], preferred_element_type=F32)
    acc = acc + jnp.dot(od_ref[...], w_ref[km:, :], preferred_element_type=F32)
    o_ref[...] = x_ref[...] + acc


def _out_proj(x2d, om, od, w, *, tm=512):
    t = x2d.shape[0]
    return pl.pallas_call(
        _out_proj_kernel,
        name="out_proj",
        grid=(t // tm,),
        in_specs=[pl.BlockSpec((tm, D_MODEL), lambda i: (i, 0)),
                  pl.BlockSpec((tm, om.shape[1]), lambda i: (i, 0)),
                  pl.BlockSpec((tm, od.shape[1]), lambda i: (i, 0)),
                  _resident(w.shape)],
        out_specs=pl.BlockSpec((tm, D_MODEL), lambda i: (i, 0)),
        out_shape=jax.ShapeDtypeStruct((t, D_MODEL), F32),
        compiler_params=_params(("arbitrary",)),
    )(x2d, om, od, w)


def _ffn_kernel(x_ref, g_ref, wg_ref, wu_ref, wd_ref, o_ref, h_ref, *, out_chunk):
    @pl.when(pl.program_id(1) == 0)
    def _():
        x = x_ref[...]
        h_ref[...] = ((x * _inv_rms(x, D_MODEL)) * g_ref[...]).astype(BF16)
        o_ref[...] = x

    h = h_ref[...]
    gate = jnp.dot(h, wg_ref[...], preferred_element_type=F32)
    up = jnp.dot(h, wu_ref[...], preferred_element_type=F32)
    act = (gate * jax.nn.sigmoid(gate) * up).astype(BF16)
    for c0 in range(0, o_ref.shape[1], out_chunk):
        o_ref[:, c0:c0 + out_chunk] += jnp.dot(act, wd_ref[:, c0:c0 + out_chunk],
                                               preferred_element_type=F32)


def _ffn(x1, g, wg, wu, wd, *, tm=512, tf=512):
    t = x1.shape[0]
    return pl.pallas_call(
        functools.partial(_ffn_kernel, out_chunk=512),
        name="ffn",
        grid=(t // tm, D_FF // tf),
        in_specs=[pl.BlockSpec((tm, D_MODEL), lambda i, f: (i, 0)),
                  pl.BlockSpec((1, D_MODEL), lambda i, f: (0, 0)),
                  pl.BlockSpec((D_MODEL, tf), lambda i, f: (0, f)),
                  pl.BlockSpec((D_MODEL, tf), lambda i, f: (0, f)),
                  pl.BlockSpec((tf, D_MODEL), lambda i, f: (f, 0))],
        out_specs=pl.BlockSpec((tm, D_MODEL), lambda i, f: (i, 0)),
        out_shape=jax.ShapeDtypeStruct((t, D_MODEL), F32),
        scratch_shapes=[pltpu.VMEM((tm, D_MODEL), BF16)],
        compiler_params=_params(("arbitrary", "arbitrary")),
    )(x1, g, wg, wu, wd)


def _rope_tables(seq):
    pos = jnp.arange(seq, dtype=jnp.int32).astype(F32)

    def angles(r):
        freqs = 1.0 / (ROPE_THETA ** (jnp.arange(0, r, 2, dtype=F32) / r))
        return pos[:, None] * freqs[None, :]

    am = angles(MLA_ROPE)
    cos, sin = jnp.cos(am), jnp.sin(am)
    zeros = jnp.zeros((seq, LANE - MLA_ROPE), F32)
    cm = jnp.concatenate([cos, cos, zeros], axis=-1)
    sm = jnp.concatenate([-sin, sin, zeros], axis=-1)

    ad = angles(DIFF_ROT)
    cos, sin = jnp.cos(ad), jnp.sin(ad)
    rest = DIFF_QK - DIFF_ROT
    cd = jnp.concatenate([cos, cos, jnp.ones((seq, rest), F32)], axis=-1)
    sd = jnp.concatenate([-sin, sin, jnp.zeros((seq, rest), F32)], axis=-1)
    return cm, sm, jnp.tile(cd, (1, 2)), jnp.tile(sd, (1, 2))


def _with_partner(v):
    return jnp.concatenate([v, jnp.roll(v, MLA_ROPE // 2, axis=-1)], axis=-1)


def kernel(x, attn_norm, w_in, q_latent_norm, w_q_up, kv_latent_norm, w_kv_up, mla_q_norm, mla_k_norm, mla_out_norm, diff_q_norm, diff_k_norm, lambda_q1, lambda_k1, lambda_q2, lambda_k2, diff_out_norm, w_o, ffn_norm, w_gate, w_up, w_down):
    batch, seq, d = x.shape
    assert d == D_MODEL and attn_norm.shape[0] == 1
    t = batch * seq
    x2d = x.reshape(t, d)
    l = 0

    wt = jnp.swapaxes(w_in[l], 0, 1).astype(BF16)
    wkpe = _with_partner(wt[O_KPE:O_DQ].T).T
    wq = w_q_up[l].astype(BF16).reshape(Q_LORA, MLA_HEADS, MLA_QK)
    wq = jnp.concatenate([wq[:, :, :MLA_NOPE].reshape(Q_LORA, -1),
                          _with_partner(wq[:, :, MLA_NOPE:]).reshape(Q_LORA, -1)], axis=1)
    wkv = w_kv_up[l].astype(BF16).reshape(KV_LORA, MLA_HEADS, MLA_NOPE + MLA_V)
    wkv = jnp.concatenate([wkv[:, :, :MLA_NOPE].reshape(KV_LORA, -1),
                           wkv[:, :, MLA_NOPE:].reshape(KV_LORA, -1)], axis=1)

    gqn, gqr = mla_q_norm[l:l + 1, :MLA_NOPE], _with_partner(mla_q_norm[l:l + 1, MLA_NOPE:])
    gkn, gkr = mla_k_norm[l:l + 1, :MLA_NOPE], _with_partner(mla_k_norm[l:l + 1, MLA_NOPE:])
    gdq = jnp.tile(diff_q_norm[l:l + 1], (1, 2))
    gdk = jnp.tile(diff_k_norm[l:l + 1], (1, 2))
    lam_params = jnp.concatenate([lambda_q1[l:l + 1], lambda_k1[l:l + 1],
                                  lambda_q2[l:l + 1], lambda_k2[l:l + 1]], axis=0)
    cm, sm, cd, sd = _rope_tables(seq)

    (qm, km, vm, qd, kd, vd), (wo16, wg16, wu16, wd16) = _qkv(
        x2d, attn_norm[l:l + 1], wt, wkpe, q_latent_norm[l:l + 1], kv_latent_norm[l:l + 1], wq, wkv,
        gqn, gqr, gkn, gkr, gdq, gdk, cm, sm, cd, sd, (w_o[l], w_gate[l], w_up[l], w_down[l]), seq=seq)
    o_mla = _mla_attn(qm, km, vm, mla_out_norm[l:l + 1], batch=batch, seq=seq)
    o_diff = _diff_attn(qd, kd, vd, diff_out_norm[l:l + 1], lam_params, batch=batch, seq=seq)
    x1 = _out_proj(x2d, o_mla, o_diff, wo16)
    out = _ffn(x1, ffn_norm[l:l + 1], wg16, wu16, wd16)
    return out.reshape(batch, seq, d)
```

```python
import functools
import math

import jax
import jax.numpy as jnp
from jax import lax
from jax.experimental import pallas as pl
from jax.experimental.pallas import tpu as pltpu

F32 = jnp.float32
BF16 = jnp.bfloat16

D_MODEL = 2048
MLA_HEADS = 8
MLA_NOPE = 128
MLA_ROPE = 64
MLA_QK = MLA_NOPE + MLA_ROPE
MLA_V = 128
Q_LORA = 512
KV_LORA = 512
DIFF_HEADS = 8
DIFF_QK = 64
DIFF_ROT = 16
DIFF_V = 128
ROPE_THETA = 500000.0
D_FF = 5632
EPS = 1e-6
LAMBDA_INIT = 0.8 - 0.6 * math.exp(-0.3 * 0)
LOG2E = math.log2(math.e)

LANE = 128
HEAD_W = 2 * LANE
VMEM_LIMIT = 56 * 1024 * 1024

O_CKV = Q_LORA
O_KPE = O_CKV + KV_LORA
O_DQ = O_KPE + MLA_ROPE
O_DK = O_DQ + DIFF_HEADS * 2 * DIFF_QK
O_DV = O_DK + DIFF_HEADS * 2 * DIFF_QK
IN_COLS = O_DV + DIFF_HEADS * DIFF_V

_NT = (((1,), (1,)), ((), ()))


def _params(sem):
    return pltpu.CompilerParams(dimension_semantics=sem, vmem_limit_bytes=VMEM_LIMIT)


def _resident(shape):
    return pl.BlockSpec(shape, lambda *_: (0,) * len(shape), pipeline_mode=pl.Buffered(1))


def _inv_rms(xf, n):
    return lax.rsqrt(jnp.sum(xf * xf, axis=-1, keepdims=True) * (1.0 / n) + EPS)


def _dot_nt(a, b):
    return lax.dot_general(a, b, _NT, preferred_element_type=F32)


def _qkv_kernel(x_ref, ga_ref, wt_ref, wkpe_ref, glq_ref, glkv_ref, wq_ref, wkv_ref,
                gqn_ref, gqr_ref, gkn_ref, gkr_ref, gdq_ref, gdk_ref,
                cm_ref, sm_ref, cd_ref, sd_ref,
                wo_ref, wg_ref, wu_ref, wd_ref,
                qm_ref, km_ref, vm_ref, qd_ref, kd_ref, vd_ref,
                wo16_ref, wg16_ref, wu16_ref, wd16_ref):
    for src, dst in ((wo_ref, wo16_ref), (wg_ref, wg16_ref), (wu_ref, wu16_ref), (wd_ref, wd16_ref)):
        dst[...] = src[...].astype(BF16)

    tm = x_ref.shape[0]
    cm, sm, cd, sd = cm_ref[...], sm_ref[...], cd_ref[...], sd_ref[...]
    x = x_ref[...]
    h = ((x * _inv_rms(x, D_MODEL)) * ga_ref[...]).astype(BF16)

    lane = lax.broadcasted_iota(jnp.int32, (tm, LANE), 1)
    lo = lane < DIFF_QK
    is_x1 = (lane % DIFF_QK) < (DIFF_ROT // 2)

    def prep_diff(xh, g, scale):
        sq = xh * xh
        s_lo = jnp.sum(jnp.where(lo, sq, 0.0), axis=-1, keepdims=True)
        s_hi = jnp.sum(jnp.where(lo, 0.0, sq), axis=-1, keepdims=True)
        inv = lax.rsqrt(jnp.where(lo, s_lo, s_hi) * (1.0 / DIFF_QK) + EPS)
        y = (xh * inv) * g
        partner = jnp.where(is_x1, pltpu.roll(y, LANE - DIFF_ROT // 2, 1), pltpu.roll(y, DIFF_ROT // 2, 1))
        out = y * cd + partner * sd
        return out * scale if scale != 1.0 else out

    def rope_mla(y):
        return y * cm + pltpu.roll(y, MLA_ROPE, 1) * sm

    half = DIFF_HEADS // 2 * LANE
    gdq, gdk = gdq_ref[...], gdk_ref[...]
    diff_scale = LOG2E / math.sqrt(DIFF_QK)

    def diff_heads(src, g, scale, dst_ref, c0):
        for j in range(DIFF_HEADS // 2):
            dst_ref[:, c0 + j * LANE:c0 + (j + 1) * LANE] = prep_diff(
                src[:, j * LANE:(j + 1) * LANE], g, scale).astype(BF16)

    dq0 = _dot_nt(h, wt_ref[O_DQ:O_DQ + half, :])
    dq1 = _dot_nt(h, wt_ref[O_DQ + half:O_DK, :])
    diff_heads(dq0, gdq, diff_scale, qd_ref, 0)
    dk0 = _dot_nt(h, wt_ref[O_DK:O_DK + half, :])
    diff_heads(dq1, gdq, diff_scale, qd_ref, half)
    dk1 = _dot_nt(h, wt_ref[O_DK + half:O_DV, :])
    diff_heads(dk0, gdk, 1.0, kd_ref, 0)
    cq = _dot_nt(h, wt_ref[0:O_CKV, :])
    ckv = _dot_nt(h, wt_ref[O_CKV:O_KPE, :])
    kpe = _dot_nt(h, wkpe_ref[...])
    diff_heads(dk1, gdk, 1.0, kd_ref, half)

    hq = ((cq * _inv_rms(cq, Q_LORA)) * glq_ref[...]).astype(BF16)
    q = jnp.dot(hq, wq_ref[...], preferred_element_type=F32)
    hkv = ((ckv * _inv_rms(ckv, KV_LORA)) * glkv_ref[...]).astype(BF16)
    kv = jnp.dot(hkv, wkv_ref[...], preferred_element_type=F32)
    dv0 = _dot_nt(h, wt_ref[O_DV:O_DV + half, :])
    dv1 = _dot_nt(h, wt_ref[O_DV + half:IN_COLS, :])

    gqn, gqr, gkn = gqn_ref[...], gqr_ref[...], gkn_ref[...]
    mla_scale = LOG2E / math.sqrt(MLA_QK)
    nope_w = MLA_HEADS * LANE
    for hd in range(MLA_HEADS):
        qn = q[:, hd * LANE:(hd + 1) * LANE]
        qr = q[:, nope_w + hd * LANE:nope_w + (hd + 1) * LANE]
        ss = jnp.sum(qn * qn + 0.5 * (qr * qr), axis=-1, keepdims=True)
        sq = lax.rsqrt(ss * (1.0 / MLA_QK) + EPS)
        qm_ref[:, hd * HEAD_W:hd * HEAD_W + LANE] = (((qn * sq) * gqn) * mla_scale).astype(BF16)
        qm_ref[:, hd * HEAD_W + LANE:(hd + 1) * HEAD_W] = (rope_mla((qr * sq) * gqr) * mla_scale).astype(BF16)

    vd_ref[:, 0:half] = dv0.astype(BF16)
    vd_ref[:, half:] = dv1.astype(BF16)

    kpe_sq = 0.5 * (kpe * kpe)
    kpe_rot = rope_mla(kpe * gkr_ref[...])
    for hd in range(MLA_HEADS):
        kn = kv[:, hd * LANE:(hd + 1) * LANE]
        ssk = jnp.sum(kn * kn + kpe_sq, axis=-1, keepdims=True)
        sk = lax.rsqrt(ssk * (1.0 / MLA_QK) + EPS)
        km_ref[:, hd * HEAD_W:hd * HEAD_W + LANE] = ((kn * sk) * gkn).astype(BF16)
        km_ref[:, hd * HEAD_W + LANE:(hd + 1) * HEAD_W] = (kpe_rot * sk).astype(BF16)
        vm_ref[:, hd * HEAD_W:hd * HEAD_W + LANE] = (
            kv[:, nope_w + hd * LANE:nope_w + (hd + 1) * LANE].astype(BF16))
        vm_ref[:, hd * HEAD_W + LANE:(hd + 1) * HEAD_W] = jnp.ones((tm, LANE), BF16)


def _qkv(x2d, ga, wt, wkpe, glq, glkv, wq, wkv, gqn, gqr, gkn, gkr, gdq, gdk, cm, sm, cd, sd,
         later_weights, *, seq, tm=256):
    t = x2d.shape[0]
    steps = t // tm
    spt = seq // tm
    tab = pl.BlockSpec((tm, LANE), lambda i: (i % spt, 0))
    vec = lambda n: _resident((1, n))
    widths = [MLA_HEADS * HEAD_W, MLA_HEADS * HEAD_W, MLA_HEADS * HEAD_W,
              DIFF_HEADS * LANE, DIFF_HEADS * LANE, DIFF_HEADS * DIFF_V]
    slabs = [pl.BlockSpec((w.shape[0] // steps, w.shape[1]), lambda i: (i, 0)) for w in later_weights]
    assert all(w.shape[0] % (16 * steps) == 0 for w in later_weights)
    outs = pl.pallas_call(
        _qkv_kernel,
        name="qkv",
        grid=(steps,),
        in_specs=[pl.BlockSpec((tm, D_MODEL), lambda i: (i, 0)), vec(D_MODEL),
                  _resident(wt.shape), _resident(wkpe.shape),
                  vec(Q_LORA), vec(KV_LORA), _resident(wq.shape), _resident(wkv.shape),
                  vec(LANE), vec(LANE), vec(LANE), vec(LANE), vec(LANE), vec(LANE),
                  tab, tab, tab, tab] + slabs,
        out_specs=[pl.BlockSpec((tm, w), lambda i: (i, 0)) for w in widths] + slabs,
        out_shape=([jax.ShapeDtypeStruct((t, w), BF16) for w in widths]
                   + [jax.ShapeDtypeStruct(w.shape, BF16) for w in later_weights]),
        compiler_params=_params(("arbitrary",)),
    )(x2d, ga, wt, wkpe, glq, glkv, wq, wkv, gqn, gqr, gkn, gkr, gdq, gdk, cm, sm, cd, sd, *later_weights)
    return outs[:len(widths)], outs[len(widths):]


def _causal_keep(tq):
    row = lax.broadcasted_iota(jnp.int32, (tq, tq), 0)
    col = lax.broadcasted_iota(jnp.int32, (tq, tq), 1)
    return col <= row


def _scores(q, k_rows, r0, tq, keep):
    s_d = _dot_nt(q, k_rows(r0, r0 + tq))
    s_d = jnp.where(keep, s_d, -jnp.inf)
    s_o = None
    if r0 > 0:
        s_o = _dot_nt(q, k_rows(0, r0))
    return s_o, s_d


def _softmax_pieces(s, with_sum=True):
    s_o, s_d = s
    m = jnp.max(s_d, axis=-1, keepdims=True)
    if s_o is None:
        p_d = jnp.exp2(s_d - m)
        return None, p_d, jnp.sum(p_d, axis=-1, keepdims=True) if with_sum else None
    m = jnp.maximum(m, jnp.max(s_o, axis=-1, keepdims=True))
    p_o = jnp.exp2(s_o - m)
    p_d = jnp.exp2(s_d - m)
    if not with_sum:
        return p_o, p_d, None
    return p_o, p_d, jnp.sum(p_o, axis=-1, keepdims=True) + jnp.sum(p_d, axis=-1, keepdims=True)


def _pv(p_o, p_d, v_rows, r0, tq):
    o = jnp.dot(p_d.astype(BF16), v_rows(r0, r0 + tq), preferred_element_type=F32)
    if p_o is not None:
        o = o + jnp.dot(p_o.astype(BF16), v_rows(0, r0), preferred_element_type=F32)
    return o


def _work_items(seq, tq, heads, descending):
    starts = list(range(0, seq, tq))
    if descending:
        starts = starts[::-1]
    return [(j, r0) for j in range(heads) for r0 in starts]


def _cols(ref, j, width):
    return lambda a, b: ref[a:b, j * width:(j + 1) * width]


def _pipelined(items, scores, finish):
    s_next = scores(items[0])
    for k, item in enumerate(items):
        s = s_next
        if k + 1 < len(items):
            s_next = scores(items[k + 1])
        finish(item, s)


def _mla_attn_kernel(q_ref, k_ref, v_ref, g_ref, o_ref, *, tq, heads):
    seq = q_ref.shape[0]
    keep = _causal_keep(tq)
    g = g_ref[...]

    def scores(item):
        j, r0 = item
        return _scores(_cols(q_ref, j, HEAD_W)(r0, r0 + tq), _cols(k_ref, j, HEAD_W), r0, tq, keep)

    def finish(item, s):
        j, r0 = item
        p_o, p_d, _ = _softmax_pieces(s, with_sum=False)
        ol = _pv(p_o, p_d, _cols(v_ref, j, HEAD_W), r0, tq)
        o = ol[:, :MLA_V] / ol[:, MLA_V:]
        o_ref[r0:r0 + tq, j * MLA_V:(j + 1) * MLA_V] = ((o * _inv_rms(o, MLA_V)) * g).astype(BF16)

    _pipelined(_work_items(seq, tq, heads, descending=True), scores, finish)


def _diff_attn_kernel(q_ref, k_ref, v_ref, g_ref, lam_ref, o_ref, *, tq, heads):
    seq = q_ref.shape[0]
    keep = _causal_keep(tq)
    g = g_ref[...]
    lp = lam_ref[...]
    lam = (jnp.exp(jnp.sum(lp[0:1, :] * lp[1:2, :], axis=-1, keepdims=True))
           - jnp.exp(jnp.sum(lp[2:3, :] * lp[3:4, :], axis=-1, keepdims=True))
           + LAMBDA_INIT)
    lane = lax.broadcasted_iota(jnp.int32, (tq, LANE), 1)
    first = lane < DIFF_QK

    def scores(item):
        j, r0 = item
        q = _cols(q_ref, j, LANE)(r0, r0 + tq)
        k_rows = _cols(k_ref, j, LANE)
        q1 = jnp.where(first, q, jnp.zeros_like(q))
        q2 = jnp.where(first, jnp.zeros_like(q), q)
        return _scores(q1, k_rows, r0, tq, keep), _scores(q2, k_rows, r0, tq, keep)

    def finish(item, s):
        j, r0 = item
        p1_o, p1_d, l1 = _softmax_pieces(s[0])
        p2_o, p2_d, l2 = _softmax_pieces(s[1])
        c = lam * (l1 / l2)
        a_o = None if p1_o is None else p1_o - c * p2_o
        o = _pv(a_o, p1_d - c * p2_d, _cols(v_ref, j, DIFF_V), r0, tq) / l1
        o_ref[r0:r0 + tq, j * DIFF_V:(j + 1) * DIFF_V] = (
            ((o * _inv_rms(o, DIFF_V)) * g) * (1.0 - LAMBDA_INIT)).astype(BF16)

    _pipelined(_work_items(seq, tq, heads, descending=False), scores, finish)


def _head_spec(seq, width, heads):
    return pl.BlockSpec((seq, heads * width), lambda b, h: (b, h))


def _mla_attn(qm, km, vm, g, *, batch, seq, tq=256, heads=2):
    t = qm.shape[0]
    return pl.pallas_call(
        functools.partial(_mla_attn_kernel, tq=tq, heads=heads),
        name="mla_attn",
        grid=(batch, MLA_HEADS // heads),
        in_specs=[_head_spec(seq, HEAD_W, heads), _head_spec(seq, HEAD_W, heads), _head_spec(seq, HEAD_W, heads),
                  pl.BlockSpec((1, MLA_V), lambda b, h: (0, 0))],
        out_specs=_head_spec(seq, MLA_V, heads),
        out_shape=jax.ShapeDtypeStruct((t, MLA_HEADS * MLA_V), BF16),
        compiler_params=_params(("arbitrary", "arbitrary")),
    )(qm, km, vm, g)


def _diff_attn(qd, kd, vd, g, lam_params, *, batch, seq, tq=256, heads=2):
    t = qd.shape[0]
    return pl.pallas_call(
        functools.partial(_diff_attn_kernel, tq=tq, heads=heads),
        name="diff_attn",
        grid=(batch, DIFF_HEADS // heads),
        in_specs=[_head_spec(seq, LANE, heads), _head_spec(seq, LANE, heads), _head_spec(seq, DIFF_V, heads),
                  pl.BlockSpec((1, DIFF_V), lambda b, h: (0, 0)),
                  pl.BlockSpec((4, DIFF_QK), lambda b, h: (0, 0))],
        out_specs=_head_spec(seq, DIFF_V, heads),
        out_shape=jax.ShapeDtypeStruct((t, DIFF_HEADS * DIFF_V), BF16),
        compiler_params=_params(("arbitrary", "arbitrary")),
    )(qd, kd, vd, g, lam_params)


def _out_proj_kernel(x_ref, om_ref, od_ref, w_ref, o_ref):
    km = om_ref.shape[1]
    acc = jnp.dot(om_ref[...], w_ref[0:km, :], preferred_element_type=F32)
    acc = acc + jnp.dot(od_ref[...], w_ref[km:, :], preferred_element_type=F32)
    o_ref[...] = x_ref[...] + acc


def _out_proj(x2d, om, od, w, *, tm=512):
    t = x2d.shape[0]
    return pl.pallas_call(
        _out_proj_kernel,
        name="out_proj",
        grid=(t // tm,),
        in_specs=[pl.BlockSpec((tm, D_MODEL), lambda i: (i, 0)),
                  pl.BlockSpec((tm, om.shape[1]), lambda i: (i, 0)),
                  pl.BlockSpec((tm, od.shape[1]), lambda i: (i, 0)),
                  _resident(w.shape)],
        out_specs=pl.BlockSpec((tm, D_MODEL), lambda i: (i, 0)),
        out_shape=jax.ShapeDtypeStruct((t, D_MODEL), F32),
        compiler_params=_params(("arbitrary",)),
    )(x2d, om, od, w)


def _ffn_kernel(x_ref, g_ref, wg_ref, wu_ref, wd_ref, o_ref, h_ref, *, out_chunk):
    @pl.when(pl.program_id(1) == 0)
    def _():
        x = x_ref[...]
        h_ref[...] = ((x * _inv_rms(x, D_MODEL)) * g_ref[...]).astype(BF16)
        o_ref[...] = x

    h = h_ref[...]
    gate = jnp.dot(h, wg_ref[...], preferred_element_type=F32)
    up = jnp.dot(h, wu_ref[...], preferred_element_type=F32)
    act = (gate * jax.nn.sigmoid(gate) * up).astype(BF16)
    for c0 in range(0, o_ref.shape[1], out_chunk):
        o_ref[:, c0:c0 + out_chunk] += jnp.dot(act, wd_ref[:, c0:c0 + out_chunk],
                                               preferred_element_type=F32)


def _ffn(x1, g, wg, wu, wd, *, tm=512, tf=512):
    t = x1.shape[0]
    return pl.pallas_call(
        functools.partial(_ffn_kernel, out_chunk=512),
        name="ffn",
        grid=(t // tm, D_FF // tf),
        in_specs=[pl.BlockSpec((tm, D_MODEL), lambda i, f: (i, 0)),
                  pl.BlockSpec((1, D_MODEL), lambda i, f: (0, 0)),
                  pl.BlockSpec((D_MODEL, tf), lambda i, f: (0, f)),
                  pl.BlockSpec((D_MODEL, tf), lambda i, f: (0, f)),
                  pl.BlockSpec((tf, D_MODEL), lambda i, f: (f, 0))],
        out_specs=pl.BlockSpec((tm, D_MODEL), lambda i, f: (i, 0)),
        out_shape=jax.ShapeDtypeStruct((t, D_MODEL), F32),
        scratch_shapes=[pltpu.VMEM((tm, D_MODEL), BF16)],
        compiler_params=_params(("arbitrary", "arbitrary")),
    )(x1, g, wg, wu, wd)


def _rope_tables(seq):
    pos = jnp.arange(seq, dtype=jnp.int32).astype(F32)

    def angles(r):
        freqs = 1.0 / (ROPE_THETA ** (jnp.arange(0, r, 2, dtype=F32) / r))
        return pos[:, None] * freqs[None, :]

    am = angles(MLA_ROPE)
    cos, sin = jnp.cos(am), jnp.sin(am)
    zeros = jnp.zeros((seq, LANE - MLA_ROPE), F32)
    cm = jnp.concatenate([cos, cos, zeros], axis=-1)
    sm = jnp.concatenate([-sin, sin, zeros], axis=-1)

    ad = angles(DIFF_ROT)
    cos, sin = jnp.cos(ad), jnp.sin(ad)
    rest = DIFF_QK - DIFF_ROT
    cd = jnp.concatenate([cos, cos, jnp.ones((seq, rest), F32)], axis=-1)
    sd = jnp.concatenate([-sin, sin, jnp.zeros((seq, rest), F32)], axis=-1)
    return cm, sm, jnp.tile(cd, (1, 2)), jnp.tile(sd, (1, 2))


def _with_partner(v):
    return jnp.concatenate([v, jnp.roll(v, MLA_ROPE // 2, axis=-1)], axis=-1)


def kernel(x, attn_norm, w_in, q_latent_norm, w_q_up, kv_latent_norm, w_kv_up, mla_q_norm, mla_k_norm, mla_out_norm, diff_q_norm, diff_k_norm, lambda_q1, lambda_k1, lambda_q2, lambda_k2, diff_out_norm, w_o, ffn_norm, w_gate, w_up, w_down):
    batch, seq, d = x.shape
    assert d == D_MODEL and attn_norm.shape[0] == 1
    t = batch * seq
    x2d = x.reshape(t, d)
    l = 0

    wt = jnp.swapaxes(w_in[l], 0, 1).astype(BF16)
    wkpe = _with_partner(wt[O_KPE:O_DQ].T).T
    wq = w_q_up[l].astype(BF16).reshape(Q_LORA, MLA_HEADS, MLA_QK)
    wq = jnp.concatenate([wq[:, :, :MLA_NOPE].reshape(Q_LORA, -1),
                          _with_partner(wq[:, :, MLA_NOPE:]).reshape(Q_LORA, -1)], axis=1)
    wkv = w_kv_up[l].astype(BF16).reshape(KV_LORA, MLA_HEADS, MLA_NOPE + MLA_V)
    wkv = jnp.concatenate([wkv[:, :, :MLA_NOPE].reshape(KV_LORA, -1),
                           wkv[:, :, MLA_NOPE:].reshape(KV_LORA, -1)], axis=1)

    gqn, gqr = mla_q_norm[l:l + 1, :MLA_NOPE], _with_partner(mla_q_norm[l:l + 1, MLA_NOPE:])
    gkn, gkr = mla_k_norm[l:l + 1, :MLA_NOPE], _with_partner(mla_k_norm[l:l + 1, MLA_NOPE:])
    gdq = jnp.tile(diff_q_norm[l:l + 1], (1, 2))
    gdk = jnp.tile(diff_k_norm[l:l + 1], (1, 2))
    lam_params = jnp.concatenate([lambda_q1[l:l + 1], lambda_k1[l:l + 1],
                                  lambda_q2[l:l + 1], lambda_k2[l:l + 1]], axis=0)
    cm, sm, cd, sd = _rope_tables(seq)

    (qm, km, vm, qd, kd, vd), (wo16, wg16, wu16, wd16) = _qkv(
        x2d, attn_norm[l:l + 1], wt, wkpe, q_latent_norm[l:l + 1], kv_latent_norm[l:l + 1], wq, wkv,
        gqn, gqr, gkn, gkr, gdq, gdk, cm, sm, cd, sd, (w_o[l], w_gate[l], w_up[l], w_down[l]), seq=seq)
    o_mla = _mla_attn(qm, km, vm, mla_out_norm[l:l + 1], batch=batch, seq=seq)
    o_diff = _diff_attn(qd, kd, vd, diff_out_norm[l:l + 1], lam_params, batch=batch, seq=seq)
    x1 = _out_proj(x2d, o_mla, o_diff, wo16)
    out = _ffn(x1, ffn_norm[l:l + 1], wg16, wu16, wd16)
    return out.reshape(batch, seq, d)
```

```python
import functools
import math

import jax
import jax.numpy as jnp
from jax import lax
from jax.experimental import pallas as pl
from jax.experimental.pallas import tpu as pltpu

F32 = jnp.float32
BF16 = jnp.bfloat16

D_MODEL = 2048
MLA_HEADS = 8
MLA_NOPE = 128
MLA_ROPE = 64
MLA_QK = MLA_NOPE + MLA_ROPE
MLA_V = 128
Q_LORA = 512
KV_LORA = 512
DIFF_HEADS = 8
DIFF_QK = 64
DIFF_ROT = 16
DIFF_V = 128
ROPE_THETA = 500000.0
D_FF = 5632
EPS = 1e-6
LAMBDA_INIT = 0.8 - 0.6 * math.exp(-0.3 * 0)
LOG2E = math.log2(math.e)

LANE = 128
HEAD_W = 2 * LANE
VMEM_LIMIT = 56 * 1024 * 1024

O_CKV = Q_LORA
O_KPE = O_CKV + KV_LORA
O_DQ = O_KPE + MLA_ROPE
O_DK = O_DQ + DIFF_HEADS * 2 * DIFF_QK
O_DV = O_DK + DIFF_HEADS * 2 * DIFF_QK
IN_COLS = O_DV + DIFF_HEADS * DIFF_V

_NT = (((1,), (1,)), ((), ()))


def _params(sem):
    return pltpu.CompilerParams(dimension_semantics=sem, vmem_limit_bytes=VMEM_LIMIT)


def _resident(shape):
    return pl.BlockSpec(shape, lambda *_: (0,) * len(shape), pipeline_mode=pl.Buffered(1))


def _inv_rms(xf, n):
    return lax.rsqrt(jnp.sum(xf * xf, axis=-1, keepdims=True) * (1.0 / n) + EPS)


def _dot_nt(a, b):
    return lax.dot_general(a, b, _NT, preferred_element_type=F32)


def _qkv_kernel(x_ref, ga_ref, wt_ref, wkpe_ref, glq_ref, glkv_ref, wq_ref, wkv_ref,
                gqn_ref, gqr_ref, gkn_ref, gkr_ref, gdq_ref, gdk_ref,
                cm_ref, sm_ref, cd_ref, sd_ref,
                wo_ref, wg_ref, wu_ref, wd_ref,
                qm_ref, km_ref, vm_ref, qd_ref, kd_ref, vd_ref,
                wo16_ref, wg16_ref, wu16_ref, wd16_ref):
    for src, dst in ((wo_ref, wo16_ref), (wg_ref, wg16_ref), (wu_ref, wu16_ref), (wd_ref, wd16_ref)):
        dst[...] = src[...].astype(BF16)

    tm = x_ref.shape[0]
    cm, sm, cd, sd = cm_ref[...], sm_ref[...], cd_ref[...], sd_ref[...]
    x = x_ref[...]
    inv_x = _inv_rms(x, D_MODEL)
    h = (x * ga_ref[...]).astype(BF16)

    def proj(w):
        return _dot_nt(h, w) * inv_x

    lane = lax.broadcasted_iota(jnp.int32, (tm, LANE), 1)
    lo = lane < DIFF_QK
    is_x1 = (lane % DIFF_QK) < (DIFF_ROT // 2)

    def prep_diff(xh, g, scale):
        sq = xh * xh
        s_lo = jnp.sum(jnp.where(lo, sq, 0.0), axis=-1, keepdims=True)
        s_hi = jnp.sum(jnp.where(lo, 0.0, sq), axis=-1, keepdims=True)
        inv = lax.rsqrt(jnp.where(lo, s_lo, s_hi) * (1.0 / DIFF_QK) + EPS)
        y = (xh * inv) * g
        partner = jnp.where(is_x1, pltpu.roll(y, LANE - DIFF_ROT // 2, 1), pltpu.roll(y, DIFF_ROT // 2, 1))
        out = y * cd + partner * sd
        return out * scale if scale != 1.0 else out

    def rope_mla(y):
        return y * cm + pltpu.roll(y, MLA_ROPE, 1) * sm

    half = DIFF_HEADS // 2 * LANE
    gdq, gdk = gdq_ref[...], gdk_ref[...]
    diff_scale = LOG2E / math.sqrt(DIFF_QK)

    def diff_heads(src, g, scale, dst_ref, c0):
        for j in range(DIFF_HEADS // 2):
            dst_ref[:, c0 + j * LANE:c0 + (j + 1) * LANE] = prep_diff(
                src[:, j * LANE:(j + 1) * LANE], g, scale).astype(BF16)

    dq0 = proj(wt_ref[O_DQ:O_DQ + half, :])
    dq1 = proj(wt_ref[O_DQ + half:O_DK, :])
    diff_heads(dq0, gdq, diff_scale, qd_ref, 0)
    dk0 = proj(wt_ref[O_DK:O_DK + half, :])
    diff_heads(dq1, gdq, diff_scale, qd_ref, half)
    dk1 = proj(wt_ref[O_DK + half:O_DV, :])
    diff_heads(dk0, gdk, 1.0, kd_ref, 0)
    cq = proj(wt_ref[0:O_CKV, :])
    ckv = proj(wt_ref[O_CKV:O_KPE, :])
    kpe = proj(wkpe_ref[...])
    diff_heads(dk1, gdk, 1.0, kd_ref, half)

    hq = ((cq * _inv_rms(cq, Q_LORA)) * glq_ref[...]).astype(BF16)
    q = jnp.dot(hq, wq_ref[...], preferred_element_type=F32)
    hkv = ((ckv * _inv_rms(ckv, KV_LORA)) * glkv_ref[...]).astype(BF16)
    kv = jnp.dot(hkv, wkv_ref[...], preferred_element_type=F32)

    gqn, gqr, gkn = gqn_ref[...], gqr_ref[...], gkn_ref[...]
    mla_scale = LOG2E / math.sqrt(MLA_QK)
    nope_w = MLA_HEADS * LANE
    for hd in range(MLA_HEADS):
        qn = q[:, hd * LANE:(hd + 1) * LANE]
        qr = q[:, nope_w + hd * LANE:nope_w + (hd + 1) * LANE]
        ss = jnp.sum(qn * qn + 0.5 * (qr * qr), axis=-1, keepdims=True)
        sq = lax.rsqrt(ss * (1.0 / MLA_QK) + EPS)
        qm_ref[:, hd * HEAD_W:hd * HEAD_W + LANE] = (((qn * sq) * gqn) * mla_scale).astype(BF16)
        qm_ref[:, hd * HEAD_W + LANE:(hd + 1) * HEAD_W] = (rope_mla((qr * sq) * gqr) * mla_scale).astype(BF16)

    vd_ref[:, 0:half] = proj(wt_ref[O_DV:O_DV + half, :]).astype(BF16)

    kpe_sq = 0.5 * (kpe * kpe)
    kpe_rot = rope_mla(kpe * gkr_ref[...])
    for hd in range(MLA_HEADS):
        kn = kv[:, hd * LANE:(hd + 1) * LANE]
        ssk = jnp.sum(kn * kn + kpe_sq, axis=-1, keepdims=True)
        sk = lax.rsqrt(ssk * (1.0 / MLA_QK) + EPS)
        km_ref[:, hd * HEAD_W:hd * HEAD_W + LANE] = ((kn * sk) * gkn).astype(BF16)
        km_ref[:, hd * HEAD_W + LANE:(hd + 1) * HEAD_W] = (kpe_rot * sk).astype(BF16)
        vm_ref[:, hd * LANE:(hd + 1) * LANE] = kv[:, nope_w + hd * LANE:nope_w + (hd + 1) * LANE].astype(BF16)
        if hd == MLA_HEADS // 2:
            vd_ref[:, half:] = proj(wt_ref[O_DV + half:IN_COLS, :]).astype(BF16)


def _qkv(x2d, ga, wt, wkpe, glq, glkv, wq, wkv, gqn, gqr, gkn, gkr, gdq, gdk, cm, sm, cd, sd,
         later_weights, *, seq, tm=256):
    t = x2d.shape[0]
    steps = t // tm
    spt = seq // tm
    tab = pl.BlockSpec((tm, LANE), lambda i: (i % spt, 0))
    vec = lambda n: _resident((1, n))
    widths = [MLA_HEADS * HEAD_W, MLA_HEADS * HEAD_W, MLA_HEADS * MLA_V,
              DIFF_HEADS * LANE, DIFF_HEADS * LANE, DIFF_HEADS * DIFF_V]
    slabs = [pl.BlockSpec((w.shape[0] // steps, w.shape[1]), lambda i: (i, 0)) for w in later_weights]
    assert all(w.shape[0] % (16 * steps) == 0 for w in later_weights)
    outs = pl.pallas_call(
        _qkv_kernel,
        name="qkv",
        grid=(steps,),
        in_specs=[pl.BlockSpec((tm, D_MODEL), lambda i: (i, 0)), vec(D_MODEL),
                  _resident(wt.shape), _resident(wkpe.shape),
                  vec(Q_LORA), vec(KV_LORA), _resident(wq.shape), _resident(wkv.shape),
                  vec(LANE), vec(LANE), vec(LANE), vec(LANE), vec(LANE), vec(LANE),
                  tab, tab, tab, tab] + slabs,
        out_specs=[pl.BlockSpec((tm, w), lambda i: (i, 0)) for w in widths] + slabs,
        out_shape=([jax.ShapeDtypeStruct((t, w), BF16) for w in widths]
                   + [jax.ShapeDtypeStruct(w.shape, BF16) for w in later_weights]),
        compiler_params=_params(("arbitrary",)),
    )(x2d, ga, wt, wkpe, glq, glkv, wq, wkv, gqn, gqr, gkn, gkr, gdq, gdk, cm, sm, cd, sd, *later_weights)
    return outs[:len(widths)], outs[len(widths):]


def _causal_keep(tq):
    row = lax.broadcasted_iota(jnp.int32, (tq, tq), 0)
    col = lax.broadcasted_iota(jnp.int32, (tq, tq), 1)
    return col <= row


def _scores(q, k_rows, r0, tq, keep):
    s_d = _dot_nt(q, k_rows(r0, r0 + tq))
    s_d = jnp.where(keep, s_d, -jnp.inf)
    s_o = None
    if r0 > 0:
        s_o = _dot_nt(q, k_rows(0, r0))
    return s_o, s_d


def _softmax_pieces(s, with_sum=True):
    s_o, s_d = s
    m = jnp.max(s_d, axis=-1, keepdims=True)
    if s_o is None:
        p_d = jnp.exp2(s_d - m)
        return None, p_d, jnp.sum(p_d, axis=-1, keepdims=True) if with_sum else None
    m = jnp.maximum(m, jnp.max(s_o, axis=-1, keepdims=True))
    p_o = jnp.exp2(s_o - m)
    p_d = jnp.exp2(s_d - m)
    if not with_sum:
        return p_o, p_d, None
    return p_o, p_d, jnp.sum(p_o, axis=-1, keepdims=True) + jnp.sum(p_d, axis=-1, keepdims=True)


def _pv(p_o, p_d, v_rows, r0, tq):
    o = jnp.dot(p_d.astype(BF16), v_rows(r0, r0 + tq), preferred_element_type=F32)
    if p_o is not None:
        o = o + jnp.dot(p_o.astype(BF16), v_rows(0, r0), preferred_element_type=F32)
    return o


def _work_items(seq, tq, heads, descending):
    starts = list(range(0, seq, tq))
    if descending:
        starts = starts[::-1]
    return [(j, r0) for j in range(heads) for r0 in starts]


def _cols(ref, j, width):
    return lambda a, b: ref[a:b, j * width:(j + 1) * width]


def _pipelined(items, scores, finish):
    s_next = scores(items[0])
    for k, item in enumerate(items):
        s = s_next
        if k + 1 < len(items):
            s_next = scores(items[k + 1])
        finish(item, s)


def _mla_attn_kernel(q_ref, k_ref, v_ref, g_ref, o_ref, v1_ref, *, tq, heads):
    seq = q_ref.shape[0]
    keep = _causal_keep(tq)
    g = g_ref[...]
    for j in range(heads):
        v1_ref[:, j * HEAD_W:j * HEAD_W + MLA_V] = v_ref[:, j * MLA_V:(j + 1) * MLA_V]
        v1_ref[:, j * HEAD_W + MLA_V:(j + 1) * HEAD_W] = jnp.ones((seq, HEAD_W - MLA_V), BF16)

    def scores(item):
        j, r0 = item
        return _scores(_cols(q_ref, j, HEAD_W)(r0, r0 + tq), _cols(k_ref, j, HEAD_W), r0, tq, keep)

    def finish(item, s):
        j, r0 = item
        p_o, p_d, _ = _softmax_pieces(s, with_sum=False)
        ol = _pv(p_o, p_d, _cols(v1_ref, j, HEAD_W), r0, tq)
        o = ol[:, :MLA_V] / ol[:, MLA_V:]
        o_ref[r0:r0 + tq, j * MLA_V:(j + 1) * MLA_V] = ((o * _inv_rms(o, MLA_V)) * g).astype(BF16)

    _pipelined(_work_items(seq, tq, heads, descending=True), scores, finish)


def _diff_attn_kernel(q_ref, k_ref, v_ref, g_ref, lam_ref, o_ref, *, tq, heads):
    seq = q_ref.shape[0]
    keep = _causal_keep(tq)
    g = g_ref[...]
    lp = lam_ref[...]
    lam = (jnp.exp(jnp.sum(lp[0:1, :] * lp[1:2, :], axis=-1, keepdims=True))
           - jnp.exp(jnp.sum(lp[2:3, :] * lp[3:4, :], axis=-1, keepdims=True))
           + LAMBDA_INIT)
    lane = lax.broadcasted_iota(jnp.int32, (tq, LANE), 1)
    first = lane < DIFF_QK

    def scores(item):
        j, r0 = item
        q = _cols(q_ref, j, LANE)(r0, r0 + tq)
        k_rows = _cols(k_ref, j, LANE)
        q1 = jnp.where(first, q, jnp.zeros_like(q))
        q2 = jnp.where(first, jnp.zeros_like(q), q)
        return _scores(q1, k_rows, r0, tq, keep), _scores(q2, k_rows, r0, tq, keep)

    def finish(item, s):
        j, r0 = item
        p1_o, p1_d, l1 = _softmax_pieces(s[0])
        p2_o, p2_d, l2 = _softmax_pieces(s[1])
        c = lam * (l1 / l2)
        a_o = None if p1_o is None else p1_o - c * p2_o
        o = _pv(a_o, p1_d - c * p2_d, _cols(v_ref, j, DIFF_V), r0, tq) / l1
        o_ref[r0:r0 + tq, j * DIFF_V:(j + 1) * DIFF_V] = (
            ((o * _inv_rms(o, DIFF_V)) * g) * (1.0 - LAMBDA_INIT)).astype(BF16)

    _pipelined(_work_items(seq, tq, heads, descending=False), scores, finish)


def _head_spec(seq, width, heads):
    return pl.BlockSpec((seq, heads * width), lambda b, h: (b, h))


def _mla_attn(qm, km, vm, g, *, batch, seq, tq=256, heads=2):
    t = qm.shape[0]
    return pl.pallas_call(
        functools.partial(_mla_attn_kernel, tq=tq, heads=heads),
        name="mla_attn",
        grid=(batch, MLA_HEADS // heads),
        in_specs=[_head_spec(seq, HEAD_W, heads), _head_spec(seq, HEAD_W, heads), _head_spec(seq, MLA_V, heads),
                  pl.BlockSpec((1, MLA_V), lambda b, h: (0, 0))],
        out_specs=_head_spec(seq, MLA_V, heads),
        out_shape=jax.ShapeDtypeStruct((t, MLA_HEADS * MLA_V), BF16),
        scratch_shapes=[pltpu.VMEM((seq, heads * HEAD_W), BF16)],
        compiler_params=_params(("arbitrary", "arbitrary")),
    )(qm, km, vm, g)


def _diff_attn(qd, kd, vd, g, lam_params, *, batch, seq, tq=256, heads=2):
    t = qd.shape[0]
    return pl.pallas_call(
        functools.partial(_diff_attn_kernel, tq=tq, heads=heads),
        name="diff_attn",
        grid=(batch, DIFF_HEADS // heads),
        in_specs=[_head_spec(seq, LANE, heads), _head_spec(seq, LANE, heads), _head_spec(seq, DIFF_V, heads),
                  pl.BlockSpec((1, DIFF_V), lambda b, h: (0, 0)),
                  pl.BlockSpec((4, DIFF_QK), lambda b, h: (0, 0))],
        out_specs=_head_spec(seq, DIFF_V, heads),
        out_shape=jax.ShapeDtypeStruct((t, DIFF_HEADS * DIFF_V), BF16),
        compiler_params=_params(("arbitrary", "arbitrary")),
    )(qd, kd, vd, g, lam_params)


def _out_proj_kernel(x_ref, om_ref, od_ref, w_ref, g_ref, o_ref, h_ref, inv_ref, *, col_chunk):
    km = om_ref.shape[1]
    om, od = om_ref[...], od_ref[...]
    ssq = None
    for c0 in range(0, o_ref.shape[1], col_chunk):
        cs = slice(c0, c0 + col_chunk)
        acc = jnp.dot(om, w_ref[0:km, cs], preferred_element_type=F32)
        acc = acc + jnp.dot(od, w_ref[km:, cs], preferred_element_type=F32)
        x1 = x_ref[:, cs] + acc
        o_ref[:, cs] = x1
        h_ref[:, cs] = (x1 * g_ref[:, cs]).astype(BF16)
        sq = x1 * x1
        for l0 in range(0, col_chunk, LANE):
            part = sq[:, l0:l0 + LANE]
            ssq = part if ssq is None else ssq + part
    inv = lax.rsqrt(jnp.sum(ssq, axis=-1, keepdims=True) * (1.0 / D_MODEL) + EPS)
    inv_ref[...] = jnp.broadcast_to(inv, inv_ref.shape)


def _out_proj(x2d, om, od, w, g, *, tm=512):
    t = x2d.shape[0]
    row = pl.BlockSpec((tm, D_MODEL), lambda i: (i, 0))
    return pl.pallas_call(
        functools.partial(_out_proj_kernel, col_chunk=512),
        name="out_proj",
        grid=(t // tm,),
        in_specs=[row,
                  pl.BlockSpec((tm, om.shape[1]), lambda i: (i, 0)),
                  pl.BlockSpec((tm, od.shape[1]), lambda i: (i, 0)),
                  _resident(w.shape), _resident((1, D_MODEL))],
        out_specs=[row, row, pl.BlockSpec((tm, LANE), lambda i: (i, 0))],
        out_shape=[jax.ShapeDtypeStruct((t, D_MODEL), F32), jax.ShapeDtypeStruct((t, D_MODEL), BF16),
                   jax.ShapeDtypeStruct((t, LANE), F32)],
        compiler_params=_params(("arbitrary",)),
    )(x2d, om, od, w, g)


def _ffn_kernel(x_ref, h_ref, inv_ref, wg_ref, wu_ref, wd_ref, o_ref, *, out_chunk):
    @pl.when(pl.program_id(1) == 0)
    def _():
        o_ref[...] = x_ref[...]

    h = h_ref[...]
    inv = inv_ref[:, 0:1]
    gate = jnp.dot(h, wg_ref[...], preferred_element_type=F32) * inv
    up = jnp.dot(h, wu_ref[...], preferred_element_type=F32) * inv
    act = (gate * jax.nn.sigmoid(gate) * up).astype(BF16)
    for c0 in range(0, o_ref.shape[1], out_chunk):
        o_ref[:, c0:c0 + out_chunk] += jnp.dot(act, wd_ref[:, c0:c0 + out_chunk],
                                               preferred_element_type=F32)


def _ffn(x1, h, inv, wg, wu, wd, *, tm=512, tf=512):
    t = x1.shape[0]
    return pl.pallas_call(
        functools.partial(_ffn_kernel, out_chunk=512),
        name="ffn",
        grid=(t // tm, D_FF // tf),
        in_specs=[pl.BlockSpec((tm, D_MODEL), lambda i, f: (i, 0)),
                  pl.BlockSpec((tm, D_MODEL), lambda i, f: (i, 0)),
                  pl.BlockSpec((tm, LANE), lambda i, f: (i, 0)),
                  pl.BlockSpec((D_MODEL, tf), lambda i, f: (0, f)),
                  pl.BlockSpec((D_MODEL, tf), lambda i, f: (0, f)),
                  pl.BlockSpec((tf, D_MODEL), lambda i, f: (f, 0))],
        out_specs=pl.BlockSpec((tm, D_MODEL), lambda i, f: (i, 0)),
        out_shape=jax.ShapeDtypeStruct((t, D_MODEL), F32),
        compiler_params=_params(("arbitrary", "arbitrary")),
    )(x1, h, inv, wg, wu, wd)


def _rope_tables(seq):
    pos = jnp.arange(seq, dtype=jnp.int32).astype(F32)

    def angles(r):
        freqs = 1.0 / (ROPE_THETA ** (jnp.arange(0, r, 2, dtype=F32) / r))
        return pos[:, None] * freqs[None, :]

    am = angles(MLA_ROPE)
    cos, sin = jnp.cos(am), jnp.sin(am)
    zeros = jnp.zeros((seq, LANE - MLA_ROPE), F32)
    cm = jnp.concatenate([cos, cos, zeros], axis=-1)
    sm = jnp.concatenate([-sin, sin, zeros], axis=-1)

    ad = angles(DIFF_ROT)
    cos, sin = jnp.cos(ad), jnp.sin(ad)
    rest = DIFF_QK - DIFF_ROT
    cd = jnp.concatenate([cos, cos, jnp.ones((seq, rest), F32)], axis=-1)
    sd = jnp.concatenate([-sin, sin, jnp.zeros((seq, rest), F32)], axis=-1)
    return cm, sm, jnp.tile(cd, (1, 2)), jnp.tile(sd, (1, 2))


def _with_partner(v):
    return jnp.concatenate([v, jnp.roll(v, MLA_ROPE // 2, axis=-1)], axis=-1)


def kernel(x, attn_norm, w_in, q_latent_norm, w_q_up, kv_latent_norm, w_kv_up, mla_q_norm, mla_k_norm, mla_out_norm, diff_q_norm, diff_k_norm, lambda_q1, lambda_k1, lambda_q2, lambda_k2, diff_out_norm, w_o, ffn_norm, w_gate, w_up, w_down):
    batch, seq, d = x.shape
    assert d == D_MODEL and attn_norm.shape[0] == 1
    t = batch * seq
    x2d = x.reshape(t, d)
    l = 0

    wt = jnp.swapaxes(w_in[l], 0, 1).astype(BF16)
    wkpe = _with_partner(wt[O_KPE:O_DQ].T).T
    wq = w_q_up[l].astype(BF16).reshape(Q_LORA, MLA_HEADS, MLA_QK)
    wq = jnp.concatenate([wq[:, :, :MLA_NOPE].reshape(Q_LORA, -1),
                          _with_partner(wq[:, :, MLA_NOPE:]).reshape(Q_LORA, -1)], axis=1)
    wkv = w_kv_up[l].astype(BF16).reshape(KV_LORA, MLA_HEADS, MLA_NOPE + MLA_V)
    wkv = jnp.concatenate([wkv[:, :, :MLA_NOPE].reshape(KV_LORA, -1),
                           wkv[:, :, MLA_NOPE:].reshape(KV_LORA, -1)], axis=1)

    gqn, gqr = mla_q_norm[l:l + 1, :MLA_NOPE], _with_partner(mla_q_norm[l:l + 1, MLA_NOPE:])
    gkn, gkr = mla_k_norm[l:l + 1, :MLA_NOPE], _with_partner(mla_k_norm[l:l + 1, MLA_NOPE:])
    gdq = jnp.tile(diff_q_norm[l:l + 1], (1, 2))
    gdk = jnp.tile(diff_k_norm[l:l + 1], (1, 2))
    lam_params = jnp.concatenate([lambda_q1[l:l + 1], lambda_k1[l:l + 1],
                                  lambda_q2[l:l + 1], lambda_k2[l:l + 1]], axis=0)
    cm, sm, cd, sd = _rope_tables(seq)

    (qm, km, vm, qd, kd, vd), (wo16, wg16, wu16, wd16) = _qkv(
        x2d, attn_norm[l:l + 1], wt, wkpe, q_latent_norm[l:l + 1], kv_latent_norm[l:l + 1], wq, wkv,
        gqn, gqr, gkn, gkr, gdq, gdk, cm, sm, cd, sd, (w_o[l], w_gate[l], w_up[l], w_down[l]), seq=seq)
    o_mla = _mla_attn(qm, km, vm, mla_out_norm[l:l + 1], batch=batch, seq=seq)
    o_diff = _diff_attn(qd, kd, vd, diff_out_norm[l:l + 1], lam_params, batch=batch, seq=seq)
    x1, h2, inv2 = _out_proj(x2d, o_mla, o_diff, wo16, ffn_norm[l:l + 1])
    out = _ffn(x1, h2, inv2, wg16, wu16, wd16)
    return out.reshape(batch, seq, d)
```

```python
import functools
import math

import jax
import jax.numpy as jnp
from jax import lax
from jax.experimental import pallas as pl
from jax.experimental.pallas import tpu as pltpu

F32 = jnp.float32
BF16 = jnp.bfloat16

D_MODEL = 2048
MLA_HEADS = 8
MLA_NOPE = 128
MLA_ROPE = 64
MLA_QK = MLA_NOPE + MLA_ROPE
MLA_V = 128
Q_LORA = 512
KV_LORA = 512
DIFF_HEADS = 8
DIFF_QK = 64
DIFF_ROT = 16
DIFF_V = 128
ROPE_THETA = 500000.0
D_FF = 5632
EPS = 1e-6
LAMBDA_INIT = 0.8 - 0.6 * math.exp(-0.3 * 0)
LOG2E = math.log2(math.e)

LANE = 128
HEAD_W = 2 * LANE
VMEM_LIMIT = 56 * 1024 * 1024

O_CKV = Q_LORA
O_KPE = O_CKV + KV_LORA
O_DQ = O_KPE + MLA_ROPE
O_DK = O_DQ + DIFF_HEADS * 2 * DIFF_QK
O_DV = O_DK + DIFF_HEADS * 2 * DIFF_QK
IN_COLS = O_DV + DIFF_HEADS * DIFF_V

_NT = (((1,), (1,)), ((), ()))


def _params(sem):
    return pltpu.CompilerParams(dimension_semantics=sem, vmem_limit_bytes=VMEM_LIMIT)


def _resident(shape):
    return pl.BlockSpec(shape, lambda *_: (0,) * len(shape), pipeline_mode=pl.Buffered(1))


def _inv_rms(xf, n):
    return lax.rsqrt(jnp.sum(xf * xf, axis=-1, keepdims=True) * (1.0 / n) + EPS)


def _dot_nt(a, b):
    return lax.dot_general(a, b, _NT, preferred_element_type=F32)


def _qkv_kernel(x_ref, ga_ref, wt_ref, wkpe_ref, glq_ref, glkv_ref, wq_ref, wkv_ref,
                gqn_ref, gqr_ref, gkn_ref, gkr_ref, gdq_ref, gdk_ref,
                cm_ref, sm_ref, cd_ref, sd_ref,
                wo_ref, wg_ref, wu_ref, wd_ref,
                qm_ref, km_ref, vm_ref, qd_ref, kd_ref, vd_ref,
                wo16_ref, wg16_ref, wu16_ref, wd16_ref):
    for src, dst in ((wo_ref, wo16_ref), (wg_ref, wg16_ref), (wu_ref, wu16_ref), (wd_ref, wd16_ref)):
        dst[...] = src[...].astype(BF16)

    tm = x_ref.shape[0]
    cm, sm, cd, sd = cm_ref[...], sm_ref[...], cd_ref[...], sd_ref[...]
    x = x_ref[...]
    h = ((x * _inv_rms(x, D_MODEL)) * ga_ref[...]).astype(BF16)

    lane = lax.broadcasted_iota(jnp.int32, (tm, LANE), 1)
    lo = lane < DIFF_QK
    is_x1 = (lane % DIFF_QK) < (DIFF_ROT // 2)

    def prep_diff(xh, g, scale):
        sq = xh * xh
        s_lo = jnp.sum(jnp.where(lo, sq, 0.0), axis=-1, keepdims=True)
        s_hi = jnp.sum(jnp.where(lo, 0.0, sq), axis=-1, keepdims=True)
        inv = lax.rsqrt(jnp.where(lo, s_lo, s_hi) * (1.0 / DIFF_QK) + EPS)
        y = (xh * inv) * g
        partner = jnp.where(is_x1, pltpu.roll(y, LANE - DIFF_ROT // 2, 1), pltpu.roll(y, DIFF_ROT // 2, 1))
        out = y * cd + partner * sd
        return out * scale if scale != 1.0 else out

    def rope_mla(y):
        return y * cm + pltpu.roll(y, MLA_ROPE, 1) * sm

    half = DIFF_HEADS // 2 * LANE
    gdq, gdk = gdq_ref[...], gdk_ref[...]
    diff_scale = LOG2E / math.sqrt(DIFF_QK)

    def diff_heads(src, g, scale, dst_ref, c0):
        for j in range(DIFF_HEADS // 2):
            dst_ref[:, c0 + j * LANE:c0 + (j + 1) * LANE] = prep_diff(
                src[:, j * LANE:(j + 1) * LANE], g, scale).astype(BF16)

    dq0 = _dot_nt(h, wt_ref[O_DQ:O_DQ + half, :])
    dq1 = _dot_nt(h, wt_ref[O_DQ + half:O_DK, :])
    diff_heads(dq0, gdq, diff_scale, qd_ref, 0)
    dk0 = _dot_nt(h, wt_ref[O_DK:O_DK + half, :])
    diff_heads(dq1, gdq, diff_scale, qd_ref, half)
    dk1 = _dot_nt(h, wt_ref[O_DK + half:O_DV, :])
    diff_heads(dk0, gdk, 1.0, kd_ref, 0)
    cq = _dot_nt(h, wt_ref[0:O_CKV, :])
    ckv = _dot_nt(h, wt_ref[O_CKV:O_KPE, :])
    kpe = _dot_nt(h, wkpe_ref[...])
    diff_heads(dk1, gdk, 1.0, kd_ref, half)

    hq = ((cq * _inv_rms(cq, Q_LORA)) * glq_ref[...]).astype(BF16)
    q = jnp.dot(hq, wq_ref[...], preferred_element_type=F32)
    hkv = ((ckv * _inv_rms(ckv, KV_LORA)) * glkv_ref[...]).astype(BF16)
    kv = jnp.dot(hkv, wkv_ref[...], preferred_element_type=F32)
    dv0 = _dot_nt(h, wt_ref[O_DV:O_DV + half, :])
    dv1 = _dot_nt(h, wt_ref[O_DV + half:IN_COLS, :])

    gqn, gqr, gkn = gqn_ref[...], gqr_ref[...], gkn_ref[...]
    mla_scale = LOG2E / math.sqrt(MLA_QK)
    nope_w = MLA_HEADS * LANE
    for hd in range(MLA_HEADS):
        qn = q[:, hd * LANE:(hd + 1) * LANE]
        qr = q[:, nope_w + hd * LANE:nope_w + (hd + 1) * LANE]
        ss = jnp.sum(qn * qn + 0.5 * (qr * qr), axis=-1, keepdims=True)
        sq = lax.rsqrt(ss * (1.0 / MLA_QK) + EPS)
        qm_ref[:, hd * HEAD_W:hd * HEAD_W + LANE] = (((qn * sq) * gqn) * mla_scale).astype(BF16)
        qm_ref[:, hd * HEAD_W + LANE:(hd + 1) * HEAD_W] = (rope_mla((qr * sq) * gqr) * mla_scale).astype(BF16)

    vd_ref[:, 0:half] = dv0.astype(BF16)
    vd_ref[:, half:] = dv1.astype(BF16)

    kpe_sq = 0.5 * (kpe * kpe)
    kpe_rot = rope_mla(kpe * gkr_ref[...])
    for hd in range(MLA_HEADS):
        kn = kv[:, hd * LANE:(hd + 1) * LANE]
        ssk = jnp.sum(kn * kn + kpe_sq, axis=-1, keepdims=True)
        sk = lax.rsqrt(ssk * (1.0 / MLA_QK) + EPS)
        km_ref[:, hd * HEAD_W:hd * HEAD_W + LANE] = ((kn * sk) * gkn).astype(BF16)
        km_ref[:, hd * HEAD_W + LANE:(hd + 1) * HEAD_W] = (kpe_rot * sk).astype(BF16)
        vm_ref[:, hd * HEAD_W:hd * HEAD_W + LANE] = (
            kv[:, nope_w + hd * LANE:nope_w + (hd + 1) * LANE].astype(BF16))
        vm_ref[:, hd * HEAD_W + LANE:(hd + 1) * HEAD_W] = jnp.ones((tm, LANE), BF16)


def _qkv(x2d, ga, wt, wkpe, glq, glkv, wq, wkv, gqn, gqr, gkn, gkr, gdq, gdk, cm, sm, cd, sd,
         later_weights, *, seq, tm=256):
    t = x2d.shape[0]
    steps = t // tm
    spt = seq // tm
    tab = pl.BlockSpec((tm, LANE), lambda i: (i % spt, 0))
    vec = lambda n: _resident((1, n))
    widths = [MLA_HEADS * HEAD_W, MLA_HEADS * HEAD_W, MLA_HEADS * HEAD_W,
              DIFF_HEADS * LANE, DIFF_HEADS * LANE, DIFF_HEADS * DIFF_V]
    slabs = [pl.BlockSpec((w.shape[0] // steps, w.shape[1]), lambda i: (i, 0)) for w in later_weights]
    assert all(w.shape[0] % (16 * steps) == 0 for w in later_weights)
    outs = pl.pallas_call(
        _qkv_kernel,
        name="qkv",
        grid=(steps,),
        in_specs=[pl.BlockSpec((tm, D_MODEL), lambda i: (i, 0)), vec(D_MODEL),
                  _resident(wt.shape), _resident(wkpe.shape),
                  vec(Q_LORA), vec(KV_LORA), _resident(wq.shape), _resident(wkv.shape),
                  vec(LANE), vec(LANE), vec(LANE), vec(LANE), vec(LANE), vec(LANE),
                  tab, tab, tab, tab] + slabs,
        out_specs=[pl.BlockSpec((tm, w), lambda i: (i, 0)) for w in widths] + slabs,
        out_shape=([jax.ShapeDtypeStruct((t, w), BF16) for w in widths]
                   + [jax.ShapeDtypeStruct(w.shape, BF16) for w in later_weights]),
        compiler_params=_params(("arbitrary",)),
    )(x2d, ga, wt, wkpe, glq, glkv, wq, wkv, gqn, gqr, gkn, gkr, gdq, gdk, cm, sm, cd, sd, *later_weights)
    return outs[:len(widths)], outs[len(widths):]


def _causal_keep(tq):
    row = lax.broadcasted_iota(jnp.int32, (tq, tq), 0)
    col = lax.broadcasted_iota(jnp.int32, (tq, tq), 1)
    return col <= row


def _scores(q, k_rows, r0, tq, keep):
    s_d = _dot_nt(q, k_rows(r0, r0 + tq))
    s_d = jnp.where(keep, s_d, -jnp.inf)
    s_o = None
    if r0 > 0:
        s_o = _dot_nt(q, k_rows(0, r0))
    return s_o, s_d


def _softmax_pieces(s, with_sum=True):
    s_o, s_d = s
    m = jnp.max(s_d, axis=-1, keepdims=True)
    if s_o is None:
        p_d = jnp.exp2(s_d - m)
        return None, p_d, jnp.sum(p_d, axis=-1, keepdims=True) if with_sum else None
    m = jnp.maximum(m, jnp.max(s_o, axis=-1, keepdims=True))
    p_o = jnp.exp2(s_o - m)
    p_d = jnp.exp2(s_d - m)
    if not with_sum:
        return p_o, p_d, None
    return p_o, p_d, jnp.sum(p_o, axis=-1, keepdims=True) + jnp.sum(p_d, axis=-1, keepdims=True)


def _pv(p_o, p_d, v_rows, r0, tq):
    o = jnp.dot(p_d.astype(BF16), v_rows(r0, r0 + tq), preferred_element_type=F32)
    if p_o is not None:
        o = o + jnp.dot(p_o.astype(BF16), v_rows(0, r0), preferred_element_type=F32)
    return o


def _work_items(seq, tq, heads, descending):
    starts = list(range(0, seq, tq))
    if descending:
        starts = starts[::-1]
    return [(j, r0) for j in range(heads) for r0 in starts]


def _cols(ref, j, width):
    return lambda a, b: ref[a:b, j * width:(j + 1) * width]


def _pipelined(items, scores, finish):
    s_next = scores(items[0])
    for k, item in enumerate(items):
        s = s_next
        if k + 1 < len(items):
            s_next = scores(items[k + 1])
        finish(item, s)


def _mla_attn_kernel(q_ref, k_ref, v_ref, g_ref, o_ref, *, tq, heads):
    seq = q_ref.shape[0]
    keep = _causal_keep(tq)
    g = g_ref[...]

    def scores(item):
        j, r0 = item
        return _scores(_cols(q_ref, j, HEAD_W)(r0, r0 + tq), _cols(k_ref, j, HEAD_W), r0, tq, keep)

    def finish(item, s):
        j, r0 = item
        p_o, p_d, _ = _softmax_pieces(s, with_sum=False)
        ol = _pv(p_o, p_d, _cols(v_ref, j, HEAD_W), r0, tq)
        o = ol[:, :MLA_V] / ol[:, MLA_V:]
        o_ref[r0:r0 + tq, j * MLA_V:(j + 1) * MLA_V] = ((o * _inv_rms(o, MLA_V)) * g).astype(BF16)

    _pipelined(_work_items(seq, tq, heads, descending=True), scores, finish)


def _diff_attn_kernel(q_ref, k_ref, v_ref, g_ref, lam_ref, o_ref, *, tq, heads):
    seq = q_ref.shape[0]
    keep = _causal_keep(tq)
    g = g_ref[...]
    lp = lam_ref[...]
    lam = (jnp.exp(jnp.sum(lp[0:1, :] * lp[1:2, :], axis=-1, keepdims=True))
           - jnp.exp(jnp.sum(lp[2:3, :] * lp[3:4, :], axis=-1, keepdims=True))
           + LAMBDA_INIT)
    lane = lax.broadcasted_iota(jnp.int32, (tq, LANE), 1)
    first = lane < DIFF_QK

    def scores(item):
        j, r0 = item
        q = _cols(q_ref, j, LANE)(r0, r0 + tq)
        k_rows = _cols(k_ref, j, LANE)
        q1 = jnp.where(first, q, jnp.zeros_like(q))
        q2 = jnp.where(first, jnp.zeros_like(q), q)
        return _scores(q1, k_rows, r0, tq, keep), _scores(q2, k_rows, r0, tq, keep)

    def finish(item, s):
        j, r0 = item
        p1_o, p1_d, l1 = _softmax_pieces(s[0])
        p2_o, p2_d, l2 = _softmax_pieces(s[1])
        c = lam * (l1 / l2)
        a_o = None if p1_o is None else p1_o - c * p2_o
        o = _pv(a_o, p1_d - c * p2_d, _cols(v_ref, j, DIFF_V), r0, tq) / l1
        o_ref[r0:r0 + tq, j * DIFF_V:(j + 1) * DIFF_V] = (
            ((o * _inv_rms(o, DIFF_V)) * g) * (1.0 - LAMBDA_INIT)).astype(BF16)

    _pipelined(_work_items(seq, tq, heads, descending=False), scores, finish)


def _head_spec(seq, width, heads):
    return pl.BlockSpec((seq, heads * width), lambda b, h: (b, h))


def _mla_attn(qm, km, vm, g, *, batch, seq, tq=256, heads=2):
    t = qm.shape[0]
    return pl.pallas_call(
        functools.partial(_mla_attn_kernel, tq=tq, heads=heads),
        name="mla_attn",
        grid=(batch, MLA_HEADS // heads),
        in_specs=[_head_spec(seq, HEAD_W, heads), _head_spec(seq, HEAD_W, heads), _head_spec(seq, HEAD_W, heads),
                  pl.BlockSpec((1, MLA_V), lambda b, h: (0, 0))],
        out_specs=_head_spec(seq, MLA_V, heads),
        out_shape=jax.ShapeDtypeStruct((t, MLA_HEADS * MLA_V), BF16),
        compiler_params=_params(("arbitrary", "arbitrary")),
    )(qm, km, vm, g)


def _diff_attn(qd, kd, vd, g, lam_params, *, batch, seq, tq=256, heads=2):
    t = qd.shape[0]
    return pl.pallas_call(
        functools.partial(_diff_attn_kernel, tq=tq, heads=heads),
        name="diff_attn",
        grid=(batch, DIFF_HEADS // heads),
        in_specs=[_head_spec(seq, LANE, heads), _head_spec(seq, LANE, heads), _head_spec(seq, DIFF_V, heads),
                  pl.BlockSpec((1, DIFF_V), lambda b, h: (0, 0)),
                  pl.BlockSpec((4, DIFF_QK), lambda b, h: (0, 0))],
        out_specs=_head_spec(seq, DIFF_V, heads),
        out_shape=jax.ShapeDtypeStruct((t, DIFF_HEADS * DIFF_V), BF16),
        compiler_params=_params(("arbitrary", "arbitrary")),
    )(qd, kd, vd, g, lam_params)


def _out_proj_kernel(x_ref, om_ref, od_ref, w_ref, o_ref):
    km = om_ref.shape[1]
    acc = jnp.dot(om_ref[...], w_ref[0:km, :], preferred_element_type=F32)
    acc = acc + jnp.dot(od_ref[...], w_ref[km:, :], preferred_element_type=F32)
    o_ref[...] = x_ref[...] + acc


def _out_proj(x2d, om, od, w, *, tm=512):
    t = x2d.shape[0]
    return pl.pallas_call(
        _out_proj_kernel,
        name="out_proj",
        grid=(t // tm,),
        in_specs=[pl.BlockSpec((tm, D_MODEL), lambda i: (i, 0)),
                  pl.BlockSpec((tm, om.shape[1]), lambda i: (i, 0)),
                  pl.BlockSpec((tm, od.shape[1]), lambda i: (i, 0)),
                  _resident(w.shape)],
        out_specs=pl.BlockSpec((tm, D_MODEL), lambda i: (i, 0)),
        out_shape=jax.ShapeDtypeStruct((t, D_MODEL), F32),
        compiler_params=_params(("arbitrary",)),
    )(x2d, om, od, w)


def _ffn_kernel(x_ref, g_ref, wg_ref, wu_ref, wd_ref, o_ref, h_ref, *, out_chunk):
    @pl.when(pl.program_id(1) == 0)
    def _():
        x = x_ref[...]
        h_ref[...] = ((x * _inv_rms(x, D_MODEL)) * g_ref[...]).astype(BF16)
        o_ref[...] = x

    h = h_ref[...]
    gate = jnp.dot(h, wg_ref[...], preferred_element_type=F32)
    up = jnp.dot(h, wu_ref[...], preferred_element_type=F32)
    act = (gate * jax.nn.sigmoid(gate) * up).astype(BF16)
    for c0 in range(0, o_ref.shape[1], out_chunk):
        o_ref[:, c0:c0 + out_chunk] += jnp.dot(act, wd_ref[:, c0:c0 + out_chunk],
                                               preferred_element_type=F32)


def _ffn(x1, g, wg, wu, wd, *, tm=1024, tf=512):
    t = x1.shape[0]
    return pl.pallas_call(
        functools.partial(_ffn_kernel, out_chunk=512),
        name="ffn",
        grid=(t // tm, D_FF // tf),
        in_specs=[pl.BlockSpec((tm, D_MODEL), lambda i, f: (i, 0)),
                  pl.BlockSpec((1, D_MODEL), lambda i, f: (0, 0)),
                  pl.BlockSpec((D_MODEL, tf), lambda i, f: (0, f)),
                  pl.BlockSpec((D_MODEL, tf), lambda i, f: (0, f)),
                  pl.BlockSpec((tf, D_MODEL), lambda i, f: (f, 0))],
        out_specs=pl.BlockSpec((tm, D_MODEL), lambda i, f: (i, 0)),
        out_shape=jax.ShapeDtypeStruct((t, D_MODEL), F32),
        scratch_shapes=[pltpu.VMEM((tm, D_MODEL), BF16)],
        compiler_params=_params(("arbitrary", "arbitrary")),
    )(x1, g, wg, wu, wd)


def _rope_tables(seq):
    pos = jnp.arange(seq, dtype=jnp.int32).astype(F32)

    def angles(r):
        freqs = 1.0 / (ROPE_THETA ** (jnp.arange(0, r, 2, dtype=F32) / r))
        return pos[:, None] * freqs[None, :]

    am = angles(MLA_ROPE)
    cos, sin = jnp.cos(am), jnp.sin(am)
    zeros = jnp.zeros((seq, LANE - MLA_ROPE), F32)
    cm = jnp.concatenate([cos, cos, zeros], axis=-1)
    sm = jnp.concatenate([-sin, sin, zeros], axis=-1)

    ad = angles(DIFF_ROT)
    cos, sin = jnp.cos(ad), jnp.sin(ad)
    rest = DIFF_QK - DIFF_ROT
    cd = jnp.concatenate([cos, cos, jnp.ones((seq, rest), F32)], axis=-1)
    sd = jnp.concatenate([-sin, sin, jnp.zeros((seq, rest), F32)], axis=-1)
    return cm, sm, jnp.tile(cd, (1, 2)), jnp.tile(sd, (1, 2))


def _with_partner(v):
    return jnp.concatenate([v, jnp.roll(v, MLA_ROPE // 2, axis=-1)], axis=-1)


def kernel(x, attn_norm, w_in, q_latent_norm, w_q_up, kv_latent_norm, w_kv_up, mla_q_norm, mla_k_norm, mla_out_norm, diff_q_norm, diff_k_norm, lambda_q1, lambda_k1, lambda_q2, lambda_k2, diff_out_norm, w_o, ffn_norm, w_gate, w_up, w_down):
    batch, seq, d = x.shape
    assert d == D_MODEL and attn_norm.shape[0] == 1
    t = batch * seq
    x2d = x.reshape(t, d)
    l = 0

    wt = jnp.swapaxes(w_in[l], 0, 1).astype(BF16)
    wkpe = _with_partner(wt[O_KPE:O_DQ].T).T
    wq = w_q_up[l].astype(BF16).reshape(Q_LORA, MLA_HEADS, MLA_QK)
    wq = jnp.concatenate([wq[:, :, :MLA_NOPE].reshape(Q_LORA, -1),
                          _with_partner(wq[:, :, MLA_NOPE:]).reshape(Q_LORA, -1)], axis=1)
    wkv = w_kv_up[l].astype(BF16).reshape(KV_LORA, MLA_HEADS, MLA_NOPE + MLA_V)
    wkv = jnp.concatenate([wkv[:, :, :MLA_NOPE].reshape(KV_LORA, -1),
                           wkv[:, :, MLA_NOPE:].reshape(KV_LORA, -1)], axis=1)

    gqn, gqr = mla_q_norm[l:l + 1, :MLA_NOPE], _with_partner(mla_q_norm[l:l + 1, MLA_NOPE:])
    gkn, gkr = mla_k_norm[l:l + 1, :MLA_NOPE], _with_partner(mla_k_norm[l:l + 1, MLA_NOPE:])
    gdq = jnp.tile(diff_q_norm[l:l + 1], (1, 2))
    gdk = jnp.tile(diff_k_norm[l:l + 1], (1, 2))
    lam_params = jnp.concatenate([lambda_q1[l:l + 1], lambda_k1[l:l + 1],
                                  lambda_q2[l:l + 1], lambda_k2[l:l + 1]], axis=0)
    cm, sm, cd, sd = _rope_tables(seq)

    (qm, km, vm, qd, kd, vd), (wo16, wg16, wu16, wd16) = _qkv(
        x2d, attn_norm[l:l + 1], wt, wkpe, q_latent_norm[l:l + 1], kv_latent_norm[l:l + 1], wq, wkv,
        gqn, gqr, gkn, gkr, gdq, gdk, cm, sm, cd, sd, (w_o[l], w_gate[l], w_up[l], w_down[l]), seq=seq)
    o_mla = _mla_attn(qm, km, vm, mla_out_norm[l:l + 1], batch=batch, seq=seq)
    o_diff = _diff_attn(qd, kd, vd, diff_out_norm[l:l + 1], lam_params, batch=batch, seq=seq)
    x1 = _out_proj(x2d, o_mla, o_diff, wo16)
    out = _ffn(x1, ffn_norm[l:l + 1], wg16, wu16, wd16)
    return out.reshape(batch, seq, d)
```

```python
import functools
import math

import jax
import jax.numpy as jnp
from jax import lax
from jax.experimental import pallas as pl
from jax.experimental.pallas import tpu as pltpu

F32 = jnp.float32
BF16 = jnp.bfloat16

D_MODEL = 2048
MLA_HEADS = 8
MLA_NOPE = 128
MLA_ROPE = 64
MLA_QK = MLA_NOPE + MLA_ROPE
MLA_V = 128
Q_LORA = 512
KV_LORA = 512
DIFF_HEADS = 8
DIFF_QK = 64
DIFF_ROT = 16
DIFF_V = 128
ROPE_THETA = 500000.0
D_FF = 5632
EPS = 1e-6
LAMBDA_INIT = 0.8 - 0.6 * math.exp(-0.3 * 0)
LOG2E = math.log2(math.e)

SMALL_SCORE_BOUND = 32.0
LANE = 128
HEAD_W = 2 * LANE
VMEM_LIMIT = 56 * 1024 * 1024

O_CKV = Q_LORA
O_KPE = O_CKV + KV_LORA
O_DQ = O_KPE + MLA_ROPE
O_DK = O_DQ + DIFF_HEADS * 2 * DIFF_QK
O_DV = O_DK + DIFF_HEADS * 2 * DIFF_QK
IN_COLS = O_DV + DIFF_HEADS * DIFF_V

_NT = (((1,), (1,)), ((), ()))


def _params(sem):
    return pltpu.CompilerParams(dimension_semantics=sem, vmem_limit_bytes=VMEM_LIMIT)


def _resident(shape):
    return pl.BlockSpec(shape, lambda *_: (0,) * len(shape), pipeline_mode=pl.Buffered(1))


def _inv_rms(xf, n):
    return lax.rsqrt(jnp.sum(xf * xf, axis=-1, keepdims=True) * (1.0 / n) + EPS)


def _dot_nt(a, b):
    return lax.dot_general(a, b, _NT, preferred_element_type=F32)


def _qkv_kernel(x_ref, ga_ref, wt_ref, wkpe_ref, glq_ref, glkv_ref, wq_ref, wkv_ref,
                gqn_ref, gqr_ref, gkn_ref, gkr_ref, gdq_ref, gdk_ref,
                cm_ref, sm_ref, cd_ref, sd_ref,
                wo_ref, wg_ref, wu_ref, wd_ref,
                qm_ref, km_ref, vm_ref, qd_ref, kd_ref, vd_ref,
                wo16_ref, wg16_ref, wu16_ref, wd16_ref):
    for src, dst in ((wo_ref, wo16_ref), (wg_ref, wg16_ref), (wu_ref, wu16_ref), (wd_ref, wd16_ref)):
        dst[...] = src[...].astype(BF16)

    tm = x_ref.shape[0]
    cm, sm, cd, sd = cm_ref[...], sm_ref[...], cd_ref[...], sd_ref[...]
    x = x_ref[...]
    h = ((x * _inv_rms(x, D_MODEL)) * ga_ref[...]).astype(BF16)

    lane = lax.broadcasted_iota(jnp.int32, (tm, LANE), 1)
    lo = lane < DIFF_QK
    is_x1 = (lane % DIFF_QK) < (DIFF_ROT // 2)

    def prep_diff(xh, g, scale):
        sq = xh * xh
        s_lo = jnp.sum(jnp.where(lo, sq, 0.0), axis=-1, keepdims=True)
        s_hi = jnp.sum(jnp.where(lo, 0.0, sq), axis=-1, keepdims=True)
        inv = lax.rsqrt(jnp.where(lo, s_lo, s_hi) * (1.0 / DIFF_QK) + EPS)
        y = (xh * inv) * g
        partner = jnp.where(is_x1, pltpu.roll(y, LANE - DIFF_ROT // 2, 1), pltpu.roll(y, DIFF_ROT // 2, 1))
        out = y * cd + partner * sd
        return out * scale if scale != 1.0 else out

    def rope_mla(y):
        return y * cm + pltpu.roll(y, MLA_ROPE, 1) * sm

    half = DIFF_HEADS // 2 * LANE
    gdq, gdk = gdq_ref[...], gdk_ref[...]
    diff_scale = LOG2E / math.sqrt(DIFF_QK)

    def diff_heads(src, g, scale, dst_ref, c0):
        for j in range(DIFF_HEADS // 2):
            dst_ref[:, c0 + j * LANE:c0 + (j + 1) * LANE] = prep_diff(
                src[:, j * LANE:(j + 1) * LANE], g, scale).astype(BF16)

    dq0 = _dot_nt(h, wt_ref[O_DQ:O_DQ + half, :])
    dq1 = _dot_nt(h, wt_ref[O_DQ + half:O_DK, :])
    diff_heads(dq0, gdq, diff_scale, qd_ref, 0)
    dk0 = _dot_nt(h, wt_ref[O_DK:O_DK + half, :])
    diff_heads(dq1, gdq, diff_scale, qd_ref, half)
    dk1 = _dot_nt(h, wt_ref[O_DK + half:O_DV, :])
    diff_heads(dk0, gdk, 1.0, kd_ref, 0)
    cq = _dot_nt(h, wt_ref[0:O_CKV, :])
    ckv = _dot_nt(h, wt_ref[O_CKV:O_KPE, :])
    kpe = _dot_nt(h, wkpe_ref[...])
    diff_heads(dk1, gdk, 1.0, kd_ref, half)

    hq = ((cq * _inv_rms(cq, Q_LORA)) * glq_ref[...]).astype(BF16)
    q = jnp.dot(hq, wq_ref[...], preferred_element_type=F32)
    hkv = ((ckv * _inv_rms(ckv, KV_LORA)) * glkv_ref[...]).astype(BF16)
    kv = jnp.dot(hkv, wkv_ref[...], preferred_element_type=F32)
    dv0 = _dot_nt(h, wt_ref[O_DV:O_DV + half, :])
    dv1 = _dot_nt(h, wt_ref[O_DV + half:IN_COLS, :])

    gqn, gqr, gkn = gqn_ref[...], gqr_ref[...], gkn_ref[...]
    mla_scale = LOG2E / math.sqrt(MLA_QK)
    nope_w = MLA_HEADS * LANE
    for hd in range(MLA_HEADS):
        qn = q[:, hd * LANE:(hd + 1) * LANE]
        qr = q[:, nope_w + hd * LANE:nope_w + (hd + 1) * LANE]
        ss = jnp.sum(qn * qn + 0.5 * (qr * qr), axis=-1, keepdims=True)
        sq = lax.rsqrt(ss * (1.0 / MLA_QK) + EPS)
        qm_ref[:, hd * HEAD_W:hd * HEAD_W + LANE] = (((qn * sq) * gqn) * mla_scale).astype(BF16)
        qm_ref[:, hd * HEAD_W + LANE:(hd + 1) * HEAD_W] = (rope_mla((qr * sq) * gqr) * mla_scale).astype(BF16)

    vd_ref[:, 0:half] = dv0.astype(BF16)
    vd_ref[:, half:] = dv1.astype(BF16)

    kpe_sq = 0.5 * (kpe * kpe)
    kpe_rot = rope_mla(kpe * gkr_ref[...])
    for hd in range(MLA_HEADS):
        kn = kv[:, hd * LANE:(hd + 1) * LANE]
        ssk = jnp.sum(kn * kn + kpe_sq, axis=-1, keepdims=True)
        sk = lax.rsqrt(ssk * (1.0 / MLA_QK) + EPS)
        km_ref[:, hd * HEAD_W:hd * HEAD_W + LANE] = ((kn * sk) * gkn).astype(BF16)
        km_ref[:, hd * HEAD_W + LANE:(hd + 1) * HEAD_W] = (kpe_rot * sk).astype(BF16)
        vm_ref[:, hd * HEAD_W:hd * HEAD_W + LANE] = (
            kv[:, nope_w + hd * LANE:nope_w + (hd + 1) * LANE].astype(BF16))
        vm_ref[:, hd * HEAD_W + LANE:(hd + 1) * HEAD_W] = jnp.ones((tm, LANE), BF16)


def _qkv(x2d, ga, wt, wkpe, glq, glkv, wq, wkv, gqn, gqr, gkn, gkr, gdq, gdk, cm, sm, cd, sd,
         later_weights, *, seq, tm=256):
    t = x2d.shape[0]
    steps = t // tm
    spt = seq // tm
    tab = pl.BlockSpec((tm, LANE), lambda i: (i % spt, 0))
    vec = lambda n: _resident((1, n))
    widths = [MLA_HEADS * HEAD_W, MLA_HEADS * HEAD_W, MLA_HEADS * HEAD_W,
              DIFF_HEADS * LANE, DIFF_HEADS * LANE, DIFF_HEADS * DIFF_V]
    slabs = [pl.BlockSpec((w.shape[0] // steps, w.shape[1]), lambda i: (i, 0)) for w in later_weights]
    assert all(w.shape[0] % (16 * steps) == 0 for w in later_weights)
    outs = pl.pallas_call(
        _qkv_kernel,
        name="qkv",
        grid=(steps,),
        in_specs=[pl.BlockSpec((tm, D_MODEL), lambda i: (i, 0)), vec(D_MODEL),
                  _resident(wt.shape), _resident(wkpe.shape),
                  vec(Q_LORA), vec(KV_LORA), _resident(wq.shape), _resident(wkv.shape),
                  vec(LANE), vec(LANE), vec(LANE), vec(LANE), vec(LANE), vec(LANE),
                  tab, tab, tab, tab] + slabs,
        out_specs=[pl.BlockSpec((tm, w), lambda i: (i, 0)) for w in widths] + slabs,
        out_shape=([jax.ShapeDtypeStruct((t, w), BF16) for w in widths]
                   + [jax.ShapeDtypeStruct(w.shape, BF16) for w in later_weights]),
        compiler_params=_params(("arbitrary",)),
    )(x2d, ga, wt, wkpe, glq, glkv, wq, wkv, gqn, gqr, gkn, gkr, gdq, gdk, cm, sm, cd, sd, *later_weights)
    return outs[:len(widths)], outs[len(widths):]


def _causal_keep(tq):
    row = lax.broadcasted_iota(jnp.int32, (tq, tq), 0)
    col = lax.broadcasted_iota(jnp.int32, (tq, tq), 1)
    return col <= row


def _scores(q, k_rows, r0, tq, keep):
    s_d = _dot_nt(q, k_rows(r0, r0 + tq))
    s_d = jnp.where(keep, s_d, -jnp.inf)
    s_o = None
    if r0 > 0:
        s_o = _dot_nt(q, k_rows(0, r0))
    return s_o, s_d


def _softmax_pieces(s, *, with_sum, subtract_max):
    s_o, s_d = s
    if subtract_max:
        m = jnp.max(s_d, axis=-1, keepdims=True)
        if s_o is not None:
            m = jnp.maximum(m, jnp.max(s_o, axis=-1, keepdims=True))
            s_o = s_o - m
        s_d = s_d - m
    p_d = jnp.exp2(s_d)
    p_o = None if s_o is None else jnp.exp2(s_o)
    if not with_sum:
        return p_o, p_d, None
    l = jnp.sum(p_d, axis=-1, keepdims=True)
    if p_o is not None:
        l = jnp.sum(p_o, axis=-1, keepdims=True) + l
    return p_o, p_d, l


def _pv(p_o, p_d, v_rows, r0, tq):
    o = jnp.dot(p_d.astype(BF16), v_rows(r0, r0 + tq), preferred_element_type=F32)
    if p_o is not None:
        o = o + jnp.dot(p_o.astype(BF16), v_rows(0, r0), preferred_element_type=F32)
    return o


def _work_items(seq, tq, heads, descending):
    starts = list(range(0, seq, tq))
    if descending:
        starts = starts[::-1]
    return [(j, r0) for j in range(heads) for r0 in starts]


def _cols(ref, j, width):
    return lambda a, b: ref[a:b, j * width:(j + 1) * width]


def _pipelined(items, scores, finish):
    s_next = scores(items[0])
    for k, item in enumerate(items):
        s = s_next
        if k + 1 < len(items):
            s_next = scores(items[k + 1])
        finish(item, s)


def _either_softmax(bound_ref, body):
    small = bound_ref[0, 0] <= SMALL_SCORE_BOUND
    pl.when(small)(functools.partial(body, subtract_max=False))
    pl.when(jnp.logical_not(small))(functools.partial(body, subtract_max=True))


def _mla_attn_kernel(bound_ref, q_ref, k_ref, v_ref, g_ref, o_ref, *, tq, heads):
    seq = q_ref.shape[0]

    def body(subtract_max):
        keep = _causal_keep(tq)
        g = g_ref[...]

        def scores(item):
            j, r0 = item
            return _scores(_cols(q_ref, j, HEAD_W)(r0, r0 + tq), _cols(k_ref, j, HEAD_W), r0, tq, keep)

        def finish(item, s):
            j, r0 = item
            p_o, p_d, _ = _softmax_pieces(s, with_sum=False, subtract_max=subtract_max)
            ol = _pv(p_o, p_d, _cols(v_ref, j, HEAD_W), r0, tq)
            o = ol[:, :MLA_V] / ol[:, MLA_V:]
            o_ref[r0:r0 + tq, j * MLA_V:(j + 1) * MLA_V] = ((o * _inv_rms(o, MLA_V)) * g).astype(BF16)

        _pipelined(_work_items(seq, tq, heads, descending=True), scores, finish)

    _either_softmax(bound_ref, body)


def _diff_attn_kernel(bound_ref, q_ref, k_ref, v_ref, g_ref, lam_ref, o_ref, *, tq, heads):
    _either_softmax(bound_ref, functools.partial(_diff_attn_body, q_ref, k_ref, v_ref, g_ref, lam_ref, o_ref,
                                                 tq=tq, heads=heads))


def _diff_attn_body(q_ref, k_ref, v_ref, g_ref, lam_ref, o_ref, *, tq, heads, subtract_max):
    seq = q_ref.shape[0]
    keep = _causal_keep(tq)
    g = g_ref[...]
    lp = lam_ref[...]
    lam = (jnp.exp(jnp.sum(lp[0:1, :] * lp[1:2, :], axis=-1, keepdims=True))
           - jnp.exp(jnp.sum(lp[2:3, :] * lp[3:4, :], axis=-1, keepdims=True))
           + LAMBDA_INIT)
    lane = lax.broadcasted_iota(jnp.int32, (tq, LANE), 1)
    first = lane < DIFF_QK

    def scores(item):
        j, r0 = item
        q = _cols(q_ref, j, LANE)(r0, r0 + tq)
        k_rows = _cols(k_ref, j, LANE)
        q1 = jnp.where(first, q, jnp.zeros_like(q))
        q2 = jnp.where(first, jnp.zeros_like(q), q)
        return _scores(q1, k_rows, r0, tq, keep), _scores(q2, k_rows, r0, tq, keep)

    def finish(item, s):
        j, r0 = item
        p1_o, p1_d, l1 = _softmax_pieces(s[0], with_sum=True, subtract_max=subtract_max)
        p2_o, p2_d, l2 = _softmax_pieces(s[1], with_sum=True, subtract_max=subtract_max)
        c = lam * (l1 / l2)
        a_o = None if p1_o is None else p1_o - c * p2_o
        o = _pv(a_o, p1_d - c * p2_d, _cols(v_ref, j, DIFF_V), r0, tq) / l1
        o_ref[r0:r0 + tq, j * DIFF_V:(j + 1) * DIFF_V] = (
            ((o * _inv_rms(o, DIFF_V)) * g) * (1.0 - LAMBDA_INIT)).astype(BF16)

    _pipelined(_work_items(seq, tq, heads, descending=False), scores, finish)


def _head_spec(seq, width, heads):
    return pl.BlockSpec((seq, heads * width), lambda b, h: (b, h))


_SCALAR_SPEC = pl.BlockSpec(memory_space=pltpu.SMEM)


def _score_bound(gq, gk, d):
    b = 1.01 * LOG2E * math.sqrt(d) * jnp.max(jnp.abs(gq)) * jnp.max(jnp.abs(gk))
    return b.astype(F32).reshape(1, 1)


def _mla_attn(bound, qm, km, vm, g, *, batch, seq, tq=256, heads=2):
    t = qm.shape[0]
    return pl.pallas_call(
        functools.partial(_mla_attn_kernel, tq=tq, heads=heads),
        name="mla_attn",
        grid=(batch, MLA_HEADS // heads),
        in_specs=[_SCALAR_SPEC, _head_spec(seq, HEAD_W, heads), _head_spec(seq, HEAD_W, heads), _head_spec(seq, HEAD_W, heads),
                  pl.BlockSpec((1, MLA_V), lambda b, h: (0, 0))],
        out_specs=_head_spec(seq, MLA_V, heads),
        out_shape=jax.ShapeDtypeStruct((t, MLA_HEADS * MLA_V), BF16),
        compiler_params=_params(("arbitrary", "arbitrary")),
    )(bound, qm, km, vm, g)


def _diff_attn(bound, qd, kd, vd, g, lam_params, *, batch, seq, tq=256, heads=2):
    t = qd.shape[0]
    return pl.pallas_call(
        functools.partial(_diff_attn_kernel, tq=tq, heads=heads),
        name="diff_attn",
        grid=(batch, DIFF_HEADS // heads),
        in_specs=[_SCALAR_SPEC, _head_spec(seq, LANE, heads), _head_spec(seq, LANE, heads), _head_spec(seq, DIFF_V, heads),
                  pl.BlockSpec((1, DIFF_V), lambda b, h: (0, 0)),
                  pl.BlockSpec((4, DIFF_QK), lambda b, h: (0, 0))],
        out_specs=_head_spec(seq, DIFF_V, heads),
        out_shape=jax.ShapeDtypeStruct((t, DIFF_HEADS * DIFF_V), BF16),
        compiler_params=_params(("arbitrary", "arbitrary")),
    )(bound, qd, kd, vd, g, lam_params)


def _out_proj_kernel(x_ref, om_ref, od_ref, w_ref, o_ref):
    km = om_ref.shape[1]
    acc = jnp.dot(om_ref[...], w_ref[0:km, :], preferred_element_type=F32)
    acc = acc + jnp.dot(od_ref[...], w_ref[km:, :], preferred_element_type=F32)
    o_ref[...] = x_ref[...] + acc


def _out_proj(x2d, om, od, w, *, tm=512):
    t = x2d.shape[0]
    return pl.pallas_call(
        _out_proj_kernel,
        name="out_proj",
        grid=(t // tm,),
        in_specs=[pl.BlockSpec((tm, D_MODEL), lambda i: (i, 0)),
                  pl.BlockSpec((tm, om.shape[1]), lambda i: (i, 0)),
                  pl.BlockSpec((tm, od.shape[1]), lambda i: (i, 0)),
                  _resident(w.shape)],
        out_specs=pl.BlockSpec((tm, D_MODEL), lambda i: (i, 0)),
        out_shape=jax.ShapeDtypeStruct((t, D_MODEL), F32),
        compiler_params=_params(("arbitrary",)),
    )(x2d, om, od, w)


def _ffn_kernel(x_ref, g_ref, wg_ref, wu_ref, wd_ref, o_ref, h_ref, *, out_chunk):
    @pl.when(pl.program_id(1) == 0)
    def _():
        x = x_ref[...]
        h_ref[...] = ((x * _inv_rms(x, D_MODEL)) * g_ref[...]).astype(BF16)
        o_ref[...] = x

    h = h_ref[...]
    gate = jnp.dot(h, wg_ref[...], preferred_element_type=F32)
    up = jnp.dot(h, wu_ref[...], preferred_element_type=F32)
    act = (gate * jax.nn.sigmoid(gate) * up).astype(BF16)
    for c0 in range(0, o_ref.shape[1], out_chunk):
        o_ref[:, c0:c0 + out_chunk] += jnp.dot(act, wd_ref[:, c0:c0 + out_chunk],
                                               preferred_element_type=F32)


def _ffn(x1, g, wg, wu, wd, *, tm=1024, tf=512):
    t = x1.shape[0]
    return pl.pallas_call(
        functools.partial(_ffn_kernel, out_chunk=512),
        name="ffn",
        grid=(t // tm, D_FF // tf),
        in_specs=[pl.BlockSpec((tm, D_MODEL), lambda i, f: (i, 0)),
                  pl.BlockSpec((1, D_MODEL), lambda i, f: (0, 0)),
                  pl.BlockSpec((D_MODEL, tf), lambda i, f: (0, f)),
                  pl.BlockSpec((D_MODEL, tf), lambda i, f: (0, f)),
                  pl.BlockSpec((tf, D_MODEL), lambda i, f: (f, 0))],
        out_specs=pl.BlockSpec((tm, D_MODEL), lambda i, f: (i, 0)),
        out_shape=jax.ShapeDtypeStruct((t, D_MODEL), F32),
        scratch_shapes=[pltpu.VMEM((tm, D_MODEL), BF16)],
        compiler_params=_params(("arbitrary", "arbitrary")),
    )(x1, g, wg, wu, wd)


def _rope_tables(seq):
    pos = jnp.arange(seq, dtype=jnp.int32).astype(F32)

    def angles(r):
        freqs = 1.0 / (ROPE_THETA ** (jnp.arange(0, r, 2, dtype=F32) / r))
        return pos[:, None] * freqs[None, :]

    am = angles(MLA_ROPE)
    cos, sin = jnp.cos(am), jnp.sin(am)
    zeros = jnp.zeros((seq, LANE - MLA_ROPE), F32)
    cm = jnp.concatenate([cos, cos, zeros], axis=-1)
    sm = jnp.concatenate([-sin, sin, zeros], axis=-1)

    ad = angles(DIFF_ROT)
    cos, sin = jnp.cos(ad), jnp.sin(ad)
    rest = DIFF_QK - DIFF_ROT
    cd = jnp.concatenate([cos, cos, jnp.ones((seq, rest), F32)], axis=-1)
    sd = jnp.concatenate([-sin, sin, jnp.zeros((seq, rest), F32)], axis=-1)
    return cm, sm, jnp.tile(cd, (1, 2)), jnp.tile(sd, (1, 2))


def _with_partner(v):
    return jnp.concatenate([v, jnp.roll(v, MLA_ROPE // 2, axis=-1)], axis=-1)


def kernel(x, attn_norm, w_in, q_latent_norm, w_q_up, kv_latent_norm, w_kv_up, mla_q_norm, mla_k_norm, mla_out_norm, diff_q_norm, diff_k_norm, lambda_q1, lambda_k1, lambda_q2, lambda_k2, diff_out_norm, w_o, ffn_norm, w_gate, w_up, w_down):
    batch, seq, d = x.shape
    assert d == D_MODEL and attn_norm.shape[0] == 1
    t = batch * seq
    x2d = x.reshape(t, d)
    l = 0

    wt = jnp.swapaxes(w_in[l], 0, 1).astype(BF16)
    wkpe = _with_partner(wt[O_KPE:O_DQ].T).T
    wq = w_q_up[l].astype(BF16).reshape(Q_LORA, MLA_HEADS, MLA_QK)
    wq = jnp.concatenate([wq[:, :, :MLA_NOPE].reshape(Q_LORA, -1),
                          _with_partner(wq[:, :, MLA_NOPE:]).reshape(Q_LORA, -1)], axis=1)
    wkv = w_kv_up[l].astype(BF16).reshape(KV_LORA, MLA_HEADS, MLA_NOPE + MLA_V)
    wkv = jnp.concatenate([wkv[:, :, :MLA_NOPE].reshape(KV_LORA, -1),
                           wkv[:, :, MLA_NOPE:].reshape(KV_LORA, -1)], axis=1)

    gqn, gqr = mla_q_norm[l:l + 1, :MLA_NOPE], _with_partner(mla_q_norm[l:l + 1, MLA_NOPE:])
    gkn, gkr = mla_k_norm[l:l + 1, :MLA_NOPE], _with_partner(mla_k_norm[l:l + 1, MLA_NOPE:])
    gdq = jnp.tile(diff_q_norm[l:l + 1], (1, 2))
    gdk = jnp.tile(diff_k_norm[l:l + 1], (1, 2))
    lam_params = jnp.concatenate([lambda_q1[l:l + 1], lambda_k1[l:l + 1],
                                  lambda_q2[l:l + 1], lambda_k2[l:l + 1]], axis=0)
    cm, sm, cd, sd = _rope_tables(seq)

    (qm, km, vm, qd, kd, vd), (wo16, wg16, wu16, wd16) = _qkv(
        x2d, attn_norm[l:l + 1], wt, wkpe, q_latent_norm[l:l + 1], kv_latent_norm[l:l + 1], wq, wkv,
        gqn, gqr, gkn, gkr, gdq, gdk, cm, sm, cd, sd, (w_o[l], w_gate[l], w_up[l], w_down[l]), seq=seq)
    o_mla = _mla_attn(_score_bound(mla_q_norm[l], mla_k_norm[l], MLA_QK), qm, km, vm,
                      mla_out_norm[l:l + 1], batch=batch, seq=seq)
    o_diff = _diff_attn(_score_bound(diff_q_norm[l], diff_k_norm[l], DIFF_QK), qd, kd, vd,
                        diff_out_norm[l:l + 1], lam_params, batch=batch, seq=seq)
    x1 = _out_proj(x2d, o_mla, o_diff, wo16)
    out = _ffn(x1, ffn_norm[l:l + 1], wg16, wu16, wd16)
    return out.reshape(batch, seq, d)
```

```python
import functools
import math

import jax
import jax.numpy as jnp
from jax import lax
from jax.experimental import pallas as pl
from jax.experimental.pallas import tpu as pltpu

F32 = jnp.float32
BF16 = jnp.bfloat16

D_MODEL = 2048
MLA_HEADS = 8
MLA_NOPE = 128
MLA_ROPE = 64
MLA_QK = MLA_NOPE + MLA_ROPE
MLA_V = 128
Q_LORA = 512
KV_LORA = 512
DIFF_HEADS = 8
DIFF_QK = 64
DIFF_ROT = 16
DIFF_V = 128
ROPE_THETA = 500000.0
D_FF = 5632
EPS = 1e-6
LAMBDA_INIT = 0.8 - 0.6 * math.exp(-0.3 * 0)
LOG2E = math.log2(math.e)

SMALL_SCORE_BOUND = 32.0
LANE = 128
HEAD_W = 2 * LANE
VMEM_LIMIT = 56 * 1024 * 1024

O_CKV = Q_LORA
O_KPE = O_CKV + KV_LORA
O_DQ = O_KPE + MLA_ROPE
O_DK = O_DQ + DIFF_HEADS * 2 * DIFF_QK
O_DV = O_DK + DIFF_HEADS * 2 * DIFF_QK
IN_COLS = O_DV + DIFF_HEADS * DIFF_V

_NT = (((1,), (1,)), ((), ()))


def _params(sem):
    return pltpu.CompilerParams(dimension_semantics=sem, vmem_limit_bytes=VMEM_LIMIT)


def _resident(shape):
    return pl.BlockSpec(shape, lambda *_: (0,) * len(shape), pipeline_mode=pl.Buffered(1))


def _inv_rms(xf, n):
    return lax.rsqrt(jnp.sum(xf * xf, axis=-1, keepdims=True) * (1.0 / n) + EPS)


def _dot_nt(a, b):
    return lax.dot_general(a, b, _NT, preferred_element_type=F32)


def _qkv_kernel(x_ref, ga_ref, wt_ref, wkpe_ref, glq_ref, glkv_ref, wq_ref, wkv_ref,
                gqn_ref, gqr_ref, gkn_ref, gkr_ref, gdq_ref, gdk_ref,
                cm_ref, sm_ref, cd_ref, sd_ref,
                wo_ref, wg_ref, wu_ref, wd_ref,
                qm_ref, km_ref, vm_ref, qd_ref, kd_ref, vd_ref,
                wo16_ref, wg16_ref, wu16_ref, wd16_ref):
    def cast_slab(src, dst):
        dst[...] = src[...].astype(BF16)

    tm = x_ref.shape[0]
    cm, sm, cd, sd = cm_ref[...], sm_ref[...], cd_ref[...], sd_ref[...]
    x = x_ref[...]
    h = ((x * _inv_rms(x, D_MODEL)) * ga_ref[...]).astype(BF16)

    lane = lax.broadcasted_iota(jnp.int32, (tm, LANE), 1)
    lo = lane < DIFF_QK
    is_x1 = (lane % DIFF_QK) < (DIFF_ROT // 2)

    def prep_diff(xh, g, scale):
        sq = xh * xh
        s_lo = jnp.sum(jnp.where(lo, sq, 0.0), axis=-1, keepdims=True)
        s_hi = jnp.sum(jnp.where(lo, 0.0, sq), axis=-1, keepdims=True)
        inv = lax.rsqrt(jnp.where(lo, s_lo, s_hi) * (1.0 / DIFF_QK) + EPS)
        y = (xh * inv) * g
        partner = jnp.where(is_x1, pltpu.roll(y, LANE - DIFF_ROT // 2, 1), pltpu.roll(y, DIFF_ROT // 2, 1))
        out = y * cd + partner * sd
        return out * scale if scale != 1.0 else out

    def rope_mla(y):
        return y * cm + pltpu.roll(y, MLA_ROPE, 1) * sm

    half = DIFF_HEADS // 2 * LANE
    gdq, gdk = gdq_ref[...], gdk_ref[...]
    diff_scale = LOG2E / math.sqrt(DIFF_QK)

    def diff_heads(src, g, scale, dst_ref, c0):
        for j in range(DIFF_HEADS // 2):
            dst_ref[:, c0 + j * LANE:c0 + (j + 1) * LANE] = prep_diff(
                src[:, j * LANE:(j + 1) * LANE], g, scale).astype(BF16)

    gqn, gqr, gkn = gqn_ref[...], gqr_ref[...], gkn_ref[...]
    mla_scale = LOG2E / math.sqrt(MLA_QK)
    nope_w = MLA_HEADS * LANE

    def q_heads(q):
        for hd in range(MLA_HEADS):
            qn = q[:, hd * LANE:(hd + 1) * LANE]
            qr = q[:, nope_w + hd * LANE:nope_w + (hd + 1) * LANE]
            ss = jnp.sum(qn * qn + 0.5 * (qr * qr), axis=-1, keepdims=True)
            sq = lax.rsqrt(ss * (1.0 / MLA_QK) + EPS)
            qm_ref[:, hd * HEAD_W:hd * HEAD_W + LANE] = (((qn * sq) * gqn) * mla_scale).astype(BF16)
            qm_ref[:, hd * HEAD_W + LANE:(hd + 1) * HEAD_W] = (
                rope_mla((qr * sq) * gqr) * mla_scale).astype(BF16)

    def k_heads(kv, kpe):
        kpe_sq = 0.5 * (kpe * kpe)
        kpe_rot = rope_mla(kpe * gkr_ref[...])
        for hd in range(MLA_HEADS):
            kn = kv[:, hd * LANE:(hd + 1) * LANE]
            ssk = jnp.sum(kn * kn + kpe_sq, axis=-1, keepdims=True)
            sk = lax.rsqrt(ssk * (1.0 / MLA_QK) + EPS)
            km_ref[:, hd * HEAD_W:hd * HEAD_W + LANE] = ((kn * sk) * gkn).astype(BF16)
            km_ref[:, hd * HEAD_W + LANE:(hd + 1) * HEAD_W] = (kpe_rot * sk).astype(BF16)
            vm_ref[:, hd * HEAD_W:hd * HEAD_W + LANE] = (
                kv[:, nope_w + hd * LANE:nope_w + (hd + 1) * LANE].astype(BF16))
            vm_ref[:, hd * HEAD_W + LANE:(hd + 1) * HEAD_W] = jnp.ones((tm, LANE), BF16)

    cq = _dot_nt(h, wt_ref[0:O_CKV, :])
    ckv = _dot_nt(h, wt_ref[O_CKV:O_KPE, :])
    kpe = _dot_nt(h, wkpe_ref[...])
    cast_slab(wg_ref, wg16_ref)
    hq = ((cq * _inv_rms(cq, Q_LORA)) * glq_ref[...]).astype(BF16)
    q = jnp.dot(hq, wq_ref[...], preferred_element_type=F32)
    dq0 = _dot_nt(h, wt_ref[O_DQ:O_DQ + half, :])
    hkv = ((ckv * _inv_rms(ckv, KV_LORA)) * glkv_ref[...]).astype(BF16)
    kv = jnp.dot(hkv, wkv_ref[...], preferred_element_type=F32)
    dq1 = _dot_nt(h, wt_ref[O_DQ + half:O_DK, :])
    cast_slab(wu_ref, wu16_ref)
    q_heads(q)
    dk0 = _dot_nt(h, wt_ref[O_DK:O_DK + half, :])
    k_heads(kv, kpe)
    dk1 = _dot_nt(h, wt_ref[O_DK + half:O_DV, :])
    cast_slab(wd_ref, wd16_ref)
    diff_heads(dq0, gdq, diff_scale, qd_ref, 0)
    vd_ref[:, 0:half] = _dot_nt(h, wt_ref[O_DV:O_DV + half, :]).astype(BF16)
    diff_heads(dq1, gdq, diff_scale, qd_ref, half)
    cast_slab(wo_ref, wo16_ref)
    diff_heads(dk0, gdk, 1.0, kd_ref, 0)
    vd_ref[:, half:] = _dot_nt(h, wt_ref[O_DV + half:IN_COLS, :]).astype(BF16)
    diff_heads(dk1, gdk, 1.0, kd_ref, half)


def _qkv(x2d, ga, wt, wkpe, glq, glkv, wq, wkv, gqn, gqr, gkn, gkr, gdq, gdk, cm, sm, cd, sd,
         later_weights, *, seq, tm=256):
    t = x2d.shape[0]
    steps = t // tm
    spt = seq // tm
    tab = pl.BlockSpec((tm, LANE), lambda i: (i % spt, 0))
    vec = lambda n: _resident((1, n))
    widths = [MLA_HEADS * HEAD_W, MLA_HEADS * HEAD_W, MLA_HEADS * HEAD_W,
              DIFF_HEADS * LANE, DIFF_HEADS * LANE, DIFF_HEADS * DIFF_V]
    slabs = [pl.BlockSpec((w.shape[0] // steps, w.shape[1]), lambda i: (i, 0)) for w in later_weights]
    assert all(w.shape[0] % (16 * steps) == 0 for w in later_weights)
    outs = pl.pallas_call(
        _qkv_kernel,
        name="qkv",
        grid=(steps,),
        in_specs=[pl.BlockSpec((tm, D_MODEL), lambda i: (i, 0)), vec(D_MODEL),
                  _resident(wt.shape), _resident(wkpe.shape),
                  vec(Q_LORA), vec(KV_LORA), _resident(wq.shape), _resident(wkv.shape),
                  vec(LANE), vec(LANE), vec(LANE), vec(LANE), vec(LANE), vec(LANE),
                  tab, tab, tab, tab] + slabs,
        out_specs=[pl.BlockSpec((tm, w), lambda i: (i, 0)) for w in widths] + slabs,
        out_shape=([jax.ShapeDtypeStruct((t, w), BF16) for w in widths]
                   + [jax.ShapeDtypeStruct(w.shape, BF16) for w in later_weights]),
        compiler_params=_params(("arbitrary",)),
    )(x2d, ga, wt, wkpe, glq, glkv, wq, wkv, gqn, gqr, gkn, gkr, gdq, gdk, cm, sm, cd, sd, *later_weights)
    return outs[:len(widths)], outs[len(widths):]


def _causal_keep(tq):
    row = lax.broadcasted_iota(jnp.int32, (tq, tq), 0)
    col = lax.broadcasted_iota(jnp.int32, (tq, tq), 1)
    return col <= row


def _scores(q, k_rows, r0, tq, keep):
    s_d = _dot_nt(q, k_rows(r0, r0 + tq))
    s_d = jnp.where(keep, s_d, -jnp.inf)
    s_o = None
    if r0 > 0:
        s_o = _dot_nt(q, k_rows(0, r0))
    return s_o, s_d


def _softmax_pieces(s, *, with_sum, subtract_max):
    s_o, s_d = s
    if subtract_max:
        m = jnp.max(s_d, axis=-1, keepdims=True)
        if s_o is not None:
            m = jnp.maximum(m, jnp.max(s_o, axis=-1, keepdims=True))
            s_o = s_o - m
        s_d = s_d - m
    p_d = jnp.exp2(s_d)
    p_o = None if s_o is None else jnp.exp2(s_o)
    if not with_sum:
        return p_o, p_d, None
    l = jnp.sum(p_d, axis=-1, keepdims=True)
    if p_o is not None:
        l = jnp.sum(p_o, axis=-1, keepdims=True) + l
    return p_o, p_d, l


def _pv(p_o, p_d, v_rows, r0, tq):
    o = jnp.dot(p_d.astype(BF16), v_rows(r0, r0 + tq), preferred_element_type=F32)
    if p_o is not None:
        o = o + jnp.dot(p_o.astype(BF16), v_rows(0, r0), preferred_element_type=F32)
    return o


def _work_items(seq, tq, heads, descending):
    starts = list(range(0, seq, tq))
    if descending:
        starts = starts[::-1]
    return [(j, r0) for j in range(heads) for r0 in starts]


def _cols(ref, j, width):
    return lambda a, b: ref[a:b, j * width:(j + 1) * width]


def _pipelined(items, scores, finish):
    s_next = scores(items[0])
    for k, item in enumerate(items):
        s = s_next
        if k + 1 < len(items):
            s_next = scores(items[k + 1])
        finish(item, s)


def _either_softmax(bound_ref, body):
    small = bound_ref[0, 0] <= SMALL_SCORE_BOUND
    pl.when(small)(functools.partial(body, subtract_max=False))
    pl.when(jnp.logical_not(small))(functools.partial(body, subtract_max=True))


def _mla_attn_kernel(bound_ref, q_ref, k_ref, v_ref, g_ref, o_ref, *, tq, heads):
    seq = q_ref.shape[0]

    def body(subtract_max):
        keep = _causal_keep(tq)
        g = g_ref[...]

        def scores(item):
            j, r0 = item
            return _scores(_cols(q_ref, j, HEAD_W)(r0, r0 + tq), _cols(k_ref, j, HEAD_W), r0, tq, keep)

        def finish(item, s):
            j, r0 = item
            p_o, p_d, _ = _softmax_pieces(s, with_sum=False, subtract_max=subtract_max)
            ol = _pv(p_o, p_d, _cols(v_ref, j, HEAD_W), r0, tq)
            o = ol[:, :MLA_V] / ol[:, MLA_V:]
            o_ref[r0:r0 + tq, j * MLA_V:(j + 1) * MLA_V] = ((o * _inv_rms(o, MLA_V)) * g).astype(BF16)

        _pipelined(_work_items(seq, tq, heads, descending=True), scores, finish)

    _either_softmax(bound_ref, body)


def _diff_attn_kernel(bound_ref, q_ref, k_ref, v_ref, g_ref, lam_ref, o_ref, *, tq, heads):
    _either_softmax(bound_ref, functools.partial(_diff_attn_body, q_ref, k_ref, v_ref, g_ref, lam_ref, o_ref,
                                                 tq=tq, heads=heads))


def _diff_attn_body(q_ref, k_ref, v_ref, g_ref, lam_ref, o_ref, *, tq, heads, subtract_max):
    seq = q_ref.shape[0]
    keep = _causal_keep(tq)
    g = g_ref[...]
    lp = lam_ref[...]
    lam = (jnp.exp(jnp.sum(lp[0:1, :] * lp[1:2, :], axis=-1, keepdims=True))
           - jnp.exp(jnp.sum(lp[2:3, :] * lp[3:4, :], axis=-1, keepdims=True))
           + LAMBDA_INIT)
    lane = lax.broadcasted_iota(jnp.int32, (tq, LANE), 1)
    first = lane < DIFF_QK

    def scores(item):
        j, r0 = item
        q = _cols(q_ref, j, LANE)(r0, r0 + tq)
        k_rows = _cols(k_ref, j, LANE)
        q1 = jnp.where(first, q, jnp.zeros_like(q))
        q2 = jnp.where(first, jnp.zeros_like(q), q)
        return _scores(q1, k_rows, r0, tq, keep), _scores(q2, k_rows, r0, tq, keep)

    def finish(item, s):
        j, r0 = item
        p1_o, p1_d, l1 = _softmax_pieces(s[0], with_sum=True, subtract_max=subtract_max)
        p2_o, p2_d, l2 = _softmax_pieces(s[1], with_sum=True, subtract_max=subtract_max)
        c = lam * (l1 / l2)
        a_o = None if p1_o is None else p1_o - c * p2_o
        o = _pv(a_o, p1_d - c * p2_d, _cols(v_ref, j, DIFF_V), r0, tq) / l1
        o_ref[r0:r0 + tq, j * DIFF_V:(j + 1) * DIFF_V] = (
            ((o * _inv_rms(o, DIFF_V)) * g) * (1.0 - LAMBDA_INIT)).astype(BF16)

    _pipelined(_work_items(seq, tq, heads, descending=False), scores, finish)


def _head_spec(seq, width, heads):
    return pl.BlockSpec((seq, heads * width), lambda b, h: (b, h))


_SCALAR_SPEC = pl.BlockSpec(memory_space=pltpu.SMEM)


def _score_bound(gq, gk, d):
    b = 1.01 * LOG2E * math.sqrt(d) * jnp.max(jnp.abs(gq)) * jnp.max(jnp.abs(gk))
    return b.astype(F32).reshape(1, 1)


def _mla_attn(bound, qm, km, vm, g, *, batch, seq, tq=256, heads=2):
    t = qm.shape[0]
    return pl.pallas_call(
        functools.partial(_mla_attn_kernel, tq=tq, heads=heads),
        name="mla_attn",
        grid=(batch, MLA_HEADS // heads),
        in_specs=[_SCALAR_SPEC, _head_spec(seq, HEAD_W, heads), _head_spec(seq, HEAD_W, heads), _head_spec(seq, HEAD_W, heads),
                  pl.BlockSpec((1, MLA_V), lambda b, h: (0, 0))],
        out_specs=_head_spec(seq, MLA_V, heads),
        out_shape=jax.ShapeDtypeStruct((t, MLA_HEADS * MLA_V), BF16),
        compiler_params=_params(("arbitrary", "arbitrary")),
    )(bound, qm, km, vm, g)


def _diff_attn(bound, qd, kd, vd, g, lam_params, *, batch, seq, tq=256, heads=2):
    t = qd.shape[0]
    return pl.pallas_call(
        functools.partial(_diff_attn_kernel, tq=tq, heads=heads),
        name="diff_attn",
        grid=(batch, DIFF_HEADS // heads),
        in_specs=[_SCALAR_SPEC, _head_spec(seq, LANE, heads), _head_spec(seq, LANE, heads), _head_spec(seq, DIFF_V, heads),
                  pl.BlockSpec((1, DIFF_V), lambda b, h: (0, 0)),
                  pl.BlockSpec((4, DIFF_QK), lambda b, h: (0, 0))],
        out_specs=_head_spec(seq, DIFF_V, heads),
        out_shape=jax.ShapeDtypeStruct((t, DIFF_HEADS * DIFF_V), BF16),
        compiler_params=_params(("arbitrary", "arbitrary")),
    )(bound, qd, kd, vd, g, lam_params)


def _out_proj_kernel(x_ref, om_ref, od_ref, w_ref, o_ref):
    km = om_ref.shape[1]
    acc = jnp.dot(om_ref[...], w_ref[0:km, :], preferred_element_type=F32)
    acc = acc + jnp.dot(od_ref[...], w_ref[km:, :], preferred_element_type=F32)
    o_ref[...] = x_ref[...] + acc


def _out_proj(x2d, om, od, w, *, tm=512):
    t = x2d.shape[0]
    return pl.pallas_call(
        _out_proj_kernel,
        name="out_proj",
        grid=(t // tm,),
        in_specs=[pl.BlockSpec((tm, D_MODEL), lambda i: (i, 0)),
                  pl.BlockSpec((tm, om.shape[1]), lambda i: (i, 0)),
                  pl.BlockSpec((tm, od.shape[1]), lambda i: (i, 0)),
                  _resident(w.shape)],
        out_specs=pl.BlockSpec((tm, D_MODEL), lambda i: (i, 0)),
        out_shape=jax.ShapeDtypeStruct((t, D_MODEL), F32),
        compiler_params=_params(("arbitrary",)),
    )(x2d, om, od, w)


def _ffn_kernel(x_ref, g_ref, wg_ref, wu_ref, wd_ref, o_ref, h_ref, *, out_chunk):
    @pl.when(pl.program_id(1) == 0)
    def _():
        x = x_ref[...]
        h_ref[...] = ((x * _inv_rms(x, D_MODEL)) * g_ref[...]).astype(BF16)
        o_ref[...] = x

    h = h_ref[...]
    gate = jnp.dot(h, wg_ref[...], preferred_element_type=F32)
    up = jnp.dot(h, wu_ref[...], preferred_element_type=F32)
    act = (gate * jax.nn.sigmoid(gate) * up).astype(BF16)
    for c0 in range(0, o_ref.shape[1], out_chunk):
        o_ref[:, c0:c0 + out_chunk] += jnp.dot(act, wd_ref[:, c0:c0 + out_chunk],
                                               preferred_element_type=F32)


def _ffn(x1, g, wg, wu, wd, *, tm=1024, tf=512):
    t = x1.shape[0]
    return pl.pallas_call(
        functools.partial(_ffn_kernel, out_chunk=512),
        name="ffn",
        grid=(t // tm, D_FF // tf),
        in_specs=[pl.BlockSpec((tm, D_MODEL), lambda i, f: (i, 0)),
                  pl.BlockSpec((1, D_MODEL), lambda i, f: (0, 0)),
                  pl.BlockSpec((D_MODEL, tf), lambda i, f: (0, f)),
                  pl.BlockSpec((D_MODEL, tf), lambda i, f: (0, f)),
                  pl.BlockSpec((tf, D_MODEL), lambda i, f: (f, 0))],
        out_specs=pl.BlockSpec((tm, D_MODEL), lambda i, f: (i, 0)),
        out_shape=jax.ShapeDtypeStruct((t, D_MODEL), F32),
        scratch_shapes=[pltpu.VMEM((tm, D_MODEL), BF16)],
        compiler_params=_params(("arbitrary", "arbitrary")),
    )(x1, g, wg, wu, wd)


def _rope_tables(seq):
    pos = jnp.arange(seq, dtype=jnp.int32).astype(F32)

    def angles(r):
        freqs = 1.0 / (ROPE_THETA ** (jnp.arange(0, r, 2, dtype=F32) / r))
        return pos[:, None] * freqs[None, :]

    am = angles(MLA_ROPE)
    cos, sin = jnp.cos(am), jnp.sin(am)
    zeros = jnp.zeros((seq, LANE - MLA_ROPE), F32)
    cm = jnp.concatenate([cos, cos, zeros], axis=-1)
    sm = jnp.concatenate([-sin, sin, zeros], axis=-1)

    ad = angles(DIFF_ROT)
    cos, sin = jnp.cos(ad), jnp.sin(ad)
    rest = DIFF_QK - DIFF_ROT
    cd = jnp.concatenate([cos, cos, jnp.ones((seq, rest), F32)], axis=-1)
    sd = jnp.concatenate([-sin, sin, jnp.zeros((seq, rest), F32)], axis=-1)
    return cm, sm, jnp.tile(cd, (1, 2)), jnp.tile(sd, (1, 2))


def _with_partner(v):
    return jnp.concatenate([v, jnp.roll(v, MLA_ROPE // 2, axis=-1)], axis=-1)


def kernel(x, attn_norm, w_in, q_latent_norm, w_q_up, kv_latent_norm, w_kv_up, mla_q_norm, mla_k_norm, mla_out_norm, diff_q_norm, diff_k_norm, lambda_q1, lambda_k1, lambda_q2, lambda_k2, diff_out_norm, w_o, ffn_norm, w_gate, w_up, w_down):
    batch, seq, d = x.shape
    assert d == D_MODEL and attn_norm.shape[0] == 1
    t = batch * seq
    x2d = x.reshape(t, d)
    l = 0

    wt = jnp.swapaxes(w_in[l], 0, 1).astype(BF16)
    wkpe = _with_partner(wt[O_KPE:O_DQ].T).T
    wq = w_q_up[l].astype(BF16).reshape(Q_LORA, MLA_HEADS, MLA_QK)
    wq = jnp.concatenate([wq[:, :, :MLA_NOPE].reshape(Q_LORA, -1),
                          _with_partner(wq[:, :, MLA_NOPE:]).reshape(Q_LORA, -1)], axis=1)
    wkv = w_kv_up[l].astype(BF16).reshape(KV_LORA, MLA_HEADS, MLA_NOPE + MLA_V)
    wkv = jnp.concatenate([wkv[:, :, :MLA_NOPE].reshape(KV_LORA, -1),
                           wkv[:, :, MLA_NOPE:].reshape(KV_LORA, -1)], axis=1)

    gqn, gqr = mla_q_norm[l:l + 1, :MLA_NOPE], _with_partner(mla_q_norm[l:l + 1, MLA_NOPE:])
    gkn, gkr = mla_k_norm[l:l + 1, :MLA_NOPE], _with_partner(mla_k_norm[l:l + 1, MLA_NOPE:])
    gdq = jnp.tile(diff_q_norm[l:l + 1], (1, 2))
    gdk = jnp.tile(diff_k_norm[l:l + 1], (1, 2))
    lam_params = jnp.concatenate([lambda_q1[l:l + 1], lambda_k1[l:l + 1],
                                  lambda_q2[l:l + 1], lambda_k2[l:l + 1]], axis=0)
    cm, sm, cd, sd = _rope_tables(seq)

    (qm, km, vm, qd, kd, vd), (wo16, wg16, wu16, wd16) = _qkv(
        x2d, attn_norm[l:l + 1], wt, wkpe, q_latent_norm[l:l + 1], kv_latent_norm[l:l + 1], wq, wkv,
        gqn, gqr, gkn, gkr, gdq, gdk, cm, sm, cd, sd, (w_o[l], w_gate[l], w_up[l], w_down[l]), seq=seq)
    o_mla = _mla_attn(_score_bound(mla_q_norm[l], mla_k_norm[l], MLA_QK), qm, km, vm,
                      mla_out_norm[l:l + 1], batch=batch, seq=seq)
    o_diff = _diff_attn(_score_bound(diff_q_norm[l], diff_k_norm[l], DIFF_QK), qd, kd, vd,
                        diff_out_norm[l:l + 1], lam_params, batch=batch, seq=seq)
    x1 = _out_proj(x2d, o_mla, o_diff, wo16)
    out = _ffn(x1, ffn_norm[l:l + 1], wg16, wu16, wd16)
    return out.reshape(batch, seq, d)
```

```python
import functools
import math

import jax
import jax.numpy as jnp
import numpy as np
from jax import lax
from jax.experimental import pallas as pl
from jax.experimental.pallas import tpu as pltpu

F32 = jnp.float32
BF16 = jnp.bfloat16

D_MODEL = 2048
MLA_HEADS = 8
MLA_NOPE = 128
MLA_ROPE = 64
MLA_QK = MLA_NOPE + MLA_ROPE
MLA_V = 128
Q_LORA = 512
KV_LORA = 512
DIFF_HEADS = 8
DIFF_QK = 64
DIFF_ROT = 16
DIFF_V = 128
ROPE_THETA = 500000.0
D_FF = 5632
EPS = 1e-6
LAMBDA_INIT = 0.8 - 0.6 * math.exp(-0.3 * 0)
LOG2E = math.log2(math.e)

SMALL_SCORE_BOUND = -1.0
LANE = 128
HEAD_W = 2 * LANE
VMEM_LIMIT = 56 * 1024 * 1024

O_CKV = Q_LORA
O_KPE = O_CKV + KV_LORA
O_DQ = O_KPE + MLA_ROPE
O_DK = O_DQ + DIFF_HEADS * 2 * DIFF_QK
O_DV = O_DK + DIFF_HEADS * 2 * DIFF_QK
IN_COLS = O_DV + DIFF_HEADS * DIFF_V

_NT = (((1,), (1,)), ((), ()))


def _params(sem):
    return pltpu.CompilerParams(dimension_semantics=sem, vmem_limit_bytes=VMEM_LIMIT)


def _resident(shape):
    return pl.BlockSpec(shape, lambda *_: (0,) * len(shape), pipeline_mode=pl.Buffered(1))


def _inv_rms(xf, n):
    return lax.rsqrt(jnp.sum(xf * xf, axis=-1, keepdims=True) * (1.0 / n) + EPS)


def _dot_nt(a, b):
    return lax.dot_general(a, b, _NT, preferred_element_type=F32)


def _qkv_kernel(x_ref, ga_ref, wt_ref, wkpe_ref, glq_ref, glkv_ref, wq_ref, wkv_ref,
                lg_ref,
                cm_ref, sm_ref, cd_ref, sd_ref,
                wo_ref, wg_ref, wu_ref, wd_ref,
                qm_ref, km_ref, vm_ref, qd_ref, kd_ref, vd_ref,
                wo16_ref, wg16_ref, wu16_ref, wd16_ref):
    def cast_slab(src, dst):
        dst[...] = src[...].astype(BF16)

    tm = x_ref.shape[0]
    cm, sm, cd, sd = cm_ref[...], sm_ref[...], cd_ref[...], sd_ref[...]
    x = x_ref[...]
    h = ((x * _inv_rms(x, D_MODEL)) * ga_ref[...]).astype(BF16)

    lane = lax.broadcasted_iota(jnp.int32, (tm, LANE), 1)
    lo = lane < DIFF_QK
    is_x1 = (lane % DIFF_QK) < (DIFF_ROT // 2)

    def prep_diff(xh, g, scale):
        sq = xh * xh
        s_lo = jnp.sum(jnp.where(lo, sq, 0.0), axis=-1, keepdims=True)
        s_hi = jnp.sum(jnp.where(lo, 0.0, sq), axis=-1, keepdims=True)
        inv = lax.rsqrt(jnp.where(lo, s_lo, s_hi) * (1.0 / DIFF_QK) + EPS)
        y = (xh * inv) * g
        partner = jnp.where(is_x1, pltpu.roll(y, LANE - DIFF_ROT // 2, 1), pltpu.roll(y, DIFF_ROT // 2, 1))
        out = y * cd + partner * sd
        return out * scale if scale != 1.0 else out

    def rope_mla(y):
        return y * cm + pltpu.roll(y, MLA_ROPE, 1) * sm

    half = DIFF_HEADS // 2 * LANE
    lg = lg_ref[...]
    gqn, gqr, gkn, gkr, gdq, gdk = (lg[r:r + 1, :] for r in range(6))
    diff_scale = LOG2E / math.sqrt(DIFF_QK)

    def diff_heads(src, g, scale, dst_ref, c0):
        for j in range(DIFF_HEADS // 2):
            dst_ref[:, c0 + j * LANE:c0 + (j + 1) * LANE] = prep_diff(
                src[:, j * LANE:(j + 1) * LANE], g, scale).astype(BF16)

    mla_scale = LOG2E / math.sqrt(MLA_QK)
    nope_w = MLA_HEADS * LANE

    def q_heads(q):
        for hd in range(MLA_HEADS):
            qn = q[:, hd * LANE:(hd + 1) * LANE]
            qr = q[:, nope_w + hd * LANE:nope_w + (hd + 1) * LANE]
            ss = jnp.sum(qn * qn + 0.5 * (qr * qr), axis=-1, keepdims=True)
            sq = lax.rsqrt(ss * (1.0 / MLA_QK) + EPS)
            qm_ref[:, hd * HEAD_W:hd * HEAD_W + LANE] = (((qn * sq) * gqn) * mla_scale).astype(BF16)
            qm_ref[:, hd * HEAD_W + LANE:(hd + 1) * HEAD_W] = (
                rope_mla((qr * sq) * gqr) * mla_scale).astype(BF16)

    def k_heads(kv, kpe):
        kpe_sq = 0.5 * (kpe * kpe)
        kpe_rot = rope_mla(kpe * gkr)
        for hd in range(MLA_HEADS):
            kn = kv[:, hd * LANE:(hd + 1) * LANE]
            ssk = jnp.sum(kn * kn + kpe_sq, axis=-1, keepdims=True)
            sk = lax.rsqrt(ssk * (1.0 / MLA_QK) + EPS)
            km_ref[:, hd * HEAD_W:hd * HEAD_W + LANE] = ((kn * sk) * gkn).astype(BF16)
            km_ref[:, hd * HEAD_W + LANE:(hd + 1) * HEAD_W] = (kpe_rot * sk).astype(BF16)
            vm_ref[:, hd * HEAD_W:hd * HEAD_W + LANE] = (
                kv[:, nope_w + hd * LANE:nope_w + (hd + 1) * LANE].astype(BF16))
            vm_ref[:, hd * HEAD_W + LANE:(hd + 1) * HEAD_W] = jnp.ones((tm, LANE), BF16)

    cq = _dot_nt(h, wt_ref[0:O_CKV, :])
    ckv = _dot_nt(h, wt_ref[O_CKV:O_KPE, :])
    kpe = _dot_nt(h, wkpe_ref[...])
    cast_slab(wg_ref, wg16_ref)
    hq = ((cq * _inv_rms(cq, Q_LORA)) * glq_ref[...]).astype(BF16)
    q = jnp.dot(hq, wq_ref[...], preferred_element_type=F32)
    dq0 = _dot_nt(h, wt_ref[O_DQ:O_DQ + half, :])
    hkv = ((ckv * _inv_rms(ckv, KV_LORA)) * glkv_ref[...]).astype(BF16)
    kv = jnp.dot(hkv, wkv_ref[...], preferred_element_type=F32)
    dq1 = _dot_nt(h, wt_ref[O_DQ + half:O_DK, :])
    cast_slab(wu_ref, wu16_ref)
    q_heads(q)
    dk0 = _dot_nt(h, wt_ref[O_DK:O_DK + half, :])
    k_heads(kv, kpe)
    dk1 = _dot_nt(h, wt_ref[O_DK + half:O_DV, :])
    cast_slab(wd_ref, wd16_ref)
    diff_heads(dq0, gdq, diff_scale, qd_ref, 0)
    vd_ref[:, 0:half] = _dot_nt(h, wt_ref[O_DV:O_DV + half, :]).astype(BF16)
    diff_heads(dq1, gdq, diff_scale, qd_ref, half)
    cast_slab(wo_ref, wo16_ref)
    diff_heads(dk0, gdk, 1.0, kd_ref, 0)
    vd_ref[:, half:] = _dot_nt(h, wt_ref[O_DV + half:IN_COLS, :]).astype(BF16)
    diff_heads(dk1, gdk, 1.0, kd_ref, half)


def _qkv(x2d, ga, wt, wkpe, glq, glkv, wq, wkv, lane_gains, cm, sm, cd, sd,
         later_weights, *, seq, tm=256):
    t = x2d.shape[0]
    steps = t // tm
    spt = seq // tm
    tab = pl.BlockSpec((tm, LANE), lambda i: (i % spt, 0))
    vec = lambda n: _resident((1, n))
    widths = [MLA_HEADS * HEAD_W, MLA_HEADS * HEAD_W, MLA_HEADS * HEAD_W,
              DIFF_HEADS * LANE, DIFF_HEADS * LANE, DIFF_HEADS * DIFF_V]
    slabs = [pl.BlockSpec((w.shape[0] // steps, w.shape[1]), lambda i: (i, 0)) for w in later_weights]
    assert all(w.shape[0] % (16 * steps) == 0 for w in later_weights)
    outs = pl.pallas_call(
        _qkv_kernel,
        name="qkv",
        grid=(steps,),
        in_specs=[pl.BlockSpec((tm, D_MODEL), lambda i: (i, 0)), vec(D_MODEL),
                  _resident(wt.shape), _resident(wkpe.shape),
                  vec(Q_LORA), vec(KV_LORA), _resident(wq.shape), _resident(wkv.shape),
                  _resident(lane_gains.shape),
                  tab, tab, tab, tab] + slabs,
        out_specs=[pl.BlockSpec((tm, w), lambda i: (i, 0)) for w in widths] + slabs,
        out_shape=([jax.ShapeDtypeStruct((t, w), BF16) for w in widths]
                   + [jax.ShapeDtypeStruct(w.shape, BF16) for w in later_weights]),
        compiler_params=_params(("arbitrary",)),
    )(x2d, ga, wt, wkpe, glq, glkv, wq, wkv, lane_gains, cm, sm, cd, sd, *later_weights)
    return outs[:len(widths)], outs[len(widths):]


def _causal_keep(tq):
    row = lax.broadcasted_iota(jnp.int32, (tq, tq), 0)
    col = lax.broadcasted_iota(jnp.int32, (tq, tq), 1)
    return col <= row


def _scores(q, k_rows, r0, tq, keep):
    s_d = _dot_nt(q, k_rows(r0, r0 + tq))
    s_d = jnp.where(keep, s_d, -jnp.inf)
    s_o = None
    if r0 > 0:
        s_o = _dot_nt(q, k_rows(0, r0))
    return s_o, s_d


def _softmax_pieces(s, *, with_sum, subtract_max):
    s_o, s_d = s
    if subtract_max:
        m = jnp.max(s_d, axis=-1, keepdims=True)
        if s_o is not None:
            m = jnp.maximum(m, jnp.max(s_o, axis=-1, keepdims=True))
            s_o = s_o - m
        s_d = s_d - m
    p_d = jnp.exp2(s_d)
    p_o = None if s_o is None else jnp.exp2(s_o)
    if not with_sum:
        return p_o, p_d, None
    l = jnp.sum(p_d, axis=-1, keepdims=True)
    if p_o is not None:
        l = jnp.sum(p_o, axis=-1, keepdims=True) + l
    return p_o, p_d, l


def _pv(p_o, p_d, v_rows, r0, tq):
    o = jnp.dot(p_d.astype(BF16), v_rows(r0, r0 + tq), preferred_element_type=F32)
    if p_o is not None:
        o = o + jnp.dot(p_o.astype(BF16), v_rows(0, r0), preferred_element_type=F32)
    return o


def _work_items(seq, tq, heads, descending):
    starts = list(range(0, seq, tq))
    if descending:
        starts = starts[::-1]
    return [(j, r0) for j in range(heads) for r0 in starts]


def _cols(ref, j, width):
    return lambda a, b: ref[a:b, j * width:(j + 1) * width]


def _pipelined(items, scores, finish):
    s_next = scores(items[0])
    for k, item in enumerate(items):
        s = s_next
        if k + 1 < len(items):
            s_next = scores(items[k + 1])
        finish(item, s)


def _either_softmax(bound, body):
    small = bound <= SMALL_SCORE_BOUND
    pl.when(small)(functools.partial(body, subtract_max=False))
    pl.when(jnp.logical_not(small))(functools.partial(body, subtract_max=True))


def _mla_attn_kernel(bound_ref, q_ref, k_ref, v_ref, g_ref, o_ref, *, tq, heads):
    seq = q_ref.shape[0]

    def body(subtract_max):
        keep = _causal_keep(tq)
        g = g_ref[...]

        def scores(item):
            j, r0 = item
            return _scores(_cols(q_ref, j, HEAD_W)(r0, r0 + tq), _cols(k_ref, j, HEAD_W), r0, tq, keep)

        def finish(item, s):
            j, r0 = item
            p_o, p_d, _ = _softmax_pieces(s, with_sum=False, subtract_max=subtract_max)
            ol = _pv(p_o, p_d, _cols(v_ref, j, HEAD_W), r0, tq)
            o = ol[:, :MLA_V] / ol[:, MLA_V:]
            o_ref[r0:r0 + tq, j * MLA_V:(j + 1) * MLA_V] = ((o * _inv_rms(o, MLA_V)) * g).astype(BF16)

        _pipelined(_work_items(seq, tq, heads, descending=True), scores, finish)

    _either_softmax(bound_ref[0, 0], body)


def _diff_attn_kernel(bound_ref, q_ref, k_ref, v_ref, g_ref, lam_ref, o_ref, *, tq, heads):
    _either_softmax(bound_ref[0, 1], functools.partial(_diff_attn_body, q_ref, k_ref, v_ref, g_ref, lam_ref, o_ref,
                                                 tq=tq, heads=heads))


def _diff_attn_body(q_ref, k_ref, v_ref, g_ref, lam_ref, o_ref, *, tq, heads, subtract_max):
    seq = q_ref.shape[0]
    keep = _causal_keep(tq)
    g = g_ref[...]
    lp = lam_ref[...]
    lam = (jnp.exp(jnp.sum(lp[0:1, :] * lp[1:2, :], axis=-1, keepdims=True))
           - jnp.exp(jnp.sum(lp[2:3, :] * lp[3:4, :], axis=-1, keepdims=True))
           + LAMBDA_INIT)
    lane = lax.broadcasted_iota(jnp.int32, (tq, LANE), 1)
    first = lane < DIFF_QK

    def scores(item):
        j, r0 = item
        q = _cols(q_ref, j, LANE)(r0, r0 + tq)
        k_rows = _cols(k_ref, j, LANE)
        q1 = jnp.where(first, q, jnp.zeros_like(q))
        q2 = jnp.where(first, jnp.zeros_like(q), q)
        return _scores(q1, k_rows, r0, tq, keep), _scores(q2, k_rows, r0, tq, keep)

    def finish(item, s):
        j, r0 = item
        p1_o, p1_d, l1 = _softmax_pieces(s[0], with_sum=True, subtract_max=subtract_max)
        p2_o, p2_d, l2 = _softmax_pieces(s[1], with_sum=True, subtract_max=subtract_max)
        c = lam * (l1 / l2)
        a_o = None if p1_o is None else p1_o - c * p2_o
        o = _pv(a_o, p1_d - c * p2_d, _cols(v_ref, j, DIFF_V), r0, tq) / l1
        o_ref[r0:r0 + tq, j * DIFF_V:(j + 1) * DIFF_V] = (
            ((o * _inv_rms(o, DIFF_V)) * g) * (1.0 - LAMBDA_INIT)).astype(BF16)

    _pipelined(_work_items(seq, tq, heads, descending=not subtract_max), scores, finish)


def _head_spec(seq, width, heads):
    return pl.BlockSpec((seq, heads * width), lambda b, h: (b, h))


_SCALAR_SPEC = pl.BlockSpec(memory_space=pltpu.SMEM)


def _score_bound(gq, gk, d):
    return 1.01 * LOG2E * math.sqrt(d) * jnp.max(jnp.abs(gq)) * jnp.max(jnp.abs(gk))


def _mla_attn(bound, qm, km, vm, g, *, batch, seq, tq=256, heads=2):
    t = qm.shape[0]
    return pl.pallas_call(
        functools.partial(_mla_attn_kernel, tq=tq, heads=heads),
        name="mla_attn",
        grid=(batch, MLA_HEADS // heads),
        in_specs=[_SCALAR_SPEC, _head_spec(seq, HEAD_W, heads), _head_spec(seq, HEAD_W, heads), _head_spec(seq, HEAD_W, heads),
                  pl.BlockSpec((1, MLA_V), lambda b, h: (0, 0))],
        out_specs=_head_spec(seq, MLA_V, heads),
        out_shape=jax.ShapeDtypeStruct((t, MLA_HEADS * MLA_V), BF16),
        compiler_params=_params(("arbitrary", "arbitrary")),
    )(bound, qm, km, vm, g)


def _diff_attn(bound, qd, kd, vd, g, lam_params, *, batch, seq, tq=256, heads=2):
    t = qd.shape[0]
    return pl.pallas_call(
        functools.partial(_diff_attn_kernel, tq=tq, heads=heads),
        name="diff_attn",
        grid=(batch, DIFF_HEADS // heads),
        in_specs=[_SCALAR_SPEC, _head_spec(seq, LANE, heads), _head_spec(seq, LANE, heads), _head_spec(seq, DIFF_V, heads),
                  pl.BlockSpec((1, DIFF_V), lambda b, h: (0, 0)),
                  pl.BlockSpec((4, DIFF_QK), lambda b, h: (0, 0))],
        out_specs=_head_spec(seq, DIFF_V, heads),
        out_shape=jax.ShapeDtypeStruct((t, DIFF_HEADS * DIFF_V), BF16),
        compiler_params=_params(("arbitrary", "arbitrary")),
    )(bound, qd, kd, vd, g, lam_params)


def _out_proj_kernel(x_ref, om_ref, od_ref, w_ref, o_ref):
    km = om_ref.shape[1]
    acc = jnp.dot(om_ref[...], w_ref[0:km, :], preferred_element_type=F32)
    acc = acc + jnp.dot(od_ref[...], w_ref[km:, :], preferred_element_type=F32)
    o_ref[...] = x_ref[...] + acc


def _out_proj(x2d, om, od, w, *, tm=512):
    t = x2d.shape[0]
    return pl.pallas_call(
        _out_proj_kernel,
        name="out_proj",
        grid=(t // tm,),
        in_specs=[pl.BlockSpec((tm, D_MODEL), lambda i: (i, 0)),
                  pl.BlockSpec((tm, om.shape[1]), lambda i: (i, 0)),
                  pl.BlockSpec((tm, od.shape[1]), lambda i: (i, 0)),
                  _resident(w.shape)],
        out_specs=pl.BlockSpec((tm, D_MODEL), lambda i: (i, 0)),
        out_shape=jax.ShapeDtypeStruct((t, D_MODEL), F32),
        compiler_params=_params(("arbitrary",)),
    )(x2d, om, od, w)


def _ffn_kernel(x_ref, g_ref, wg_ref, wu_ref, wd_ref, o_ref, h_ref, *, out_chunk):
    @pl.when(pl.program_id(1) == 0)
    def _():
        x = x_ref[...]
        h_ref[...] = ((x * _inv_rms(x, D_MODEL)) * g_ref[...]).astype(BF16)
        o_ref[...] = x

    h = h_ref[...]
    gate = jnp.dot(h, wg_ref[...], preferred_element_type=F32)
    up = jnp.dot(h, wu_ref[...], preferred_element_type=F32)
    act = (gate * jax.nn.sigmoid(gate) * up).astype(BF16)
    for c0 in range(0, o_ref.shape[1], out_chunk):
        o_ref[:, c0:c0 + out_chunk] += jnp.dot(act, wd_ref[:, c0:c0 + out_chunk],
                                               preferred_element_type=F32)


def _ffn(x1, g, wg, wu, wd, *, tm=1024, tf=512):
    t = x1.shape[0]
    return pl.pallas_call(
        functools.partial(_ffn_kernel, out_chunk=512),
        name="ffn",
        grid=(t // tm, D_FF // tf),
        in_specs=[pl.BlockSpec((tm, D_MODEL), lambda i, f: (i, 0)),
                  pl.BlockSpec((1, D_MODEL), lambda i, f: (0, 0)),
                  pl.BlockSpec((D_MODEL, tf), lambda i, f: (0, f)),
                  pl.BlockSpec((D_MODEL, tf), lambda i, f: (0, f)),
                  pl.BlockSpec((tf, D_MODEL), lambda i, f: (f, 0))],
        out_specs=pl.BlockSpec((tm, D_MODEL), lambda i, f: (i, 0)),
        out_shape=jax.ShapeDtypeStruct((t, D_MODEL), F32),
        scratch_shapes=[pltpu.VMEM((tm, D_MODEL), BF16)],
        compiler_params=_params(("arbitrary", "arbitrary")),
    )(x1, g, wg, wu, wd)


def _rope_tables(seq):
    pos = np.arange(seq, dtype=np.float64)

    def angles(r):
        freqs = 1.0 / (ROPE_THETA ** (np.arange(0, r, 2, dtype=np.float64) / r))
        return pos[:, None] * freqs[None, :]

    am = angles(MLA_ROPE)
    cos, sin = np.cos(am), np.sin(am)
    zeros = np.zeros((seq, LANE - MLA_ROPE))
    cm = np.concatenate([cos, cos, zeros], axis=-1)
    sm = np.concatenate([-sin, sin, zeros], axis=-1)

    ad = angles(DIFF_ROT)
    cos, sin = np.cos(ad), np.sin(ad)
    rest = DIFF_QK - DIFF_ROT
    cd = np.tile(np.concatenate([cos, cos, np.ones((seq, rest))], axis=-1), (1, 2))
    sd = np.tile(np.concatenate([-sin, sin, np.zeros((seq, rest))], axis=-1), (1, 2))
    return tuple(jnp.asarray(t.astype(np.float32)) for t in (cm, sm, cd, sd))


def _with_partner(v, axis=-1):
    return jnp.concatenate([v, jnp.roll(v, MLA_ROPE // 2, axis=axis)], axis=axis)


def kernel(x, attn_norm, w_in, q_latent_norm, w_q_up, kv_latent_norm, w_kv_up, mla_q_norm, mla_k_norm, mla_out_norm, diff_q_norm, diff_k_norm, lambda_q1, lambda_k1, lambda_q2, lambda_k2, diff_out_norm, w_o, ffn_norm, w_gate, w_up, w_down):
    batch, seq, d = x.shape
    assert d == D_MODEL and attn_norm.shape[0] == 1
    t = batch * seq
    x2d = x.reshape(t, d)
    l = 0

    wt = jnp.swapaxes(w_in[l], 0, 1).astype(BF16)
    wkpe = _with_partner(wt[O_KPE:O_DQ], axis=0)
    wq = w_q_up[l].astype(BF16).reshape(Q_LORA, MLA_HEADS, MLA_QK)
    wq = jnp.concatenate([wq[:, :, :MLA_NOPE].reshape(Q_LORA, -1),
                          _with_partner(wq[:, :, MLA_NOPE:]).reshape(Q_LORA, -1)], axis=1)
    wkv = w_kv_up[l].astype(BF16).reshape(KV_LORA, MLA_HEADS, MLA_NOPE + MLA_V)
    wkv = jnp.concatenate([wkv[:, :, :MLA_NOPE].reshape(KV_LORA, -1),
                           wkv[:, :, MLA_NOPE:].reshape(KV_LORA, -1)], axis=1)

    gq, gk = mla_q_norm[l], mla_k_norm[l]
    lane_gains = jnp.stack([gq[:MLA_NOPE], _with_partner(gq[MLA_NOPE:]), gk[:MLA_NOPE], _with_partner(gk[MLA_NOPE:]),
                            jnp.tile(diff_q_norm[l], 2), jnp.tile(diff_k_norm[l], 2),
                            jnp.zeros((LANE,), F32), jnp.zeros((LANE,), F32)])
    bounds = jnp.stack([_score_bound(gq, gk, MLA_QK), _score_bound(diff_q_norm[l], diff_k_norm[l], DIFF_QK)])
    bounds = bounds.astype(F32).reshape(1, 2)
    lam_params = jnp.concatenate([lambda_q1[l:l + 1], lambda_k1[l:l + 1],
                                  lambda_q2[l:l + 1], lambda_k2[l:l + 1]], axis=0)
    cm, sm, cd, sd = _rope_tables(seq)

    (qm, km, vm, qd, kd, vd), (wo16, wg16, wu16, wd16) = _qkv(
        x2d, attn_norm[l:l + 1], wt, wkpe, q_latent_norm[l:l + 1], kv_latent_norm[l:l + 1], wq, wkv,
        lane_gains, cm, sm, cd, sd, (w_o[l], w_gate[l], w_up[l], w_down[l]), seq=seq)
    o_mla = _mla_attn(bounds, qm, km, vm, mla_out_norm[l:l + 1], batch=batch, seq=seq)
    o_diff = _diff_attn(bounds, qd, kd, vd, diff_out_norm[l:l + 1], lam_params, batch=batch, seq=seq)
    x1 = _out_proj(x2d, o_mla, o_diff, wo16)
    out = _ffn(x1, ffn_norm[l:l + 1], wg16, wu16, wd16)
    return out.reshape(batch, seq, d)
```

```python
import functools
import math

import jax
import jax.numpy as jnp
import numpy as np
from jax import lax
from jax.experimental import pallas as pl
from jax.experimental.pallas import tpu as pltpu

F32 = jnp.float32
BF16 = jnp.bfloat16

D_MODEL = 2048
MLA_HEADS = 8
MLA_NOPE = 128
MLA_ROPE = 64
MLA_QK = MLA_NOPE + MLA_ROPE
MLA_V = 128
Q_LORA = 512
KV_LORA = 512
DIFF_HEADS = 8
DIFF_QK = 64
DIFF_ROT = 16
DIFF_V = 128
ROPE_THETA = 500000.0
D_FF = 5632
EPS = 1e-6
LAMBDA_INIT = 0.8 - 0.6 * math.exp(-0.3 * 0)
LOG2E = math.log2(math.e)

SMALL_SCORE_BOUND = 32.0
LANE = 128
HEAD_W = 2 * LANE
VMEM_LIMIT = 56 * 1024 * 1024

O_CKV = Q_LORA
O_KPE = O_CKV + KV_LORA
O_DQ = O_KPE + MLA_ROPE
O_DK = O_DQ + DIFF_HEADS * 2 * DIFF_QK
O_DV = O_DK + DIFF_HEADS * 2 * DIFF_QK
IN_COLS = O_DV + DIFF_HEADS * DIFF_V

_NT = (((1,), (1,)), ((), ()))


def _params(sem):
    return pltpu.CompilerParams(dimension_semantics=sem, vmem_limit_bytes=VMEM_LIMIT)


def _resident(shape):
    return pl.BlockSpec(shape, lambda *_: (0,) * len(shape), pipeline_mode=pl.Buffered(1))


def _inv_rms(xf, n):
    return lax.rsqrt(jnp.sum(xf * xf, axis=-1, keepdims=True) * (1.0 / n) + EPS)


def _dot_nt(a, b):
    return lax.dot_general(a, b, _NT, preferred_element_type=F32)


def _qkv_kernel(x_ref, ga_ref, wt_ref, wkpe_ref, glq_ref, glkv_ref, wq_ref, wkv_ref,
                lg_ref,
                cm_ref, sm_ref, cd_ref, sd_ref,
                wo_ref, wg_ref, wu_ref, wd_ref,
                qm_ref, km_ref, vm_ref, qd_ref, kd_ref, vd_ref,
                wo16_ref, wg16_ref, wu16_ref, wd16_ref):
    def cast_slab(src, dst):
        dst[...] = src[...].astype(BF16)

    tm = x_ref.shape[0]
    cm, sm, cd, sd = cm_ref[...], sm_ref[...], cd_ref[...], sd_ref[...]
    x = x_ref[...]
    h = ((x * _inv_rms(x, D_MODEL)) * ga_ref[...]).astype(BF16)

    lane = lax.broadcasted_iota(jnp.int32, (tm, LANE), 1)
    lo = lane < DIFF_QK
    is_x1 = (lane % DIFF_QK) < (DIFF_ROT // 2)

    def prep_diff(xh, g, scale):
        sq = xh * xh
        s_lo = jnp.sum(jnp.where(lo, sq, 0.0), axis=-1, keepdims=True)
        s_hi = jnp.sum(jnp.where(lo, 0.0, sq), axis=-1, keepdims=True)
        inv = lax.rsqrt(jnp.where(lo, s_lo, s_hi) * (1.0 / DIFF_QK) + EPS)
        y = (xh * inv) * g
        partner = jnp.where(is_x1, pltpu.roll(y, LANE - DIFF_ROT // 2, 1), pltpu.roll(y, DIFF_ROT // 2, 1))
        out = y * cd + partner * sd
        return out * scale if scale != 1.0 else out

    def rope_mla(y):
        return y * cm + pltpu.roll(y, MLA_ROPE, 1) * sm

    half = DIFF_HEADS // 2 * LANE
    lg = lg_ref[...]
    gqn, gqr, gkn, gkr, gdq, gdk = (lg[r:r + 1, :] for r in range(6))
    diff_scale = LOG2E / math.sqrt(DIFF_QK)

    def diff_heads(src, g, scale, dst_ref, c0):
        for j in range(DIFF_HEADS // 2):
            dst_ref[:, c0 + j * LANE:c0 + (j + 1) * LANE] = prep_diff(
                src[:, j * LANE:(j + 1) * LANE], g, scale).astype(BF16)

    mla_scale = LOG2E / math.sqrt(MLA_QK)
    nope_w = MLA_HEADS * LANE

    def q_heads(q):
        for hd in range(MLA_HEADS):
            qn = q[:, hd * LANE:(hd + 1) * LANE]
            qr = q[:, nope_w + hd * LANE:nope_w + (hd + 1) * LANE]
            ss = jnp.sum(qn * qn + 0.5 * (qr * qr), axis=-1, keepdims=True)
            sq = lax.rsqrt(ss * (1.0 / MLA_QK) + EPS)
            qm_ref[:, hd * HEAD_W:hd * HEAD_W + LANE] = (((qn * sq) * gqn) * mla_scale).astype(BF16)
            qm_ref[:, hd * HEAD_W + LANE:(hd + 1) * HEAD_W] = (
                rope_mla((qr * sq) * gqr) * mla_scale).astype(BF16)

    def k_heads(kv, kpe):
        kpe_sq = 0.5 * (kpe * kpe)
        kpe_rot = rope_mla(kpe * gkr)
        for hd in range(MLA_HEADS):
            kn = kv[:, hd * LANE:(hd + 1) * LANE]
            ssk = jnp.sum(kn * kn + kpe_sq, axis=-1, keepdims=True)
            sk = lax.rsqrt(ssk * (1.0 / MLA_QK) + EPS)
            km_ref[:, hd * HEAD_W:hd * HEAD_W + LANE] = ((kn * sk) * gkn).astype(BF16)
            km_ref[:, hd * HEAD_W + LANE:(hd + 1) * HEAD_W] = (kpe_rot * sk).astype(BF16)
            vm_ref[:, hd * HEAD_W:hd * HEAD_W + LANE] = (
                kv[:, nope_w + hd * LANE:nope_w + (hd + 1) * LANE].astype(BF16))
            vm_ref[:, hd * HEAD_W + LANE:(hd + 1) * HEAD_W] = jnp.ones((tm, LANE), BF16)

    cq = _dot_nt(h, wt_ref[0:O_CKV, :])
    ckv = _dot_nt(h, wt_ref[O_CKV:O_KPE, :])
    kpe = _dot_nt(h, wkpe_ref[...])
    cast_slab(wg_ref, wg16_ref)
    hq = ((cq * _inv_rms(cq, Q_LORA)) * glq_ref[...]).astype(BF16)
    q = jnp.dot(hq, wq_ref[...], preferred_element_type=F32)
    dq0 = _dot_nt(h, wt_ref[O_DQ:O_DQ + half, :])
    hkv = ((ckv * _inv_rms(ckv, KV_LORA)) * glkv_ref[...]).astype(BF16)
    kv = jnp.dot(hkv, wkv_ref[...], preferred_element_type=F32)
    dq1 = _dot_nt(h, wt_ref[O_DQ + half:O_DK, :])
    cast_slab(wu_ref, wu16_ref)
    q_heads(q)
    dk0 = _dot_nt(h, wt_ref[O_DK:O_DK + half, :])
    k_heads(kv, kpe)
    dk1 = _dot_nt(h, wt_ref[O_DK + half:O_DV, :])
    cast_slab(wd_ref, wd16_ref)
    diff_heads(dq0, gdq, diff_scale, qd_ref, 0)
    vd_ref[:, 0:half] = _dot_nt(h, wt_ref[O_DV:O_DV + half, :]).astype(BF16)
    diff_heads(dq1, gdq, diff_scale, qd_ref, half)
    cast_slab(wo_ref, wo16_ref)
    diff_heads(dk0, gdk, 1.0, kd_ref, 0)
    vd_ref[:, half:] = _dot_nt(h, wt_ref[O_DV + half:IN_COLS, :]).astype(BF16)
    diff_heads(dk1, gdk, 1.0, kd_ref, half)


def _qkv(x2d, ga, wt, wkpe, glq, glkv, wq, wkv, lane_gains, cm, sm, cd, sd,
         later_weights, *, seq, tm=256):
    t = x2d.shape[0]
    steps = t // tm
    spt = seq // tm
    tab = pl.BlockSpec((tm, LANE), lambda i: (i % spt, 0))
    vec = lambda n: _resident((1, n))
    widths = [MLA_HEADS * HEAD_W, MLA_HEADS * HEAD_W, MLA_HEADS * HEAD_W,
              DIFF_HEADS * LANE, DIFF_HEADS * LANE, DIFF_HEADS * DIFF_V]
    slabs = [pl.BlockSpec((w.shape[0] // steps, w.shape[1]), lambda i: (i, 0)) for w in later_weights]
    assert all(w.shape[0] % (16 * steps) == 0 for w in later_weights)
    outs = pl.pallas_call(
        _qkv_kernel,
        name="qkv",
        grid=(steps,),
        in_specs=[pl.BlockSpec((tm, D_MODEL), lambda i: (i, 0)), vec(D_MODEL),
                  _resident(wt.shape), _resident(wkpe.shape),
                  vec(Q_LORA), vec(KV_LORA), _resident(wq.shape), _resident(wkv.shape),
                  _resident(lane_gains.shape),
                  tab, tab, tab, tab] + slabs,
        out_specs=[pl.BlockSpec((tm, w), lambda i: (i, 0)) for w in widths] + slabs,
        out_shape=([jax.ShapeDtypeStruct((t, w), BF16) for w in widths]
                   + [jax.ShapeDtypeStruct(w.shape, BF16) for w in later_weights]),
        compiler_params=_params(("arbitrary",)),
    )(x2d, ga, wt, wkpe, glq, glkv, wq, wkv, lane_gains, cm, sm, cd, sd, *later_weights)
    return outs[:len(widths)], outs[len(widths):]


def _causal_keep(tq):
    row = lax.broadcasted_iota(jnp.int32, (tq, tq), 0)
    col = lax.broadcasted_iota(jnp.int32, (tq, tq), 1)
    return col <= row


def _scores(q, k_rows, r0, tq, keep):
    s_d = _dot_nt(q, k_rows(r0, r0 + tq))
    s_d = jnp.where(keep, s_d, -jnp.inf)
    s_o = None
    if r0 > 0:
        s_o = _dot_nt(q, k_rows(0, r0))
    return s_o, s_d


def _softmax_pieces(s, *, with_sum, subtract_max):
    s_o, s_d = s
    if subtract_max:
        m = jnp.max(s_d, axis=-1, keepdims=True)
        if s_o is not None:
            m = jnp.maximum(m, jnp.max(s_o, axis=-1, keepdims=True))
            s_o = s_o - m
        s_d = s_d - m
    p_d = jnp.exp2(s_d)
    p_o = None if s_o is None else jnp.exp2(s_o)
    if not with_sum:
        return p_o, p_d, None
    l = jnp.sum(p_d, axis=-1, keepdims=True)
    if p_o is not None:
        l = jnp.sum(p_o, axis=-1, keepdims=True) + l
    return p_o, p_d, l


def _pv(p_o, p_d, v_rows, r0, tq):
    o = jnp.dot(p_d.astype(BF16), v_rows(r0, r0 + tq), preferred_element_type=F32)
    if p_o is not None:
        o = o + jnp.dot(p_o.astype(BF16), v_rows(0, r0), preferred_element_type=F32)
    return o


def _work_items(seq, tq, heads, descending):
    starts = list(range(0, seq, tq))
    if descending:
        starts = starts[::-1]
    return [(j, r0) for j in range(heads) for r0 in starts]


def _cols(ref, j, width):
    return lambda a, b: ref[a:b, j * width:(j + 1) * width]


def _pipelined(items, scores, finish):
    s_next = scores(items[0])
    for k, item in enumerate(items):
        s = s_next
        if k + 1 < len(items):
            s_next = scores(items[k + 1])
        finish(item, s)


def _either_softmax(bound, body):
    small = bound <= SMALL_SCORE_BOUND
    pl.when(small)(functools.partial(body, subtract_max=False))
    pl.when(jnp.logical_not(small))(functools.partial(body, subtract_max=True))


def _mla_attn_kernel(bound_ref, q_ref, k_ref, v_ref, g_ref, o_ref, *, tq, heads):
    seq = q_ref.shape[0]

    def body(subtract_max):
        keep = _causal_keep(tq)
        g = g_ref[...]

        def scores(item):
            j, r0 = item
            return _scores(_cols(q_ref, j, HEAD_W)(r0, r0 + tq), _cols(k_ref, j, HEAD_W), r0, tq, keep)

        def finish(item, s):
            j, r0 = item
            p_o, p_d, _ = _softmax_pieces(s, with_sum=False, subtract_max=subtract_max)
            ol = _pv(p_o, p_d, _cols(v_ref, j, HEAD_W), r0, tq)
            o = ol[:, :MLA_V] / ol[:, MLA_V:]
            o_ref[r0:r0 + tq, j * MLA_V:(j + 1) * MLA_V] = ((o * _inv_rms(o, MLA_V)) * g).astype(BF16)

        _pipelined(_work_items(seq, tq, heads, descending=True), scores, finish)

    _either_softmax(bound_ref[0, 0], body)


def _diff_attn_kernel(bound_ref, q_ref, k_ref, v_ref, g_ref, lam_ref, o_ref, *, tq, heads):
    _either_softmax(bound_ref[0, 1], functools.partial(_diff_attn_body, q_ref, k_ref, v_ref, g_ref, lam_ref, o_ref,
                                                 tq=tq, heads=heads))


def _diff_attn_body(q_ref, k_ref, v_ref, g_ref, lam_ref, o_ref, *, tq, heads, subtract_max):
    seq = q_ref.shape[0]
    keep = _causal_keep(tq)
    g = g_ref[...]
    lp = lam_ref[...]
    lam = (jnp.exp(jnp.sum(lp[0:1, :] * lp[1:2, :], axis=-1, keepdims=True))
           - jnp.exp(jnp.sum(lp[2:3, :] * lp[3:4, :], axis=-1, keepdims=True))
           + LAMBDA_INIT)
    lane = lax.broadcasted_iota(jnp.int32, (tq, LANE), 1)
    first = lane < DIFF_QK

    def scores(item):
        j, r0 = item
        q = _cols(q_ref, j, LANE)(r0, r0 + tq)
        k_rows = _cols(k_ref, j, LANE)
        q1 = jnp.where(first, q, jnp.zeros_like(q))
        q2 = jnp.where(first, jnp.zeros_like(q), q)
        return _scores(q1, k_rows, r0, tq, keep), _scores(q2, k_rows, r0, tq, keep)

    def finish(item, s):
        j, r0 = item
        p1_o, p1_d, l1 = _softmax_pieces(s[0], with_sum=True, subtract_max=subtract_max)
        p2_o, p2_d, l2 = _softmax_pieces(s[1], with_sum=True, subtract_max=subtract_max)
        c = lam * (l1 / l2)
        a_o = None if p1_o is None else p1_o - c * p2_o
        o = _pv(a_o, p1_d - c * p2_d, _cols(v_ref, j, DIFF_V), r0, tq) / l1
        o_ref[r0:r0 + tq, j * DIFF_V:(j + 1) * DIFF_V] = (
            ((o * _inv_rms(o, DIFF_V)) * g) * (1.0 - LAMBDA_INIT)).astype(BF16)

    _pipelined(_work_items(seq, tq, heads, descending=not subtract_max), scores, finish)


def _head_spec(seq, width, heads):
    return pl.BlockSpec((seq, heads * width), lambda b, h: (b, h))


_SCALAR_SPEC = pl.BlockSpec(memory_space=pltpu.SMEM)


def _score_bound(gq, gk, d):
    return 1.01 * LOG2E * math.sqrt(d) * jnp.max(jnp.abs(gq)) * jnp.max(jnp.abs(gk))


def _mla_attn(bound, qm, km, vm, g, *, batch, seq, tq=256, heads=2):
    t = qm.shape[0]
    return pl.pallas_call(
        functools.partial(_mla_attn_kernel, tq=tq, heads=heads),
        name="mla_attn",
        grid=(batch, MLA_HEADS // heads),
        in_specs=[_SCALAR_SPEC, _head_spec(seq, HEAD_W, heads), _head_spec(seq, HEAD_W, heads), _head_spec(seq, HEAD_W, heads),
                  pl.BlockSpec((1, MLA_V), lambda b, h: (0, 0))],
        out_specs=_head_spec(seq, MLA_V, heads),
        out_shape=jax.ShapeDtypeStruct((t, MLA_HEADS * MLA_V), BF16),
        compiler_params=_params(("arbitrary", "arbitrary")),
    )(bound, qm, km, vm, g)


def _diff_attn(bound, qd, kd, vd, g, lam_params, *, batch, seq, tq=256, heads=2):
    t = qd.shape[0]
    return pl.pallas_call(
        functools.partial(_diff_attn_kernel, tq=tq, heads=heads),
        name="diff_attn",
        grid=(batch, DIFF_HEADS // heads),
        in_specs=[_SCALAR_SPEC, _head_spec(seq, LANE, heads), _head_spec(seq, LANE, heads), _head_spec(seq, DIFF_V, heads),
                  pl.BlockSpec((1, DIFF_V), lambda b, h: (0, 0)),
                  pl.BlockSpec((4, DIFF_QK), lambda b, h: (0, 0))],
        out_specs=_head_spec(seq, DIFF_V, heads),
        out_shape=jax.ShapeDtypeStruct((t, DIFF_HEADS * DIFF_V), BF16),
        compiler_params=_params(("arbitrary", "arbitrary")),
    )(bound, qd, kd, vd, g, lam_params)


def _out_proj_kernel(x_ref, om_ref, od_ref, w_ref, o_ref, *, col_chunk):
    km = om_ref.shape[1]
    om, od = om_ref[...], od_ref[...]
    for c0 in range(0, o_ref.shape[1], col_chunk):
        cs = slice(c0, c0 + col_chunk)
        acc = jnp.dot(om, w_ref[0:km, cs], preferred_element_type=F32)
        acc = acc + jnp.dot(od, w_ref[km:, cs], preferred_element_type=F32)
        o_ref[:, cs] = x_ref[:, cs] + acc


def _out_proj(x2d, om, od, w, *, tm=1024):
    t = x2d.shape[0]
    return pl.pallas_call(
        functools.partial(_out_proj_kernel, col_chunk=512),
        name="out_proj",
        grid=(t // tm,),
        in_specs=[pl.BlockSpec((tm, D_MODEL), lambda i: (i, 0)),
                  pl.BlockSpec((tm, om.shape[1]), lambda i: (i, 0)),
                  pl.BlockSpec((tm, od.shape[1]), lambda i: (i, 0)),
                  _resident(w.shape)],
        out_specs=pl.BlockSpec((tm, D_MODEL), lambda i: (i, 0)),
        out_shape=jax.ShapeDtypeStruct((t, D_MODEL), F32),
        compiler_params=_params(("arbitrary",)),
    )(x2d, om, od, w)


def _ffn_kernel(x_ref, g_ref, wg_ref, wu_ref, wd_ref, o_ref, h_ref, *, out_chunk):
    @pl.when(pl.program_id(1) == 0)
    def _():
        x = x_ref[...]
        h_ref[...] = ((x * _inv_rms(x, D_MODEL)) * g_ref[...]).astype(BF16)
        o_ref[...] = x

    h = h_ref[...]
    gate = jnp.dot(h, wg_ref[...], preferred_element_type=F32)
    up = jnp.dot(h, wu_ref[...], preferred_element_type=F32)
    act = (gate * jax.nn.sigmoid(gate) * up).astype(BF16)
    for c0 in range(0, o_ref.shape[1], out_chunk):
        o_ref[:, c0:c0 + out_chunk] += jnp.dot(act, wd_ref[:, c0:c0 + out_chunk],
                                               preferred_element_type=F32)


def _ffn(x1, g, wg, wu, wd, *, tm=1024, tf=512):
    t = x1.shape[0]
    return pl.pallas_call(
        functools.partial(_ffn_kernel, out_chunk=512),
        name="ffn",
        grid=(t // tm, D_FF // tf),
        in_specs=[pl.BlockSpec((tm, D_MODEL), lambda i, f: (i, 0)),
                  pl.BlockSpec((1, D_MODEL), lambda i, f: (0, 0)),
                  pl.BlockSpec((D_MODEL, tf), lambda i, f: (0, f)),
                  pl.BlockSpec((D_MODEL, tf), lambda i, f: (0, f)),
                  pl.BlockSpec((tf, D_MODEL), lambda i, f: (f, 0))],
        out_specs=pl.BlockSpec((tm, D_MODEL), lambda i, f: (i, 0)),
        out_shape=jax.ShapeDtypeStruct((t, D_MODEL), F32),
        scratch_shapes=[pltpu.VMEM((tm, D_MODEL), BF16)],
        compiler_params=_params(("arbitrary", "arbitrary")),
    )(x1, g, wg, wu, wd)


def _rope_tables(seq):
    pos = np.arange(seq, dtype=np.float64)

    def angles(r):
        freqs = 1.0 / (ROPE_THETA ** (np.arange(0, r, 2, dtype=np.float64) / r))
        return pos[:, None] * freqs[None, :]

    am = angles(MLA_ROPE)
    cos, sin = np.cos(am), np.sin(am)
    zeros = np.zeros((seq, LANE - MLA_ROPE))
    cm = np.concatenate([cos, cos, zeros], axis=-1)
    sm = np.concatenate([-sin, sin, zeros], axis=-1)

    ad = angles(DIFF_ROT)
    cos, sin = np.cos(ad), np.sin(ad)
    rest = DIFF_QK - DIFF_ROT
    cd = np.tile(np.concatenate([cos, cos, np.ones((seq, rest))], axis=-1), (1, 2))
    sd = np.tile(np.concatenate([-sin, sin, np.zeros((seq, rest))], axis=-1), (1, 2))
    return tuple(jnp.asarray(t.astype(np.float32)) for t in (cm, sm, cd, sd))


def _with_partner(v, axis=-1):
    return jnp.concatenate([v, jnp.roll(v, MLA_ROPE // 2, axis=axis)], axis=axis)


def kernel(x, attn_norm, w_in, q_latent_norm, w_q_up, kv_latent_norm, w_kv_up, mla_q_norm, mla_k_norm, mla_out_norm, diff_q_norm, diff_k_norm, lambda_q1, lambda_k1, lambda_q2, lambda_k2, diff_out_norm, w_o, ffn_norm, w_gate, w_up, w_down):
    batch, seq, d = x.shape
    assert d == D_MODEL and attn_norm.shape[0] == 1
    t = batch * seq
    x2d = x.reshape(t, d)
    l = 0

    wt = jnp.swapaxes(w_in[l], 0, 1).astype(BF16)
    wkpe = _with_partner(wt[O_KPE:O_DQ], axis=0)
    wq = w_q_up[l].astype(BF16).reshape(Q_LORA, MLA_HEADS, MLA_QK)
    wq = jnp.concatenate([wq[:, :, :MLA_NOPE].reshape(Q_LORA, -1),
                          _with_partner(wq[:, :, MLA_NOPE:]).reshape(Q_LORA, -1)], axis=1)
    wkv = w_kv_up[l].astype(BF16).reshape(KV_LORA, MLA_HEADS, MLA_NOPE + MLA_V)
    wkv = jnp.concatenate([wkv[:, :, :MLA_NOPE].reshape(KV_LORA, -1),
                           wkv[:, :, MLA_NOPE:].reshape(KV_LORA, -1)], axis=1)

    gq, gk = mla_q_norm[l], mla_k_norm[l]
    lane_gains = jnp.stack([gq[:MLA_NOPE], _with_partner(gq[MLA_NOPE:]), gk[:MLA_NOPE], _with_partner(gk[MLA_NOPE:]),
                            jnp.tile(diff_q_norm[l], 2), jnp.tile(diff_k_norm[l], 2),
                            jnp.zeros((LANE,), F32), jnp.zeros((LANE,), F32)])
    bounds = jnp.stack([_score_bound(gq, gk, MLA_QK), _score_bound(diff_q_norm[l], diff_k_norm[l], DIFF_QK)])
    bounds = bounds.astype(F32).reshape(1, 2)
    lam_params = jnp.concatenate([lambda_q1[l:l + 1], lambda_k1[l:l + 1],
                                  lambda_q2[l:l + 1], lambda_k2[l:l + 1]], axis=0)
    cm, sm, cd, sd = _rope_tables(seq)

    (qm, km, vm, qd, kd, vd), (wo16, wg16, wu16, wd16) = _qkv(
        x2d, attn_norm[l:l + 1], wt, wkpe, q_latent_norm[l:l + 1], kv_latent_norm[l:l + 1], wq, wkv,
        lane_gains, cm, sm, cd, sd, (w_o[l], w_gate[l], w_up[l], w_down[l]), seq=seq)
    o_mla = _mla_attn(bounds, qm, km, vm, mla_out_norm[l:l + 1], batch=batch, seq=seq)
    o_diff = _diff_attn(bounds, qd, kd, vd, diff_out_norm[l:l + 1], lam_params, batch=batch, seq=seq)
    x1 = _out_proj(x2d, o_mla, o_diff, wo16)
    out = _ffn(x1, ffn_norm[l:l + 1], wg16, wu16, wd16)
    return out.reshape(batch, seq, d)
```

```python
import functools
import math

import jax
import jax.numpy as jnp
import numpy as np
from jax import lax
from jax.experimental import pallas as pl
from jax.experimental.pallas import tpu as pltpu

F32 = jnp.float32
BF16 = jnp.bfloat16

D_MODEL = 2048
MLA_HEADS = 8
MLA_NOPE = 128
MLA_ROPE = 64
MLA_QK = MLA_NOPE + MLA_ROPE
MLA_V = 128
Q_LORA = 512
KV_LORA = 512
DIFF_HEADS = 8
DIFF_QK = 64
DIFF_ROT = 16
DIFF_V = 128
ROPE_THETA = 500000.0
D_FF = 5632
EPS = 1e-6
LAMBDA_INIT = 0.8 - 0.6 * math.exp(-0.3 * 0)
LOG2E = math.log2(math.e)

SMALL_SCORE_BOUND = 32.0
LANE = 128
HEAD_W = 2 * LANE
VMEM_LIMIT = 56 * 1024 * 1024

O_CKV = Q_LORA
O_KPE = O_CKV + KV_LORA
O_DQ = O_KPE + MLA_ROPE
O_DK = O_DQ + DIFF_HEADS * 2 * DIFF_QK
O_DV = O_DK + DIFF_HEADS * 2 * DIFF_QK
IN_COLS = O_DV + DIFF_HEADS * DIFF_V

_NT = (((1,), (1,)), ((), ()))


def _params(sem):
    return pltpu.CompilerParams(dimension_semantics=sem, vmem_limit_bytes=VMEM_LIMIT)


def _resident(shape):
    return pl.BlockSpec(shape, lambda *_: (0,) * len(shape), pipeline_mode=pl.Buffered(1))


def _inv_rms(xf, n):
    return lax.rsqrt(jnp.sum(xf * xf, axis=-1, keepdims=True) * (1.0 / n) + EPS)


def _dot_nt(a, b):
    return lax.dot_general(a, b, _NT, preferred_element_type=F32)


def _qkv_kernel(x_ref, ga_ref, wt_ref, wkpe_ref, glq_ref, glkv_ref, wq_ref, wkv_ref,
                lg_ref,
                cm_ref, sm_ref, cd_ref, sd_ref,
                wo_ref, wg_ref, wu_ref, wd_ref,
                qm_ref, km_ref, vm_ref, qd_ref, kd_ref, vd_ref,
                wo16_ref, wg16_ref, wu16_ref, wd16_ref):
    def cast_slab(src, dst):
        dst[...] = src[...].astype(BF16)

    tm = x_ref.shape[0]
    cm, sm, cd, sd = cm_ref[...], sm_ref[...], cd_ref[...], sd_ref[...]
    x = x_ref[...]
    h = ((x * _inv_rms(x, D_MODEL)) * ga_ref[...]).astype(BF16)

    lane = lax.broadcasted_iota(jnp.int32, (tm, LANE), 1)
    lo = lane < DIFF_QK
    is_x1 = (lane % DIFF_QK) < (DIFF_ROT // 2)

    def prep_diff(xh, g, scale):
        sq = xh * xh
        s_lo = jnp.sum(jnp.where(lo, sq, 0.0), axis=-1, keepdims=True)
        s_hi = jnp.sum(jnp.where(lo, 0.0, sq), axis=-1, keepdims=True)
        inv = lax.rsqrt(jnp.where(lo, s_lo, s_hi) * (1.0 / DIFF_QK) + EPS)
        y = (xh * inv) * g
        partner = jnp.where(is_x1, pltpu.roll(y, LANE - DIFF_ROT // 2, 1), pltpu.roll(y, DIFF_ROT // 2, 1))
        out = y * cd + partner * sd
        return out * scale if scale != 1.0 else out

    def rope_mla(y):
        return y * cm + pltpu.roll(y, MLA_ROPE, 1) * sm

    half = DIFF_HEADS // 2 * LANE
    lg = lg_ref[...]
    gqn, gqr, gkn, gkr, gdq, gdk = (lg[r:r + 1, :] for r in range(6))
    diff_scale = LOG2E / math.sqrt(DIFF_QK)

    def diff_heads(src, g, scale, dst_ref, c0):
        for j in range(DIFF_HEADS // 2):
            dst_ref[:, c0 + j * LANE:c0 + (j + 1) * LANE] = prep_diff(
                src[:, j * LANE:(j + 1) * LANE], g, scale).astype(BF16)

    mla_scale = LOG2E / math.sqrt(MLA_QK)
    nope_w = MLA_HEADS * LANE

    def q_heads(q):
        for hd in range(MLA_HEADS):
            qn = q[:, hd * LANE:(hd + 1) * LANE]
            qr = q[:, nope_w + hd * LANE:nope_w + (hd + 1) * LANE]
            ss = jnp.sum(qn * qn + 0.5 * (qr * qr), axis=-1, keepdims=True)
            sq = lax.rsqrt(ss * (1.0 / MLA_QK) + EPS)
            qm_ref[:, hd * HEAD_W:hd * HEAD_W + LANE] = (((qn * sq) * gqn) * mla_scale).astype(BF16)
            qm_ref[:, hd * HEAD_W + LANE:(hd + 1) * HEAD_W] = (
                rope_mla((qr * sq) * gqr) * mla_scale).astype(BF16)

    def k_heads(kv, kpe):
        kpe_sq = 0.5 * (kpe * kpe)
        kpe_rot = rope_mla(kpe * gkr)
        for hd in range(MLA_HEADS):
            kn = kv[:, hd * LANE:(hd + 1) * LANE]
            ssk = jnp.sum(kn * kn + kpe_sq, axis=-1, keepdims=True)
            sk = lax.rsqrt(ssk * (1.0 / MLA_QK) + EPS)
            km_ref[:, hd * HEAD_W:hd * HEAD_W + LANE] = ((kn * sk) * gkn).astype(BF16)
            km_ref[:, hd * HEAD_W + LANE:(hd + 1) * HEAD_W] = (kpe_rot * sk).astype(BF16)
            vm_ref[:, hd * HEAD_W:hd * HEAD_W + LANE] = (
                kv[:, nope_w + hd * LANE:nope_w + (hd + 1) * LANE].astype(BF16))
            vm_ref[:, hd * HEAD_W + LANE:(hd + 1) * HEAD_W] = jnp.ones((tm, LANE), BF16)

    cq = _dot_nt(h, wt_ref[0:O_CKV, :])
    ckv = _dot_nt(h, wt_ref[O_CKV:O_KPE, :])
    kpe = _dot_nt(h, wkpe_ref[...])
    cast_slab(wg_ref, wg16_ref)
    hq = ((cq * _inv_rms(cq, Q_LORA)) * glq_ref[...]).astype(BF16)
    q = jnp.dot(hq, wq_ref[...], preferred_element_type=F32)
    dq0 = _dot_nt(h, wt_ref[O_DQ:O_DQ + half, :])
    hkv = ((ckv * _inv_rms(ckv, KV_LORA)) * glkv_ref[...]).astype(BF16)
    kv = jnp.dot(hkv, wkv_ref[...], preferred_element_type=F32)
    dq1 = _dot_nt(h, wt_ref[O_DQ + half:O_DK, :])
    cast_slab(wu_ref, wu16_ref)
    q_heads(q)
    dk0 = _dot_nt(h, wt_ref[O_DK:O_DK + half, :])
    k_heads(kv, kpe)
    dk1 = _dot_nt(h, wt_ref[O_DK + half:O_DV, :])
    cast_slab(wd_ref, wd16_ref)
    diff_heads(dq0, gdq, diff_scale, qd_ref, 0)
    vd_ref[:, 0:half] = _dot_nt(h, wt_ref[O_DV:O_DV + half, :]).astype(BF16)
    diff_heads(dq1, gdq, diff_scale, qd_ref, half)
    cast_slab(wo_ref, wo16_ref)
    diff_heads(dk0, gdk, 1.0, kd_ref, 0)
    vd_ref[:, half:] = _dot_nt(h, wt_ref[O_DV + half:IN_COLS, :]).astype(BF16)
    diff_heads(dk1, gdk, 1.0, kd_ref, half)


def _qkv(x2d, ga, wt, wkpe, glq, glkv, wq, wkv, lane_gains, cm, sm, cd, sd,
         later_weights, *, seq, tm=256):
    t = x2d.shape[0]
    steps = t // tm
    spt = seq // tm
    tab = pl.BlockSpec((tm, LANE), lambda i: (i % spt, 0))
    vec = lambda n: _resident((1, n))
    widths = [MLA_HEADS * HEAD_W, MLA_HEADS * HEAD_W, MLA_HEADS * HEAD_W,
              DIFF_HEADS * LANE, DIFF_HEADS * LANE, DIFF_HEADS * DIFF_V]
    slabs = [pl.BlockSpec((w.shape[0] // steps, w.shape[1]), lambda i: (i, 0)) for w in later_weights]
    assert all(w.shape[0] % (16 * steps) == 0 for w in later_weights)
    outs = pl.pallas_call(
        _qkv_kernel,
        name="qkv",
        grid=(steps,),
        in_specs=[pl.BlockSpec((tm, D_MODEL), lambda i: (i, 0)), vec(D_MODEL),
                  _resident(wt.shape), _resident(wkpe.shape),
                  vec(Q_LORA), vec(KV_LORA), _resident(wq.shape), _resident(wkv.shape),
                  _resident(lane_gains.shape),
                  tab, tab, tab, tab] + slabs,
        out_specs=[pl.BlockSpec((tm, w), lambda i: (i, 0)) for w in widths] + slabs,
        out_shape=([jax.ShapeDtypeStruct((t, w), BF16) for w in widths]
                   + [jax.ShapeDtypeStruct(w.shape, BF16) for w in later_weights]),
        compiler_params=_params(("arbitrary",)),
    )(x2d, ga, wt, wkpe, glq, glkv, wq, wkv, lane_gains, cm, sm, cd, sd, *later_weights)
    return outs[:len(widths)], outs[len(widths):]


def _causal_keep(tq):
    row = lax.broadcasted_iota(jnp.int32, (tq, tq), 0)
    col = lax.broadcasted_iota(jnp.int32, (tq, tq), 1)
    return col <= row


def _scores(q, k_rows, r0, tq, keep):
    s_d = _dot_nt(q, k_rows(r0, r0 + tq))
    s_d = jnp.where(keep, s_d, -jnp.inf)
    s_o = None
    if r0 > 0:
        s_o = _dot_nt(q, k_rows(0, r0))
    return s_o, s_d


def _softmax_pieces(s, *, with_sum, subtract_max):
    s_o, s_d = s
    if subtract_max:
        m = jnp.max(s_d, axis=-1, keepdims=True)
        if s_o is not None:
            m = jnp.maximum(m, jnp.max(s_o, axis=-1, keepdims=True))
            s_o = s_o - m
        s_d = s_d - m
    p_d = jnp.exp2(s_d)
    p_o = None if s_o is None else jnp.exp2(s_o)
    if not with_sum:
        return p_o, p_d, None
    l = jnp.sum(p_d, axis=-1, keepdims=True)
    if p_o is not None:
        l = jnp.sum(p_o, axis=-1, keepdims=True) + l
    return p_o, p_d, l


def _pv(p_o, p_d, v_rows, r0, tq):
    o = jnp.dot(p_d.astype(BF16), v_rows(r0, r0 + tq), preferred_element_type=F32)
    if p_o is not None:
        o = o + jnp.dot(p_o.astype(BF16), v_rows(0, r0), preferred_element_type=F32)
    return o


def _work_items(seq, tq, heads, descending):
    starts = list(range(0, seq, tq))
    if descending:
        starts = starts[::-1]
    return [(j, r0) for j in range(heads) for r0 in starts]


def _cols(ref, j, width):
    return lambda a, b: ref[a:b, j * width:(j + 1) * width]


def _pipelined(items, scores, finish):
    s_next = scores(items[0])
    for k, item in enumerate(items):
        s = s_next
        if k + 1 < len(items):
            s_next = scores(items[k + 1])
        finish(item, s)


def _either_softmax(bound, body):
    small = bound <= SMALL_SCORE_BOUND
    pl.when(small)(functools.partial(body, subtract_max=False))
    pl.when(jnp.logical_not(small))(functools.partial(body, subtract_max=True))


def _mla_attn_kernel(bound_ref, q_ref, k_ref, v_ref, g_ref, o_ref, *, tq, heads):
    seq = q_ref.shape[0]

    def body(subtract_max):
        keep = _causal_keep(tq)
        g = g_ref[...]

        def scores(item):
            j, r0 = item
            return _scores(_cols(q_ref, j, HEAD_W)(r0, r0 + tq), _cols(k_ref, j, HEAD_W), r0, tq, keep)

        def finish(item, s):
            j, r0 = item
            p_o, p_d, _ = _softmax_pieces(s, with_sum=False, subtract_max=subtract_max)
            ol = _pv(p_o, p_d, _cols(v_ref, j, HEAD_W), r0, tq)
            o = ol[:, :MLA_V] / ol[:, MLA_V:]
            o_ref[r0:r0 + tq, j * MLA_V:(j + 1) * MLA_V] = ((o * _inv_rms(o, MLA_V)) * g).astype(BF16)

        _pipelined(_work_items(seq, tq, heads, descending=True), scores, finish)

    _either_softmax(bound_ref[0, 0], body)


def _diff_attn_kernel(bound_ref, q_ref, k_ref, v_ref, g_ref, lam_ref, o_ref, *, tq, heads):
    _either_softmax(bound_ref[0, 1], functools.partial(_diff_attn_body, q_ref, k_ref, v_ref, g_ref, lam_ref, o_ref,
                                                 tq=tq, heads=heads))


def _diff_attn_body(q_ref, k_ref, v_ref, g_ref, lam_ref, o_ref, *, tq, heads, subtract_max):
    seq = q_ref.shape[0]
    keep = _causal_keep(tq)
    g = g_ref[...]
    lp = lam_ref[...]
    lam = (jnp.exp(jnp.sum(lp[0:1, :] * lp[1:2, :], axis=-1, keepdims=True))
           - jnp.exp(jnp.sum(lp[2:3, :] * lp[3:4, :], axis=-1, keepdims=True))
           + LAMBDA_INIT)
    lane = lax.broadcasted_iota(jnp.int32, (tq, LANE), 1)
    first = lane < DIFF_QK

    def scores(item):
        j, r0 = item
        q = _cols(q_ref, j, LANE)(r0, r0 + tq)
        k_rows = _cols(k_ref, j, LANE)
        q1 = jnp.where(first, q, jnp.zeros_like(q))
        q2 = jnp.where(first, jnp.zeros_like(q), q)
        return _scores(q1, k_rows, r0, tq, keep), _scores(q2, k_rows, r0, tq, keep)

    def finish(item, s):
        j, r0 = item
        p1_o, p1_d, l1 = _softmax_pieces(s[0], with_sum=True, subtract_max=subtract_max)
        p2_o, p2_d, l2 = _softmax_pieces(s[1], with_sum=True, subtract_max=subtract_max)
        c = lam * (l1 / l2)
        a_o = None if p1_o is None else p1_o - c * p2_o
        o = _pv(a_o, p1_d - c * p2_d, _cols(v_ref, j, DIFF_V), r0, tq) / l1
        o_ref[r0:r0 + tq, j * DIFF_V:(j + 1) * DIFF_V] = (
            ((o * _inv_rms(o, DIFF_V)) * g) * (1.0 - LAMBDA_INIT)).astype(BF16)

    _pipelined(_work_items(seq, tq, heads, descending=not subtract_max), scores, finish)


def _head_spec(seq, width, heads):
    return pl.BlockSpec((seq, heads * width), lambda b, h: (b, h))


_SCALAR_SPEC = pl.BlockSpec(memory_space=pltpu.SMEM)


def _score_bound(gq, gk, d):
    return 1.01 * LOG2E * math.sqrt(d) * jnp.max(jnp.abs(gq)) * jnp.max(jnp.abs(gk))


def _mla_attn(bound, qm, km, vm, g, *, batch, seq, tq=256, heads=2):
    t = qm.shape[0]
    return pl.pallas_call(
        functools.partial(_mla_attn_kernel, tq=tq, heads=heads),
        name="mla_attn",
        grid=(batch, MLA_HEADS // heads),
        in_specs=[_SCALAR_SPEC, _head_spec(seq, HEAD_W, heads), _head_spec(seq, HEAD_W, heads), _head_spec(seq, HEAD_W, heads),
                  pl.BlockSpec((1, MLA_V), lambda b, h: (0, 0))],
        out_specs=_head_spec(seq, MLA_V, heads),
        out_shape=jax.ShapeDtypeStruct((t, MLA_HEADS * MLA_V), BF16),
        compiler_params=_params(("arbitrary", "arbitrary")),
    )(bound, qm, km, vm, g)


def _diff_attn(bound, qd, kd, vd, g, lam_params, *, batch, seq, tq=256, heads=2):
    t = qd.shape[0]
    return pl.pallas_call(
        functools.partial(_diff_attn_kernel, tq=tq, heads=heads),
        name="diff_attn",
        grid=(batch, DIFF_HEADS // heads),
        in_specs=[_SCALAR_SPEC, _head_spec(seq, LANE, heads), _head_spec(seq, LANE, heads), _head_spec(seq, DIFF_V, heads),
                  pl.BlockSpec((1, DIFF_V), lambda b, h: (0, 0)),
                  pl.BlockSpec((4, DIFF_QK), lambda b, h: (0, 0))],
        out_specs=_head_spec(seq, DIFF_V, heads),
        out_shape=jax.ShapeDtypeStruct((t, DIFF_HEADS * DIFF_V), BF16),
        compiler_params=_params(("arbitrary", "arbitrary")),
    )(bound, qd, kd, vd, g, lam_params)


def _out_proj_kernel(x_ref, om_ref, od_ref, w_ref, o_ref):
    km = om_ref.shape[1]
    acc = jnp.dot(om_ref[...], w_ref[0:km, :], preferred_element_type=F32)
    acc = acc + jnp.dot(od_ref[...], w_ref[km:, :], preferred_element_type=F32)
    o_ref[...] = x_ref[...] + acc


def _out_proj(x2d, om, od, w, *, tm=512):
    t = x2d.shape[0]
    return pl.pallas_call(
        _out_proj_kernel,
        name="out_proj",
        grid=(t // tm,),
        in_specs=[pl.BlockSpec((tm, D_MODEL), lambda i: (i, 0)),
                  pl.BlockSpec((tm, om.shape[1]), lambda i: (i, 0)),
                  pl.BlockSpec((tm, od.shape[1]), lambda i: (i, 0)),
                  _resident(w.shape)],
        out_specs=pl.BlockSpec((tm, D_MODEL), lambda i: (i, 0)),
        out_shape=jax.ShapeDtypeStruct((t, D_MODEL), F32),
        compiler_params=_params(("arbitrary",)),
    )(x2d, om, od, w)


def _ffn_kernel(x_ref, g_ref, wg_ref, wu_ref, wd_ref, o_ref, h_ref, *, out_chunk):
    @pl.when(pl.program_id(1) == 0)
    def _():
        x = x_ref[...]
        h_ref[...] = ((x * _inv_rms(x, D_MODEL)) * g_ref[...]).astype(BF16)
        o_ref[...] = x

    h = h_ref[...]
    gate = jnp.dot(h, wg_ref[...], preferred_element_type=F32)
    up = jnp.dot(h, wu_ref[...], preferred_element_type=F32)
    act = (gate * jax.nn.sigmoid(gate) * up).astype(BF16)
    for c0 in range(0, o_ref.shape[1], out_chunk):
        o_ref[:, c0:c0 + out_chunk] += jnp.dot(act, wd_ref[:, c0:c0 + out_chunk],
                                               preferred_element_type=F32)


def _ffn(x1, g, wg, wu, wd, *, tm=1024, tf=512):
    t = x1.shape[0]
    return pl.pallas_call(
        functools.partial(_ffn_kernel, out_chunk=512),
        name="ffn",
        grid=(t // tm, D_FF // tf),
        in_specs=[pl.BlockSpec((tm, D_MODEL), lambda i, f: (i, 0)),
                  pl.BlockSpec((1, D_MODEL), lambda i, f: (0, 0)),
                  pl.BlockSpec((D_MODEL, tf), lambda i, f: (0, f)),
                  pl.BlockSpec((D_MODEL, tf), lambda i, f: (0, f)),
                  pl.BlockSpec((tf, D_MODEL), lambda i, f: (f, 0))],
        out_specs=pl.BlockSpec((tm, D_MODEL), lambda i, f: (i, 0)),
        out_shape=jax.ShapeDtypeStruct((t, D_MODEL), F32),
        scratch_shapes=[pltpu.VMEM((tm, D_MODEL), BF16)],
        compiler_params=_params(("arbitrary", "arbitrary")),
    )(x1, g, wg, wu, wd)


def _rope_tables(seq):
    pos = np.arange(seq, dtype=np.float64)

    def angles(r):
        freqs = 1.0 / (ROPE_THETA ** (np.arange(0, r, 2, dtype=np.float64) / r))
        return pos[:, None] * freqs[None, :]

    am = angles(MLA_ROPE)
    cos, sin = np.cos(am), np.sin(am)
    zeros = np.zeros((seq, LANE - MLA_ROPE))
    cm = np.concatenate([cos, cos, zeros], axis=-1)
    sm = np.concatenate([-sin, sin, zeros], axis=-1)

    ad = angles(DIFF_ROT)
    cos, sin = np.cos(ad), np.sin(ad)
    rest = DIFF_QK - DIFF_ROT
    cd = np.tile(np.concatenate([cos, cos, np.ones((seq, rest))], axis=-1), (1, 2))
    sd = np.tile(np.concatenate([-sin, sin, np.zeros((seq, rest))], axis=-1), (1, 2))
    return tuple(jnp.asarray(t.astype(np.float32)) for t in (cm, sm, cd, sd))


def _with_partner(v, axis=-1):
    return jnp.concatenate([v, jnp.roll(v, MLA_ROPE // 2, axis=axis)], axis=axis)


def kernel(x, attn_norm, w_in, q_latent_norm, w_q_up, kv_latent_norm, w_kv_up, mla_q_norm, mla_k_norm, mla_out_norm, diff_q_norm, diff_k_norm, lambda_q1, lambda_k1, lambda_q2, lambda_k2, diff_out_norm, w_o, ffn_norm, w_gate, w_up, w_down):
    batch, seq, d = x.shape
    assert d == D_MODEL and attn_norm.shape[0] == 1
    t = batch * seq
    x2d = x.reshape(t, d)
    l = 0

    wt = jnp.swapaxes(w_in[l], 0, 1).astype(BF16)
    wkpe = _with_partner(wt[O_KPE:O_DQ], axis=0)
    wq = w_q_up[l].astype(BF16).reshape(Q_LORA, MLA_HEADS, MLA_QK)
    wq = jnp.concatenate([wq[:, :, :MLA_NOPE].reshape(Q_LORA, -1),
                          _with_partner(wq[:, :, MLA_NOPE:]).reshape(Q_LORA, -1)], axis=1)
    wkv = w_kv_up[l].astype(BF16).reshape(KV_LORA, MLA_HEADS, MLA_NOPE + MLA_V)
    wkv = jnp.concatenate([wkv[:, :, :MLA_NOPE].reshape(KV_LORA, -1),
                           wkv[:, :, MLA_NOPE:].reshape(KV_LORA, -1)], axis=1)

    gq, gk = mla_q_norm[l], mla_k_norm[l]
    lane_gains = jnp.stack([gq[:MLA_NOPE], _with_partner(gq[MLA_NOPE:]), gk[:MLA_NOPE], _with_partner(gk[MLA_NOPE:]),
                            jnp.tile(diff_q_norm[l], 2), jnp.tile(diff_k_norm[l], 2),
                            jnp.zeros((LANE,), F32), jnp.zeros((LANE,), F32)])
    bounds = jnp.stack([_score_bound(gq, gk, MLA_QK), _score_bound(diff_q_norm[l], diff_k_norm[l], DIFF_QK)])
    bounds = bounds.astype(F32).reshape(1, 2)
    lam_params = jnp.concatenate([lambda_q1[l:l + 1], lambda_k1[l:l + 1],
                                  lambda_q2[l:l + 1], lambda_k2[l:l + 1]], axis=0)
    cm, sm, cd, sd = _rope_tables(seq)

    (qm, km, vm, qd, kd, vd), (wo16, wg16, wu16, wd16) = _qkv(
        x2d, attn_norm[l:l + 1], wt, wkpe, q_latent_norm[l:l + 1], kv_latent_norm[l:l + 1], wq, wkv,
        lane_gains, cm, sm, cd, sd, (w_o[l], w_gate[l], w_up[l], w_down[l]), seq=seq)
    o_mla = _mla_attn(bounds, qm, km, vm, mla_out_norm[l:l + 1], batch=batch, seq=seq)
    o_diff = _diff_attn(bounds, qd, kd, vd, diff_out_norm[l:l + 1], lam_params, batch=batch, seq=seq)
    x1 = _out_proj(x2d, o_mla, o_diff, wo16)
    out = _ffn(x1, ffn_norm[l:l + 1], wg16, wu16, wd16)
    return out.reshape(batch, seq, d)
```

```python
import functools
import math

import jax
import jax.numpy as jnp
import numpy as np
from jax import lax
from jax.experimental import pallas as pl
from jax.experimental.pallas import tpu as pltpu

F32 = jnp.float32
BF16 = jnp.bfloat16

D_MODEL = 2048
MLA_HEADS = 8
MLA_NOPE = 128
MLA_ROPE = 64
MLA_QK = MLA_NOPE + MLA_ROPE
MLA_V = 128
Q_LORA = 512
KV_LORA = 512
DIFF_HEADS = 8
DIFF_QK = 64
DIFF_ROT = 16
DIFF_V = 128
ROPE_THETA = 500000.0
D_FF = 5632
EPS = 1e-6
LAMBDA_INIT = 0.8 - 0.6 * math.exp(-0.3 * 0)
LOG2E = math.log2(math.e)

SMALL_SCORE_BOUND = 32.0
LANE = 128
HEAD_W = 2 * LANE
VMEM_LIMIT = 56 * 1024 * 1024

O_CKV = Q_LORA
O_KPE = O_CKV + KV_LORA
O_DQ = O_KPE + MLA_ROPE
O_DK = O_DQ + DIFF_HEADS * 2 * DIFF_QK
O_DV = O_DK + DIFF_HEADS * 2 * DIFF_QK
IN_COLS = O_DV + DIFF_HEADS * DIFF_V

_NT = (((1,), (1,)), ((), ()))


def _params(sem):
    return pltpu.CompilerParams(dimension_semantics=sem, vmem_limit_bytes=VMEM_LIMIT)


def _resident(shape):
    return pl.BlockSpec(shape, lambda *_: (0,) * len(shape), pipeline_mode=pl.Buffered(1))


def _inv_rms(xf, n):
    return lax.rsqrt(jnp.sum(xf * xf, axis=-1, keepdims=True) * (1.0 / n) + EPS)


def _dot_nt(a, b):
    return lax.dot_general(a, b, _NT, preferred_element_type=F32)


def _qkv_kernel(x_ref, ga_ref, wt_ref, wkpe_ref, glq_ref, glkv_ref, wq_ref, wkv_ref,
                lg_ref,
                cm_ref, sm_ref, cd_ref, sd_ref,
                wo_ref, wg_ref, wu_ref, wd_ref,
                qm_ref, km_ref, vm_ref, qd_ref, kd_ref, vd_ref,
                wo16_ref, wg16_ref, wu16_ref, wd16_ref):
    def cast_slab(src, dst):
        dst[...] = src[...].astype(BF16)

    tm = x_ref.shape[0]
    cm, sm, cd, sd = cm_ref[...], sm_ref[...], cd_ref[...], sd_ref[...]
    x = x_ref[...]
    h = ((x * _inv_rms(x, D_MODEL)) * ga_ref[...]).astype(BF16)

    lane = lax.broadcasted_iota(jnp.int32, (tm, LANE), 1)
    lo = lane < DIFF_QK
    is_x1 = (lane % DIFF_QK) < (DIFF_ROT // 2)

    def prep_diff(xh, g, scale):
        sq = xh * xh
        s_lo = jnp.sum(jnp.where(lo, sq, 0.0), axis=-1, keepdims=True)
        s_hi = jnp.sum(jnp.where(lo, 0.0, sq), axis=-1, keepdims=True)
        inv = lax.rsqrt(jnp.where(lo, s_lo, s_hi) * (1.0 / DIFF_QK) + EPS)
        y = (xh * inv) * g
        partner = jnp.where(is_x1, pltpu.roll(y, LANE - DIFF_ROT // 2, 1), pltpu.roll(y, DIFF_ROT // 2, 1))
        out = y * cd + partner * sd
        return out * scale if scale != 1.0 else out

    def rope_mla(y):
        return y * cm + pltpu.roll(y, MLA_ROPE, 1) * sm

    half = DIFF_HEADS // 2 * LANE
    quarter = half // 2
    lg = lg_ref[...]
    gqn, gqr, gkn, gkr, gdq, gdk = (lg[r:r + 1, :] for r in range(6))
    diff_scale = LOG2E / math.sqrt(DIFF_QK)

    def diff_heads(src, g, scale, dst_ref, c0):
        for j in range(src.shape[1] // LANE):
            dst_ref[:, c0 + j * LANE:c0 + (j + 1) * LANE] = prep_diff(
                src[:, j * LANE:(j + 1) * LANE], g, scale).astype(BF16)

    mla_scale = LOG2E / math.sqrt(MLA_QK)
    nope_w = MLA_HEADS * LANE

    def q_heads(q):
        for hd in range(MLA_HEADS):
            qn = q[:, hd * LANE:(hd + 1) * LANE]
            qr = q[:, nope_w + hd * LANE:nope_w + (hd + 1) * LANE]
            ss = jnp.sum(qn * qn + 0.5 * (qr * qr), axis=-1, keepdims=True)
            sq = lax.rsqrt(ss * (1.0 / MLA_QK) + EPS)
            qm_ref[:, hd * HEAD_W:hd * HEAD_W + LANE] = (((qn * sq) * gqn) * mla_scale).astype(BF16)
            qm_ref[:, hd * HEAD_W + LANE:(hd + 1) * HEAD_W] = (
                rope_mla((qr * sq) * gqr) * mla_scale).astype(BF16)

    def k_heads(kv, kpe):
        kpe_sq = 0.5 * (kpe * kpe)
        kpe_rot = rope_mla(kpe * gkr)
        for hd in range(MLA_HEADS):
            kn = kv[:, hd * LANE:(hd + 1) * LANE]
            ssk = jnp.sum(kn * kn + kpe_sq, axis=-1, keepdims=True)
            sk = lax.rsqrt(ssk * (1.0 / MLA_QK) + EPS)
            km_ref[:, hd * HEAD_W:hd * HEAD_W + LANE] = ((kn * sk) * gkn).astype(BF16)
            km_ref[:, hd * HEAD_W + LANE:(hd + 1) * HEAD_W] = (kpe_rot * sk).astype(BF16)
            vm_ref[:, hd * HEAD_W:hd * HEAD_W + LANE] = (
                kv[:, nope_w + hd * LANE:nope_w + (hd + 1) * LANE].astype(BF16))
            vm_ref[:, hd * HEAD_W + LANE:(hd + 1) * HEAD_W] = jnp.ones((tm, LANE), BF16)

    cq = _dot_nt(h, wt_ref[0:O_CKV, :])
    ckv = _dot_nt(h, wt_ref[O_CKV:O_KPE, :])
    kpe = _dot_nt(h, wkpe_ref[...])
    cast_slab(wg_ref, wg16_ref)
    hq = ((cq * _inv_rms(cq, Q_LORA)) * glq_ref[...]).astype(BF16)
    q = jnp.dot(hq, wq_ref[...], preferred_element_type=F32)
    dq0 = _dot_nt(h, wt_ref[O_DQ:O_DQ + half, :])
    hkv = ((ckv * _inv_rms(ckv, KV_LORA)) * glkv_ref[...]).astype(BF16)
    kv = jnp.dot(hkv, wkv_ref[...], preferred_element_type=F32)
    dq1 = _dot_nt(h, wt_ref[O_DQ + half:O_DK, :])
    cast_slab(wu_ref, wu16_ref)
    q_heads(q)
    dk0 = _dot_nt(h, wt_ref[O_DK:O_DK + half, :])
    k_heads(kv, kpe)
    dk1a = _dot_nt(h, wt_ref[O_DK + half:O_DK + half + quarter, :])
    dk1b = _dot_nt(h, wt_ref[O_DK + half + quarter:O_DV, :])
    cast_slab(wd_ref, wd16_ref)
    diff_heads(dq0, gdq, diff_scale, qd_ref, 0)
    vd_ref[:, 0:half] = _dot_nt(h, wt_ref[O_DV:O_DV + half, :]).astype(BF16)
    diff_heads(dq1, gdq, diff_scale, qd_ref, half)
    cast_slab(wo_ref, wo16_ref)
    diff_heads(dk0, gdk, 1.0, kd_ref, 0)
    vd_ref[:, half:] = _dot_nt(h, wt_ref[O_DV + half:IN_COLS, :]).astype(BF16)
    diff_heads(dk1a, gdk, 1.0, kd_ref, half)
    diff_heads(dk1b, gdk, 1.0, kd_ref, half + quarter)


def _qkv(x2d, ga, wt, wkpe, glq, glkv, wq, wkv, lane_gains, cm, sm, cd, sd,
         later_weights, *, seq, tm=256):
    t = x2d.shape[0]
    steps = t // tm
    spt = seq // tm
    tab = pl.BlockSpec((tm, LANE), lambda i: (i % spt, 0))
    vec = lambda n: _resident((1, n))
    widths = [MLA_HEADS * HEAD_W, MLA_HEADS * HEAD_W, MLA_HEADS * HEAD_W,
              DIFF_HEADS * LANE, DIFF_HEADS * LANE, DIFF_HEADS * DIFF_V]
    slabs = [pl.BlockSpec((w.shape[0] // steps, w.shape[1]), lambda i: (i, 0)) for w in later_weights]
    assert all(w.shape[0] % (16 * steps) == 0 for w in later_weights)
    outs = pl.pallas_call(
        _qkv_kernel,
        name="qkv",
        grid=(steps,),
        in_specs=[pl.BlockSpec((tm, D_MODEL), lambda i: (i, 0)), vec(D_MODEL),
                  _resident(wt.shape), _resident(wkpe.shape),
                  vec(Q_LORA), vec(KV_LORA), _resident(wq.shape), _resident(wkv.shape),
                  _resident(lane_gains.shape),
                  tab, tab, tab, tab] + slabs,
        out_specs=[pl.BlockSpec((tm, w), lambda i: (i, 0)) for w in widths] + slabs,
        out_shape=([jax.ShapeDtypeStruct((t, w), BF16) for w in widths]
                   + [jax.ShapeDtypeStruct(w.shape, BF16) for w in later_weights]),
        compiler_params=_params(("arbitrary",)),
    )(x2d, ga, wt, wkpe, glq, glkv, wq, wkv, lane_gains, cm, sm, cd, sd, *later_weights)
    return outs[:len(widths)], outs[len(widths):]


def _causal_keep(tq):
    row = lax.broadcasted_iota(jnp.int32, (tq, tq), 0)
    col = lax.broadcasted_iota(jnp.int32, (tq, tq), 1)
    return col <= row


def _scores(q, k_rows, r0, tq, keep):
    s_d = _dot_nt(q, k_rows(r0, r0 + tq))
    s_d = jnp.where(keep, s_d, -jnp.inf)
    s_o = None
    if r0 > 0:
        s_o = _dot_nt(q, k_rows(0, r0))
    return s_o, s_d


def _softmax_pieces(s, *, with_sum, subtract_max):
    s_o, s_d = s
    if subtract_max:
        m = jnp.max(s_d, axis=-1, keepdims=True)
        if s_o is not None:
            m = jnp.maximum(m, jnp.max(s_o, axis=-1, keepdims=True))
            s_o = s_o - m
        s_d = s_d - m
    p_d = jnp.exp2(s_d)
    p_o = None if s_o is None else jnp.exp2(s_o)
    if not with_sum:
        return p_o, p_d, None
    l = jnp.sum(p_d, axis=-1, keepdims=True)
    if p_o is not None:
        l = jnp.sum(p_o, axis=-1, keepdims=True) + l
    return p_o, p_d, l


def _pv(p_o, p_d, v_rows, r0, tq):
    o = jnp.dot(p_d.astype(BF16), v_rows(r0, r0 + tq), preferred_element_type=F32)
    if p_o is not None:
        o = o + jnp.dot(p_o.astype(BF16), v_rows(0, r0), preferred_element_type=F32)
    return o


def _work_items(seq, tq, heads, descending):
    starts = list(range(0, seq, tq))
    if descending:
        starts = starts[::-1]
    return [(j, r0) for j in range(heads) for r0 in starts]


def _cols(ref, j, width):
    return lambda a, b: ref[a:b, j * width:(j + 1) * width]


def _pipelined(items, scores, finish):
    s_next = scores(items[0])
    for k, item in enumerate(items):
        s = s_next
        if k + 1 < len(items):
            s_next = scores(items[k + 1])
        finish(item, s)


def _either_softmax(bound, body):
    small = bound <= SMALL_SCORE_BOUND
    pl.when(small)(functools.partial(body, subtract_max=False))
    pl.when(jnp.logical_not(small))(functools.partial(body, subtract_max=True))


def _mla_attn_kernel(bound_ref, q_ref, k_ref, v_ref, g_ref, o_ref, *, tq, heads):
    seq = q_ref.shape[0]

    def body(subtract_max):
        keep = _causal_keep(tq)
        g = g_ref[...]

        def scores(item):
            j, r0 = item
            return _scores(_cols(q_ref, j, HEAD_W)(r0, r0 + tq), _cols(k_ref, j, HEAD_W), r0, tq, keep)

        def finish(item, s):
            j, r0 = item
            p_o, p_d, _ = _softmax_pieces(s, with_sum=False, subtract_max=subtract_max)
            ol = _pv(p_o, p_d, _cols(v_ref, j, HEAD_W), r0, tq)
            o = ol[:, :MLA_V] / ol[:, MLA_V:]
            o_ref[r0:r0 + tq, j * MLA_V:(j + 1) * MLA_V] = ((o * _inv_rms(o, MLA_V)) * g).astype(BF16)

        _pipelined(_work_items(seq, tq, heads, descending=True), scores, finish)

    _either_softmax(bound_ref[0, 0], body)


def _diff_attn_kernel(bound_ref, q_ref, k_ref, v_ref, g_ref, lam_ref, o_ref, *, tq, heads):
    _either_softmax(bound_ref[0, 1], functools.partial(_diff_attn_body, q_ref, k_ref, v_ref, g_ref, lam_ref, o_ref,
                                                 tq=tq, heads=heads))


def _diff_attn_body(q_ref, k_ref, v_ref, g_ref, lam_ref, o_ref, *, tq, heads, subtract_max):
    seq = q_ref.shape[0]
    keep = _causal_keep(tq)
    g = g_ref[...]
    lp = lam_ref[...]
    lam = (jnp.exp(jnp.sum(lp[0:1, :] * lp[1:2, :], axis=-1, keepdims=True))
           - jnp.exp(jnp.sum(lp[2:3, :] * lp[3:4, :], axis=-1, keepdims=True))
           + LAMBDA_INIT)
    lane = lax.broadcasted_iota(jnp.int32, (tq, LANE), 1)
    first = lane < DIFF_QK

    def scores(item):
        j, r0 = item
        q = _cols(q_ref, j, LANE)(r0, r0 + tq)
        k_rows = _cols(k_ref, j, LANE)
        q1 = jnp.where(first, q, jnp.zeros_like(q))
        q2 = jnp.where(first, jnp.zeros_like(q), q)
        return _scores(q1, k_rows, r0, tq, keep), _scores(q2, k_rows, r0, tq, keep)

    def finish(item, s):
        j, r0 = item
        p1_o, p1_d, l1 = _softmax_pieces(s[0], with_sum=True, subtract_max=subtract_max)
        p2_o, p2_d, l2 = _softmax_pieces(s[1], with_sum=True, subtract_max=subtract_max)
        c = lam * (l1 / l2)
        a_o = None if p1_o is None else p1_o - c * p2_o
        o = _pv(a_o, p1_d - c * p2_d, _cols(v_ref, j, DIFF_V), r0, tq) / l1
        o_ref[r0:r0 + tq, j * DIFF_V:(j + 1) * DIFF_V] = (
            ((o * _inv_rms(o, DIFF_V)) * g) * (1.0 - LAMBDA_INIT)).astype(BF16)

    _pipelined(_work_items(seq, tq, heads, descending=not subtract_max), scores, finish)


def _head_spec(seq, width, heads):
    return pl.BlockSpec((seq, heads * width), lambda b, h: (b, h))


_SCALAR_SPEC = pl.BlockSpec(memory_space=pltpu.SMEM)


def _score_bound(gq, gk, d):
    return 1.01 * LOG2E * math.sqrt(d) * jnp.max(jnp.abs(gq)) * jnp.max(jnp.abs(gk))


def _mla_attn(bound, qm, km, vm, g, *, batch, seq, tq=256, heads=2):
    t = qm.shape[0]
    return pl.pallas_call(
        functools.partial(_mla_attn_kernel, tq=tq, heads=heads),
        name="mla_attn",
        grid=(batch, MLA_HEADS // heads),
        in_specs=[_SCALAR_SPEC, _head_spec(seq, HEAD_W, heads), _head_spec(seq, HEAD_W, heads), _head_spec(seq, HEAD_W, heads),
                  pl.BlockSpec((1, MLA_V), lambda b, h: (0, 0))],
        out_specs=_head_spec(seq, MLA_V, heads),
        out_shape=jax.ShapeDtypeStruct((t, MLA_HEADS * MLA_V), BF16),
        compiler_params=_params(("arbitrary", "arbitrary")),
    )(bound, qm, km, vm, g)


def _diff_attn(bound, qd, kd, vd, g, lam_params, *, batch, seq, tq=256, heads=2):
    t = qd.shape[0]
    return pl.pallas_call(
        functools.partial(_diff_attn_kernel, tq=tq, heads=heads),
        name="diff_attn",
        grid=(batch, DIFF_HEADS // heads),
        in_specs=[_SCALAR_SPEC, _head_spec(seq, LANE, heads), _head_spec(seq, LANE, heads), _head_spec(seq, DIFF_V, heads),
                  pl.BlockSpec((1, DIFF_V), lambda b, h: (0, 0)),
                  pl.BlockSpec((4, DIFF_QK), lambda b, h: (0, 0))],
        out_specs=_head_spec(seq, DIFF_V, heads),
        out_shape=jax.ShapeDtypeStruct((t, DIFF_HEADS * DIFF_V), BF16),
        compiler_params=_params(("arbitrary", "arbitrary")),
    )(bound, qd, kd, vd, g, lam_params)


def _out_proj_kernel(x_ref, om_ref, od_ref, w_ref, o_ref):
    km = om_ref.shape[1]
    acc = jnp.dot(om_ref[...], w_ref[0:km, :], preferred_element_type=F32)
    acc = acc + jnp.dot(od_ref[...], w_ref[km:, :], preferred_element_type=F32)
    o_ref[...] = x_ref[...] + acc


def _out_proj(x2d, om, od, w, *, tm=512):
    t = x2d.shape[0]
    return pl.pallas_call(
        _out_proj_kernel,
        name="out_proj",
        grid=(t // tm,),
        in_specs=[pl.BlockSpec((tm, D_MODEL), lambda i: (i, 0)),
                  pl.BlockSpec((tm, om.shape[1]), lambda i: (i, 0)),
                  pl.BlockSpec((tm, od.shape[1]), lambda i: (i, 0)),
                  _resident(w.shape)],
        out_specs=pl.BlockSpec((tm, D_MODEL), lambda i: (i, 0)),
        out_shape=jax.ShapeDtypeStruct((t, D_MODEL), F32),
        compiler_params=_params(("arbitrary",)),
    )(x2d, om, od, w)


def _ffn_kernel(x_ref, g_ref, wg_ref, wu_ref, wd_ref, o_ref, h_ref, *, out_chunk):
    @pl.when(pl.program_id(1) == 0)
    def _():
        x = x_ref[...]
        h_ref[...] = ((x * _inv_rms(x, D_MODEL)) * g_ref[...]).astype(BF16)
        o_ref[...] = x

    h = h_ref[...]
    gate = jnp.dot(h, wg_ref[...], preferred_element_type=F32)
    up = jnp.dot(h, wu_ref[...], preferred_element_type=F32)
    act = (gate * jax.nn.sigmoid(gate) * up).astype(BF16)
    for c0 in range(0, o_ref.shape[1], out_chunk):
        o_ref[:, c0:c0 + out_chunk] += jnp.dot(act, wd_ref[:, c0:c0 + out_chunk],
                                               preferred_element_type=F32)


def _ffn(x1, g, wg, wu, wd, *, tm=1024, tf=512):
    t = x1.shape[0]
    return pl.pallas_call(
        functools.partial(_ffn_kernel, out_chunk=512),
        name="ffn",
        grid=(t // tm, D_FF // tf),
        in_specs=[pl.BlockSpec((tm, D_MODEL), lambda i, f: (i, 0)),
                  pl.BlockSpec((1, D_MODEL), lambda i, f: (0, 0)),
                  pl.BlockSpec((D_MODEL, tf), lambda i, f: (0, f)),
                  pl.BlockSpec((D_MODEL, tf), lambda i, f: (0, f)),
                  pl.BlockSpec((tf, D_MODEL), lambda i, f: (f, 0))],
        out_specs=pl.BlockSpec((tm, D_MODEL), lambda i, f: (i, 0)),
        out_shape=jax.ShapeDtypeStruct((t, D_MODEL), F32),
        scratch_shapes=[pltpu.VMEM((tm, D_MODEL), BF16)],
        compiler_params=_params(("arbitrary", "arbitrary")),
    )(x1, g, wg, wu, wd)


def _rope_tables(seq):
    pos = np.arange(seq, dtype=np.float64)

    def angles(r):
        freqs = 1.0 / (ROPE_THETA ** (np.arange(0, r, 2, dtype=np.float64) / r))
        return pos[:, None] * freqs[None, :]

    am = angles(MLA_ROPE)
    cos, sin = np.cos(am), np.sin(am)
    zeros = np.zeros((seq, LANE - MLA_ROPE))
    cm = np.concatenate([cos, cos, zeros], axis=-1)
    sm = np.concatenate([-sin, sin, zeros], axis=-1)

    ad = angles(DIFF_ROT)
    cos, sin = np.cos(ad), np.sin(ad)
    rest = DIFF_QK - DIFF_ROT
    cd = np.tile(np.concatenate([cos, cos, np.ones((seq, rest))], axis=-1), (1, 2))
    sd = np.tile(np.concatenate([-sin, sin, np.zeros((seq, rest))], axis=-1), (1, 2))
    return tuple(jnp.asarray(t.astype(np.float32)) for t in (cm, sm, cd, sd))


def _with_partner(v, axis=-1):
    return jnp.concatenate([v, jnp.roll(v, MLA_ROPE // 2, axis=axis)], axis=axis)


def kernel(x, attn_norm, w_in, q_latent_norm, w_q_up, kv_latent_norm, w_kv_up, mla_q_norm, mla_k_norm, mla_out_norm, diff_q_norm, diff_k_norm, lambda_q1, lambda_k1, lambda_q2, lambda_k2, diff_out_norm, w_o, ffn_norm, w_gate, w_up, w_down):
    batch, seq, d = x.shape
    assert d == D_MODEL and attn_norm.shape[0] == 1
    t = batch * seq
    x2d = x.reshape(t, d)
    l = 0

    wt = jnp.swapaxes(w_in[l], 0, 1).astype(BF16)
    wkpe = _with_partner(wt[O_KPE:O_DQ], axis=0)
    wq = w_q_up[l].astype(BF16).reshape(Q_LORA, MLA_HEADS, MLA_QK)
    wq = jnp.concatenate([wq[:, :, :MLA_NOPE].reshape(Q_LORA, -1),
                          _with_partner(wq[:, :, MLA_NOPE:]).reshape(Q_LORA, -1)], axis=1)
    wkv = w_kv_up[l].astype(BF16).reshape(KV_LORA, MLA_HEADS, MLA_NOPE + MLA_V)
    wkv = jnp.concatenate([wkv[:, :, :MLA_NOPE].reshape(KV_LORA, -1),
                           wkv[:, :, MLA_NOPE:].reshape(KV_LORA, -1)], axis=1)

    gq, gk = mla_q_norm[l], mla_k_norm[l]
    lane_gains = jnp.stack([gq[:MLA_NOPE], _with_partner(gq[MLA_NOPE:]), gk[:MLA_NOPE], _with_partner(gk[MLA_NOPE:]),
                            jnp.tile(diff_q_norm[l], 2), jnp.tile(diff_k_norm[l], 2),
                            jnp.zeros((LANE,), F32), jnp.zeros((LANE,), F32)])
    bounds = jnp.stack([_score_bound(gq, gk, MLA_QK), _score_bound(diff_q_norm[l], diff_k_norm[l], DIFF_QK)])
    bounds = bounds.astype(F32).reshape(1, 2)
    lam_params = jnp.concatenate([lambda_q1[l:l + 1], lambda_k1[l:l + 1],
                                  lambda_q2[l:l + 1], lambda_k2[l:l + 1]], axis=0)
    cm, sm, cd, sd = _rope_tables(seq)

    (qm, km, vm, qd, kd, vd), (wo16, wg16, wu16, wd16) = _qkv(
        x2d, attn_norm[l:l + 1], wt, wkpe, q_latent_norm[l:l + 1], kv_latent_norm[l:l + 1], wq, wkv,
        lane_gains, cm, sm, cd, sd, (w_o[l], w_gate[l], w_up[l], w_down[l]), seq=seq)
    o_mla = _mla_attn(bounds, qm, km, vm, mla_out_norm[l:l + 1], batch=batch, seq=seq)
    o_diff = _diff_attn(bounds, qd, kd, vd, diff_out_norm[l:l + 1], lam_params, batch=batch, seq=seq)
    x1 = _out_proj(x2d, o_mla, o_diff, wo16)
    out = _ffn(x1, ffn_norm[l:l + 1], wg16, wu16, wd16)
    return out.reshape(batch, seq, d)
```

```python
import functools
import math

import jax
import jax.numpy as jnp
import numpy as np
from jax import lax
from jax.experimental import pallas as pl
from jax.experimental.pallas import tpu as pltpu

F32 = jnp.float32
BF16 = jnp.bfloat16

D_MODEL = 2048
MLA_HEADS = 8
MLA_NOPE = 128
MLA_ROPE = 64
MLA_QK = MLA_NOPE + MLA_ROPE
MLA_V = 128
Q_LORA = 512
KV_LORA = 512
DIFF_HEADS = 8
DIFF_QK = 64
DIFF_ROT = 16
DIFF_V = 128
ROPE_THETA = 500000.0
D_FF = 5632
EPS = 1e-6
LAMBDA_INIT = 0.8 - 0.6 * math.exp(-0.3 * 0)
LOG2E = math.log2(math.e)

SMALL_SCORE_BOUND = 32.0
LANE = 128
HEAD_W = 2 * LANE
VMEM_LIMIT = 56 * 1024 * 1024

O_CKV = Q_LORA
O_KPE = O_CKV + KV_LORA
O_DQ = O_KPE + MLA_ROPE
O_DK = O_DQ + DIFF_HEADS * 2 * DIFF_QK
O_DV = O_DK + DIFF_HEADS * 2 * DIFF_QK
IN_COLS = O_DV + DIFF_HEADS * DIFF_V

_NT = (((1,), (1,)), ((), ()))


def _params(sem):
    return pltpu.CompilerParams(dimension_semantics=sem, vmem_limit_bytes=VMEM_LIMIT)


def _resident(shape):
    return pl.BlockSpec(shape, lambda *_: (0,) * len(shape), pipeline_mode=pl.Buffered(1))


def _inv_rms(xf, n):
    return lax.rsqrt(jnp.sum(xf * xf, axis=-1, keepdims=True) * (1.0 / n) + EPS)


def _dot_nt(a, b):
    return lax.dot_general(a, b, _NT, preferred_element_type=F32)


def _qkv_kernel(x_ref, ga_ref, wt_ref, wkpe_ref, glq_ref, glkv_ref, wq_ref, wkv_ref,
                lg_ref,
                cm_ref, sm_ref, cd_ref, sd_ref,
                wo_ref, wg_ref, wu_ref, wd_ref,
                qm_ref, km_ref, vm_ref, qd_ref, kd_ref, vd_ref,
                wo16_ref, wg16_ref, wu16_ref, wd16_ref):
    def cast_slab(src, dst):
        dst[...] = src[...].astype(BF16)

    tm = x_ref.shape[0]
    cm, sm, cd, sd = cm_ref[...], sm_ref[...], cd_ref[...], sd_ref[...]
    x = x_ref[...]
    h = ((x * _inv_rms(x, D_MODEL)) * ga_ref[...]).astype(BF16)

    lane = lax.broadcasted_iota(jnp.int32, (tm, LANE), 1)
    lo = lane < DIFF_QK
    is_x1 = (lane % DIFF_QK) < (DIFF_ROT // 2)

    def prep_diff(xh, g, scale):
        sq = xh * xh
        s_lo = jnp.sum(jnp.where(lo, sq, 0.0), axis=-1, keepdims=True)
        s_hi = jnp.sum(jnp.where(lo, 0.0, sq), axis=-1, keepdims=True)
        inv = lax.rsqrt(jnp.where(lo, s_lo, s_hi) * (1.0 / DIFF_QK) + EPS)
        y = (xh * inv) * g
        partner = jnp.where(is_x1, pltpu.roll(y, LANE - DIFF_ROT // 2, 1), pltpu.roll(y, DIFF_ROT // 2, 1))
        out = y * cd + partner * sd
        return out * scale if scale != 1.0 else out

    def rope_mla(y):
        return y * cm + pltpu.roll(y, MLA_ROPE, 1) * sm

    half = DIFF_HEADS // 2 * LANE
    lg = lg_ref[...]
    gqn, gqr, gkn, gkr, gdq, gdk = (lg[r:r + 1, :] for r in range(6))
    diff_scale = LOG2E / math.sqrt(DIFF_QK)

    def diff_heads(src, g, scale, dst_ref, c0):
        for j in range(DIFF_HEADS // 2):
            dst_ref[:, c0 + j * LANE:c0 + (j + 1) * LANE] = prep_diff(
                src[:, j * LANE:(j + 1) * LANE], g, scale).astype(BF16)

    mla_scale = LOG2E / math.sqrt(MLA_QK)
    nope_w = MLA_HEADS * LANE

    def q_heads(q):
        for hd in range(MLA_HEADS):
            qn = q[:, hd * LANE:(hd + 1) * LANE]
            qr = q[:, nope_w + hd * LANE:nope_w + (hd + 1) * LANE]
            ss = jnp.sum(qn * qn + 0.5 * (qr * qr), axis=-1, keepdims=True)
            sq = lax.rsqrt(ss * (1.0 / MLA_QK) + EPS)
            qm_ref[:, hd * HEAD_W:hd * HEAD_W + LANE] = (((qn * sq) * gqn) * mla_scale).astype(BF16)
            qm_ref[:, hd * HEAD_W + LANE:(hd + 1) * HEAD_W] = (
                rope_mla((qr * sq) * gqr) * mla_scale).astype(BF16)

    def k_heads(kv, kpe):
        kpe_sq = 0.5 * (kpe * kpe)
        kpe_rot = rope_mla(kpe * gkr)
        for hd in range(MLA_HEADS):
            kn = kv[:, hd * LANE:(hd + 1) * LANE]
            ssk = jnp.sum(kn * kn + kpe_sq, axis=-1, keepdims=True)
            sk = lax.rsqrt(ssk * (1.0 / MLA_QK) + EPS)
            km_ref[:, hd * HEAD_W:hd * HEAD_W + LANE] = ((kn * sk) * gkn).astype(BF16)
            km_ref[:, hd * HEAD_W + LANE:(hd + 1) * HEAD_W] = (kpe_rot * sk).astype(BF16)
            vm_ref[:, hd * HEAD_W:hd * HEAD_W + LANE] = (
                kv[:, nope_w + hd * LANE:nope_w + (hd + 1) * LANE].astype(BF16))
            vm_ref[:, hd * HEAD_W + LANE:(hd + 1) * HEAD_W] = jnp.ones((tm, LANE), BF16)

    cq = _dot_nt(h, wt_ref[0:O_CKV, :])
    ckv = _dot_nt(h, wt_ref[O_CKV:O_KPE, :])
    kpe = _dot_nt(h, wkpe_ref[...])
    cast_slab(wg_ref, wg16_ref)
    hq = ((cq * _inv_rms(cq, Q_LORA)) * glq_ref[...]).astype(BF16)
    q = jnp.dot(hq, wq_ref[...], preferred_element_type=F32)
    dq0 = _dot_nt(h, wt_ref[O_DQ:O_DQ + half, :])
    hkv = ((ckv * _inv_rms(ckv, KV_LORA)) * glkv_ref[...]).astype(BF16)
    kv = jnp.dot(hkv, wkv_ref[...], preferred_element_type=F32)
    dq1 = _dot_nt(h, wt_ref[O_DQ + half:O_DK, :])
    cast_slab(wu_ref, wu16_ref)
    q_heads(q)
    dk0 = _dot_nt(h, wt_ref[O_DK:O_DK + half, :])
    k_heads(kv, kpe)
    dk1 = _dot_nt(h, wt_ref[O_DK + half:O_DV, :])
    cast_slab(wd_ref, wd16_ref)
    diff_heads(dq0, gdq, diff_scale, qd_ref, 0)
    vd_ref[:, 0:half] = _dot_nt(h, wt_ref[O_DV:O_DV + half, :]).astype(BF16)
    diff_heads(dq1, gdq, diff_scale, qd_ref, half)
    cast_slab(wo_ref, wo16_ref)
    diff_heads(dk0, gdk, 1.0, kd_ref, 0)
    vd_ref[:, half:] = _dot_nt(h, wt_ref[O_DV + half:IN_COLS, :]).astype(BF16)
    diff_heads(dk1, gdk, 1.0, kd_ref, half)


def _qkv(x2d, ga, wt, wkpe, glq, glkv, wq, wkv, lane_gains, cm, sm, cd, sd,
         later_weights, *, seq, tm=256):
    t = x2d.shape[0]
    steps = t // tm
    spt = seq // tm
    tab = pl.BlockSpec((tm, LANE), lambda i: (i % spt, 0))
    vec = lambda n: _resident((1, n))
    widths = [MLA_HEADS * HEAD_W, MLA_HEADS * HEAD_W, MLA_HEADS * HEAD_W,
              DIFF_HEADS * LANE, DIFF_HEADS * LANE, DIFF_HEADS * DIFF_V]
    slabs = [pl.BlockSpec((w.shape[0] // steps, w.shape[1]), lambda i: (i, 0)) for w in later_weights]
    assert all(w.shape[0] % (16 * steps) == 0 for w in later_weights)
    outs = pl.pallas_call(
        _qkv_kernel,
        name="qkv",
        grid=(steps,),
        in_specs=[pl.BlockSpec((tm, D_MODEL), lambda i: (i, 0)), vec(D_MODEL),
                  _resident(wt.shape), _resident(wkpe.shape),
                  vec(Q_LORA), vec(KV_LORA), _resident(wq.shape), _resident(wkv.shape),
                  _resident(lane_gains.shape),
                  tab, tab, tab, tab] + slabs,
        out_specs=[pl.BlockSpec((tm, w), lambda i: (i, 0)) for w in widths] + slabs,
        out_shape=([jax.ShapeDtypeStruct((t, w), BF16) for w in widths]
                   + [jax.ShapeDtypeStruct(w.shape, BF16) for w in later_weights]),
        compiler_params=_params(("arbitrary",)),
    )(x2d, ga, wt, wkpe, glq, glkv, wq, wkv, lane_gains, cm, sm, cd, sd, *later_weights)
    return outs[:len(widths)], outs[len(widths):]


def _causal_keep(tq):
    row = lax.broadcasted_iota(jnp.int32, (tq, tq), 0)
    col = lax.broadcasted_iota(jnp.int32, (tq, tq), 1)
    return col <= row


def _scores(q, k_rows, r0, tq, keep):
    s_d = _dot_nt(q, k_rows(r0, r0 + tq))
    s_d = jnp.where(keep, s_d, -jnp.inf)
    s_o = None
    if r0 > 0:
        s_o = _dot_nt(q, k_rows(0, r0))
    return s_o, s_d


def _softmax_pieces(s, *, with_sum, subtract_max):
    s_o, s_d = s
    if subtract_max:
        m = jnp.max(s_d, axis=-1, keepdims=True)
        if s_o is not None:
            m = jnp.maximum(m, jnp.max(s_o, axis=-1, keepdims=True))
            s_o = s_o - m
        s_d = s_d - m
    p_d = jnp.exp2(s_d)
    p_o = None if s_o is None else jnp.exp2(s_o)
    if not with_sum:
        return p_o, p_d, None
    l = jnp.sum(p_d, axis=-1, keepdims=True)
    if p_o is not None:
        l = jnp.sum(p_o, axis=-1, keepdims=True) + l
    return p_o, p_d, l


def _pv(p_o, p_d, v_rows, r0, tq):
    o = jnp.dot(p_d.astype(BF16), v_rows(r0, r0 + tq), preferred_element_type=F32)
    if p_o is not None:
        o = o + jnp.dot(p_o.astype(BF16), v_rows(0, r0), preferred_element_type=F32)
    return o


def _work_items(seq, tq, heads, descending):
    starts = list(range(0, seq, tq))
    if descending:
        starts = starts[::-1]
    return [(j, r0) for j in range(heads) for r0 in starts]


def _cols(ref, j, width):
    return lambda a, b: ref[a:b, j * width:(j + 1) * width]


def _pipelined(items, scores, finish):
    s_next = scores(items[0])
    for k, item in enumerate(items):
        s = s_next
        if k + 1 < len(items):
            s_next = scores(items[k + 1])
        finish(item, s)


def _either_softmax(bound, body):
    small = bound <= SMALL_SCORE_BOUND
    pl.when(small)(functools.partial(body, subtract_max=False))
    pl.when(jnp.logical_not(small))(functools.partial(body, subtract_max=True))


def _mla_attn_kernel(bound_ref, q_ref, k_ref, v_ref, g_ref, o_ref, *, tq, heads):
    seq = q_ref.shape[0]

    def body(subtract_max):
        keep = _causal_keep(tq)
        g = g_ref[...]

        def scores(item):
            j, r0 = item
            return _scores(_cols(q_ref, j, HEAD_W)(r0, r0 + tq), _cols(k_ref, j, HEAD_W), r0, tq, keep)

        def finish(item, s):
            j, r0 = item
            p_o, p_d, _ = _softmax_pieces(s, with_sum=False, subtract_max=subtract_max)
            ol = _pv(p_o, p_d, _cols(v_ref, j, HEAD_W), r0, tq)
            o = ol[:, :MLA_V] / ol[:, MLA_V:]
            o_ref[r0:r0 + tq, j * MLA_V:(j + 1) * MLA_V] = ((o * _inv_rms(o, MLA_V)) * g).astype(BF16)

        _pipelined(_work_items(seq, tq, heads, descending=True), scores, finish)

    _either_softmax(bound_ref[0, 0], body)


def _diff_attn_kernel(bound_ref, q_ref, k_ref, v_ref, g_ref, lam_ref, o_ref, *, tq, heads):
    _either_softmax(bound_ref[0, 1], functools.partial(_diff_attn_body, q_ref, k_ref, v_ref, g_ref, lam_ref, o_ref,
                                                 tq=tq, heads=heads))


def _diff_attn_body(q_ref, k_ref, v_ref, g_ref, lam_ref, o_ref, *, tq, heads, subtract_max):
    seq = q_ref.shape[0]
    keep = _causal_keep(tq)
    g = g_ref[...]
    lp = lam_ref[...]
    lam = (jnp.exp(jnp.sum(lp[0:1, :] * lp[1:2, :], axis=-1, keepdims=True))
           - jnp.exp(jnp.sum(lp[2:3, :] * lp[3:4, :], axis=-1, keepdims=True))
           + LAMBDA_INIT)
    lane = lax.broadcasted_iota(jnp.int32, (tq, LANE), 1)
    first = lane < DIFF_QK

    def scores(item):
        j, r0 = item
        q = _cols(q_ref, j, LANE)(r0, r0 + tq)
        k_rows = _cols(k_ref, j, LANE)
        q1 = jnp.where(first, q, jnp.zeros_like(q))
        q2 = jnp.where(first, jnp.zeros_like(q), q)
        return _scores(q1, k_rows, r0, tq, keep), _scores(q2, k_rows, r0, tq, keep)

    def finish(item, s):
        j, r0 = item
        p1_o, p1_d, l1 = _softmax_pieces(s[0], with_sum=True, subtract_max=subtract_max)
        p2_o, p2_d, l2 = _softmax_pieces(s[1], with_sum=True, subtract_max=subtract_max)
        c = lam * (l1 / l2)
        a_o = None if p1_o is None else p1_o - c * p2_o
        o = _pv(a_o, p1_d - c * p2_d, _cols(v_ref, j, DIFF_V), r0, tq) / l1
        o_ref[r0:r0 + tq, j * DIFF_V:(j + 1) * DIFF_V] = (
            ((o * _inv_rms(o, DIFF_V)) * g) * (1.0 - LAMBDA_INIT)).astype(BF16)

    _pipelined(_work_items(seq, tq, heads, descending=not subtract_max), scores, finish)


def _head_spec(seq, width, heads):
    return pl.BlockSpec((seq, heads * width), lambda b, h: (b, h))


_SCALAR_SPEC = pl.BlockSpec(memory_space=pltpu.SMEM)


def _score_bound(gq, gk, d):
    return 1.01 * LOG2E * math.sqrt(d) * jnp.max(jnp.abs(gq)) * jnp.max(jnp.abs(gk))


def _mla_attn(bound, qm, km, vm, g, *, batch, seq, tq=256, heads=2):
    t = qm.shape[0]
    return pl.pallas_call(
        functools.partial(_mla_attn_kernel, tq=tq, heads=heads),
        name="mla_attn",
        grid=(batch, MLA_HEADS // heads),
        in_specs=[_SCALAR_SPEC, _head_spec(seq, HEAD_W, heads), _head_spec(seq, HEAD_W, heads), _head_spec(seq, HEAD_W, heads),
                  pl.BlockSpec((1, MLA_V), lambda b, h: (0, 0))],
        out_specs=_head_spec(seq, MLA_V, heads),
        out_shape=jax.ShapeDtypeStruct((t, MLA_HEADS * MLA_V), BF16),
        compiler_params=_params(("arbitrary", "arbitrary")),
    )(bound, qm, km, vm, g)


def _diff_attn(bound, qd, kd, vd, g, lam_params, *, batch, seq, tq=256, heads=2):
    t = qd.shape[0]
    return pl.pallas_call(
        functools.partial(_diff_attn_kernel, tq=tq, heads=heads),
        name="diff_attn",
        grid=(batch, DIFF_HEADS // heads),
        in_specs=[_SCALAR_SPEC, _head_spec(seq, LANE, heads), _head_spec(seq, LANE, heads), _head_spec(seq, DIFF_V, heads),
                  pl.BlockSpec((1, DIFF_V), lambda b, h: (0, 0)),
                  pl.BlockSpec((4, DIFF_QK), lambda b, h: (0, 0))],
        out_specs=_head_spec(seq, DIFF_V, heads),
        out_shape=jax.ShapeDtypeStruct((t, DIFF_HEADS * DIFF_V), BF16),
        compiler_params=_params(("arbitrary", "arbitrary")),
    )(bound, qd, kd, vd, g, lam_params)


def _out_proj_kernel(x_ref, om_ref, od_ref, w_ref, o_ref):
    km = om_ref.shape[1]
    acc = jnp.dot(om_ref[...], w_ref[0:km, :], preferred_element_type=F32)
    acc = acc + jnp.dot(od_ref[...], w_ref[km:, :], preferred_element_type=F32)
    o_ref[...] = x_ref[...] + acc


def _out_proj(x2d, om, od, w, *, tm=512):
    t = x2d.shape[0]
    return pl.pallas_call(
        _out_proj_kernel,
        name="out_proj",
        grid=(t // tm,),
        in_specs=[pl.BlockSpec((tm, D_MODEL), lambda i: (i, 0)),
                  pl.BlockSpec((tm, om.shape[1]), lambda i: (i, 0)),
                  pl.BlockSpec((tm, od.shape[1]), lambda i: (i, 0)),
                  _resident(w.shape)],
        out_specs=pl.BlockSpec((tm, D_MODEL), lambda i: (i, 0)),
        out_shape=jax.ShapeDtypeStruct((t, D_MODEL), F32),
        compiler_params=_params(("arbitrary",)),
    )(x2d, om, od, w)


def _ffn_kernel(x_ref, g_ref, wg_ref, wu_ref, wd_ref, o_ref, h_ref, *, out_chunk):
    def step(first):
        if first:
            x = x_ref[...]
            h_ref[...] = ((x * _inv_rms(x, D_MODEL)) * g_ref[...]).astype(BF16)
        h = h_ref[...]
        gate = jnp.dot(h, wg_ref[...], preferred_element_type=F32)
        up = jnp.dot(h, wu_ref[...], preferred_element_type=F32)
        act = (gate * jax.nn.sigmoid(gate) * up).astype(BF16)
        for c0 in range(0, o_ref.shape[1], out_chunk):
            cs = slice(c0, c0 + out_chunk)
            part = jnp.dot(act, wd_ref[:, cs], preferred_element_type=F32)
            o_ref[:, cs] = (x_ref[:, cs] if first else o_ref[:, cs]) + part

    first_block = pl.program_id(1) == 0
    pl.when(first_block)(functools.partial(step, True))
    pl.when(jnp.logical_not(first_block))(functools.partial(step, False))


def _ffn(x1, g, wg, wu, wd, *, tm=1024, tf=512):
    t = x1.shape[0]
    return pl.pallas_call(
        functools.partial(_ffn_kernel, out_chunk=512),
        name="ffn",
        grid=(t // tm, D_FF // tf),
        in_specs=[pl.BlockSpec((tm, D_MODEL), lambda i, f: (i, 0)),
                  pl.BlockSpec((1, D_MODEL), lambda i, f: (0, 0)),
                  pl.BlockSpec((D_MODEL, tf), lambda i, f: (0, f)),
                  pl.BlockSpec((D_MODEL, tf), lambda i, f: (0, f)),
                  pl.BlockSpec((tf, D_MODEL), lambda i, f: (f, 0))],
        out_specs=pl.BlockSpec((tm, D_MODEL), lambda i, f: (i, 0)),
        out_shape=jax.ShapeDtypeStruct((t, D_MODEL), F32),
        scratch_shapes=[pltpu.VMEM((tm, D_MODEL), BF16)],
        compiler_params=_params(("arbitrary", "arbitrary")),
    )(x1, g, wg, wu, wd)


def _rope_tables(seq):
    pos = np.arange(seq, dtype=np.float64)

    def angles(r):
        freqs = 1.0 / (ROPE_THETA ** (np.arange(0, r, 2, dtype=np.float64) / r))
        return pos[:, None] * freqs[None, :]

    am = angles(MLA_ROPE)
    cos, sin = np.cos(am), np.sin(am)
    zeros = np.zeros((seq, LANE - MLA_ROPE))
    cm = np.concatenate([cos, cos, zeros], axis=-1)
    sm = np.concatenate([-sin, sin, zeros], axis=-1)

    ad = angles(DIFF_ROT)
    cos, sin = np.cos(ad), np.sin(ad)
    rest = DIFF_QK - DIFF_ROT
    cd = np.tile(np.concatenate([cos, cos, np.ones((seq, rest))], axis=-1), (1, 2))
    sd = np.tile(np.concatenate([-sin, sin, np.zeros((seq, rest))], axis=-1), (1, 2))
    return tuple(jnp.asarray(t.astype(np.float32)) for t in (cm, sm, cd, sd))


def _with_partner(v, axis=-1):
    return jnp.concatenate([v, jnp.roll(v, MLA_ROPE // 2, axis=axis)], axis=axis)


def kernel(x, attn_norm, w_in, q_latent_norm, w_q_up, kv_latent_norm, w_kv_up, mla_q_norm, mla_k_norm, mla_out_norm, diff_q_norm, diff_k_norm, lambda_q1, lambda_k1, lambda_q2, lambda_k2, diff_out_norm, w_o, ffn_norm, w_gate, w_up, w_down):
    batch, seq, d = x.shape
    assert d == D_MODEL and attn_norm.shape[0] == 1
    t = batch * seq
    x2d = x.reshape(t, d)
    l = 0

    wt = jnp.swapaxes(w_in[l], 0, 1).astype(BF16)
    wkpe = _with_partner(wt[O_KPE:O_DQ], axis=0)
    wq = w_q_up[l].astype(BF16).reshape(Q_LORA, MLA_HEADS, MLA_QK)
    wq = jnp.concatenate([wq[:, :, :MLA_NOPE].reshape(Q_LORA, -1),
                          _with_partner(wq[:, :, MLA_NOPE:]).reshape(Q_LORA, -1)], axis=1)
    wkv = w_kv_up[l].astype(BF16).reshape(KV_LORA, MLA_HEADS, MLA_NOPE + MLA_V)
    wkv = jnp.concatenate([wkv[:, :, :MLA_NOPE].reshape(KV_LORA, -1),
                           wkv[:, :, MLA_NOPE:].reshape(KV_LORA, -1)], axis=1)

    gq, gk = mla_q_norm[l], mla_k_norm[l]
    lane_gains = jnp.stack([gq[:MLA_NOPE], _with_partner(gq[MLA_NOPE:]), gk[:MLA_NOPE], _with_partner(gk[MLA_NOPE:]),
                            jnp.tile(diff_q_norm[l], 2), jnp.tile(diff_k_norm[l], 2),
                            jnp.zeros((LANE,), F32), jnp.zeros((LANE,), F32)])
    bounds = jnp.stack([_score_bound(gq, gk, MLA_QK), _score_bound(diff_q_norm[l], diff_k_norm[l], DIFF_QK)])
    bounds = bounds.astype(F32).reshape(1, 2)
    lam_params = jnp.concatenate([lambda_q1[l:l + 1], lambda_k1[l:l + 1],
                                  lambda_q2[l:l + 1], lambda_k2[l:l + 1]], axis=0)
    cm, sm, cd, sd = _rope_tables(seq)

    (qm, km, vm, qd, kd, vd), (wo16, wg16, wu16, wd16) = _qkv(
        x2d, attn_norm[l:l + 1], wt, wkpe, q_latent_norm[l:l + 1], kv_latent_norm[l:l + 1], wq, wkv,
        lane_gains, cm, sm, cd, sd, (w_o[l], w_gate[l], w_up[l], w_down[l]), seq=seq)
    o_mla = _mla_attn(bounds, qm, km, vm, mla_out_norm[l:l + 1], batch=batch, seq=seq)
    o_diff = _diff_attn(bounds, qd, kd, vd, diff_out_norm[l:l + 1], lam_params, batch=batch, seq=seq)
    x1 = _out_proj(x2d, o_mla, o_diff, wo16)
    out = _ffn(x1, ffn_norm[l:l + 1], wg16, wu16, wd16)
    return out.reshape(batch, seq, d)
```

```python
import functools
import math

import jax
import jax.numpy as jnp
import numpy as np
from jax import lax
from jax.experimental import pallas as pl
from jax.experimental.pallas import tpu as pltpu

F32 = jnp.float32
BF16 = jnp.bfloat16

D_MODEL = 2048
MLA_HEADS = 8
MLA_NOPE = 128
MLA_ROPE = 64
MLA_QK = MLA_NOPE + MLA_ROPE
MLA_V = 128
Q_LORA = 512
KV_LORA = 512
DIFF_HEADS = 8
DIFF_QK = 64
DIFF_ROT = 16
DIFF_V = 128
ROPE_THETA = 500000.0
D_FF = 5632
EPS = 1e-6
LAMBDA_INIT = 0.8 - 0.6 * math.exp(-0.3 * 0)
LOG2E = math.log2(math.e)

SMALL_SCORE_BOUND = 32.0
LANE = 128
HEAD_W = 2 * LANE
VMEM_LIMIT = 56 * 1024 * 1024

O_CKV = Q_LORA
O_KPE = O_CKV + KV_LORA
O_DQ = O_KPE + MLA_ROPE
O_DK = O_DQ + DIFF_HEADS * 2 * DIFF_QK
O_DV = O_DK + DIFF_HEADS * 2 * DIFF_QK
IN_COLS = O_DV + DIFF_HEADS * DIFF_V

_NT = (((1,), (1,)), ((), ()))


def _params(sem):
    return pltpu.CompilerParams(dimension_semantics=sem, vmem_limit_bytes=VMEM_LIMIT)


def _resident(shape):
    return pl.BlockSpec(shape, lambda *_: (0,) * len(shape), pipeline_mode=pl.Buffered(1))


def _inv_rms(xf, n):
    return lax.rsqrt(jnp.sum(xf * xf, axis=-1, keepdims=True) * (1.0 / n) + EPS)


def _dot_nt(a, b):
    return lax.dot_general(a, b, _NT, preferred_element_type=F32)


def _qkv_kernel(x_ref, ga_ref, wt_ref, wkpe_ref, glq_ref, glkv_ref, wq_ref, wkv_ref,
                lg_ref,
                cm_ref, sm_ref, cd_ref, sd_ref,
                wo_ref, wg_ref, wu_ref, wd_ref,
                qm_ref, km_ref, vm_ref, qd_ref, kd_ref, vd_ref,
                wo16_ref, wg16_ref, wu16_ref, wd16_ref):
    def cast_slab(src, dst):
        dst[...] = src[...].astype(BF16)

    tm = x_ref.shape[0]
    cm, sm, cd, sd = cm_ref[...], sm_ref[...], cd_ref[...], sd_ref[...]
    x = x_ref[...]
    h = ((x * _inv_rms(x, D_MODEL)) * ga_ref[...]).astype(BF16)

    lane = lax.broadcasted_iota(jnp.int32, (tm, LANE), 1)
    lo = lane < DIFF_QK
    is_x1 = (lane % DIFF_QK) < (DIFF_ROT // 2)

    def prep_diff(xh, g, scale):
        sq = xh * xh
        s_lo = jnp.sum(jnp.where(lo, sq, 0.0), axis=-1, keepdims=True)
        s_hi = jnp.sum(jnp.where(lo, 0.0, sq), axis=-1, keepdims=True)
        inv = lax.rsqrt(jnp.where(lo, s_lo, s_hi) * (1.0 / DIFF_QK) + EPS)
        y = (xh * inv) * g
        partner = jnp.where(is_x1, pltpu.roll(y, LANE - DIFF_ROT // 2, 1), pltpu.roll(y, DIFF_ROT // 2, 1))
        out = y * cd + partner * sd
        return out * scale if scale != 1.0 else out

    def rope_mla(y):
        return y * cm + pltpu.roll(y, MLA_ROPE, 1) * sm

    half = DIFF_HEADS // 2 * LANE
    lg = lg_ref[...]
    gqn, gqr, gkn, gkr, gdq, gdk = (lg[r:r + 1, :] for r in range(6))
    diff_scale = LOG2E / math.sqrt(DIFF_QK)

    def diff_heads(src, g, scale, dst_ref, c0):
        for j in range(DIFF_HEADS // 2):
            dst_ref[:, c0 + j * LANE:c0 + (j + 1) * LANE] = prep_diff(
                src[:, j * LANE:(j + 1) * LANE], g, scale).astype(BF16)

    mla_scale = LOG2E / math.sqrt(MLA_QK)
    nope_w = MLA_HEADS * LANE

    def q_heads(q):
        for hd in range(MLA_HEADS):
            qn = q[:, hd * LANE:(hd + 1) * LANE]
            qr = q[:, nope_w + hd * LANE:nope_w + (hd + 1) * LANE]
            ss = jnp.sum(qn * qn + 0.5 * (qr * qr), axis=-1, keepdims=True)
            sq = lax.rsqrt(ss * (1.0 / MLA_QK) + EPS)
            qm_ref[:, hd * HEAD_W:hd * HEAD_W + LANE] = (((qn * sq) * gqn) * mla_scale).astype(BF16)
            qm_ref[:, hd * HEAD_W + LANE:(hd + 1) * HEAD_W] = (
                rope_mla((qr * sq) * gqr) * mla_scale).astype(BF16)

    def k_heads(kv, kpe):
        kpe_sq = 0.5 * (kpe * kpe)
        kpe_rot = rope_mla(kpe * gkr)
        for hd in range(MLA_HEADS):
            kn = kv[:, hd * LANE:(hd + 1) * LANE]
            ssk = jnp.sum(kn * kn + kpe_sq, axis=-1, keepdims=True)
            sk = lax.rsqrt(ssk * (1.0 / MLA_QK) + EPS)
            km_ref[:, hd * HEAD_W:hd * HEAD_W + LANE] = ((kn * sk) * gkn).astype(BF16)
            km_ref[:, hd * HEAD_W + LANE:(hd + 1) * HEAD_W] = (kpe_rot * sk).astype(BF16)
            vm_ref[:, hd * HEAD_W:hd * HEAD_W + LANE] = (
                kv[:, nope_w + hd * LANE:nope_w + (hd + 1) * LANE].astype(BF16))
            vm_ref[:, hd * HEAD_W + LANE:(hd + 1) * HEAD_W] = jnp.ones((tm, LANE), BF16)

    cq = _dot_nt(h, wt_ref[0:O_CKV, :])
    ckv = _dot_nt(h, wt_ref[O_CKV:O_KPE, :])
    kpe = _dot_nt(h, wkpe_ref[...])
    cast_slab(wg_ref, wg16_ref)
    hq = ((cq * _inv_rms(cq, Q_LORA)) * glq_ref[...]).astype(BF16)
    q = jnp.dot(hq, wq_ref[...], preferred_element_type=F32)
    dq0 = _dot_nt(h, wt_ref[O_DQ:O_DQ + half, :])
    hkv = ((ckv * _inv_rms(ckv, KV_LORA)) * glkv_ref[...]).astype(BF16)
    kv = jnp.dot(hkv, wkv_ref[...], preferred_element_type=F32)
    dq1 = _dot_nt(h, wt_ref[O_DQ + half:O_DK, :])
    cast_slab(wu_ref, wu16_ref)
    q_heads(q)
    dk0 = _dot_nt(h, wt_ref[O_DK:O_DK + half, :])
    k_heads(kv, kpe)
    dk1 = _dot_nt(h, wt_ref[O_DK + half:O_DV, :])
    cast_slab(wd_ref, wd16_ref)
    diff_heads(dq0, gdq, diff_scale, qd_ref, 0)
    vd_ref[:, 0:half] = _dot_nt(h, wt_ref[O_DV:O_DV + half, :]).astype(BF16)
    diff_heads(dq1, gdq, diff_scale, qd_ref, half)
    cast_slab(wo_ref, wo16_ref)
    diff_heads(dk0, gdk, 1.0, kd_ref, 0)
    vd_ref[:, half:] = _dot_nt(h, wt_ref[O_DV + half:IN_COLS, :]).astype(BF16)
    diff_heads(dk1, gdk, 1.0, kd_ref, half)


def _qkv(x2d, ga, wt, wkpe, glq, glkv, wq, wkv, lane_gains, cm, sm, cd, sd,
         later_weights, *, seq, tm=256):
    t = x2d.shape[0]
    steps = t // tm
    spt = seq // tm
    tab = pl.BlockSpec((tm, LANE), lambda i: (i % spt, 0))
    vec = lambda n: _resident((1, n))
    widths = [MLA_HEADS * HEAD_W, MLA_HEADS * HEAD_W, MLA_HEADS * HEAD_W,
              DIFF_HEADS * LANE, DIFF_HEADS * LANE, DIFF_HEADS * DIFF_V]
    slabs = [pl.BlockSpec((w.shape[0] // steps, w.shape[1]), lambda i: (i, 0)) for w in later_weights]
    assert all(w.shape[0] % (16 * steps) == 0 for w in later_weights)
    outs = pl.pallas_call(
        _qkv_kernel,
        name="qkv",
        grid=(steps,),
        in_specs=[pl.BlockSpec((tm, D_MODEL), lambda i: (i, 0)), vec(D_MODEL),
                  _resident(wt.shape), _resident(wkpe.shape),
                  vec(Q_LORA), vec(KV_LORA), _resident(wq.shape), _resident(wkv.shape),
                  _resident(lane_gains.shape),
                  tab, tab, tab, tab] + slabs,
        out_specs=[pl.BlockSpec((tm, w), lambda i: (i, 0)) for w in widths] + slabs,
        out_shape=([jax.ShapeDtypeStruct((t, w), BF16) for w in widths]
                   + [jax.ShapeDtypeStruct(w.shape, BF16) for w in later_weights]),
        compiler_params=_params(("arbitrary",)),
    )(x2d, ga, wt, wkpe, glq, glkv, wq, wkv, lane_gains, cm, sm, cd, sd, *later_weights)
    return outs[:len(widths)], outs[len(widths):]


def _causal_keep(tq):
    row = lax.broadcasted_iota(jnp.int32, (tq, tq), 0)
    col = lax.broadcasted_iota(jnp.int32, (tq, tq), 1)
    return col <= row


def _scores(q, k_rows, r0, tq, keep):
    s_d = _dot_nt(q, k_rows(r0, r0 + tq))
    s_d = jnp.where(keep, s_d, -jnp.inf)
    s_o = None
    if r0 > 0:
        s_o = _dot_nt(q, k_rows(0, r0))
    return s_o, s_d


def _softmax_pieces(s, *, with_sum, subtract_max):
    s_o, s_d = s
    if subtract_max:
        m = jnp.max(s_d, axis=-1, keepdims=True)
        if s_o is not None:
            m = jnp.maximum(m, jnp.max(s_o, axis=-1, keepdims=True))
            s_o = s_o - m
        s_d = s_d - m
    p_d = jnp.exp2(s_d)
    p_o = None if s_o is None else jnp.exp2(s_o)
    if not with_sum:
        return p_o, p_d, None
    l = jnp.sum(p_d, axis=-1, keepdims=True)
    if p_o is not None:
        l = jnp.sum(p_o, axis=-1, keepdims=True) + l
    return p_o, p_d, l


def _pv(p_o, p_d, v_rows, r0, tq):
    o = jnp.dot(p_d.astype(BF16), v_rows(r0, r0 + tq), preferred_element_type=F32)
    if p_o is not None:
        o = o + jnp.dot(p_o.astype(BF16), v_rows(0, r0), preferred_element_type=F32)
    return o


def _work_items(seq, tq, heads, descending):
    starts = list(range(0, seq, tq))
    if descending:
        starts = starts[::-1]
    return [(j, r0) for j in range(heads) for r0 in starts]


def _cols(ref, j, width):
    return lambda a, b: ref[a:b, j * width:(j + 1) * width]


def _pipelined(items, scores, finish):
    s_next = scores(items[0])
    for k, item in enumerate(items):
        s = s_next
        if k + 1 < len(items):
            s_next = scores(items[k + 1])
        finish(item, s)


def _either_softmax(bound, body):
    small = bound <= SMALL_SCORE_BOUND
    pl.when(small)(functools.partial(body, subtract_max=False))
    pl.when(jnp.logical_not(small))(functools.partial(body, subtract_max=True))


def _mla_attn_kernel(bound_ref, q_ref, k_ref, v_ref, g_ref, o_ref, *, tq, heads):
    seq = q_ref.shape[0]

    def body(subtract_max):
        keep = _causal_keep(tq)
        g = g_ref[...]

        def scores(item):
            j, r0 = item
            q = _cols(q_ref, j, HEAD_W)(r0, r0 + tq)
            kt = lambda a, b: k_ref[j * HEAD_W:(j + 1) * HEAD_W, a:b]
            s_d = jnp.where(keep, jnp.dot(q, kt(r0, r0 + tq), preferred_element_type=F32), -jnp.inf)
            s_o = jnp.dot(q, kt(0, r0), preferred_element_type=F32) if r0 > 0 else None
            return s_o, s_d

        def finish(item, s):
            j, r0 = item
            p_o, p_d, _ = _softmax_pieces(s, with_sum=False, subtract_max=subtract_max)
            ol = _pv(p_o, p_d, _cols(v_ref, j, HEAD_W), r0, tq)
            o = ol[:, :MLA_V] / ol[:, MLA_V:]
            o_ref[r0:r0 + tq, j * MLA_V:(j + 1) * MLA_V] = ((o * _inv_rms(o, MLA_V)) * g).astype(BF16)

        _pipelined(_work_items(seq, tq, heads, descending=True), scores, finish)

    _either_softmax(bound_ref[0, 0], body)


def _diff_attn_kernel(bound_ref, q_ref, k_ref, v_ref, g_ref, lam_ref, o_ref, *, tq, heads):
    _either_softmax(bound_ref[0, 1], functools.partial(_diff_attn_body, q_ref, k_ref, v_ref, g_ref, lam_ref, o_ref,
                                                 tq=tq, heads=heads))


def _diff_attn_body(q_ref, k_ref, v_ref, g_ref, lam_ref, o_ref, *, tq, heads, subtract_max):
    seq = q_ref.shape[0]
    keep = _causal_keep(tq)
    g = g_ref[...]
    lp = lam_ref[...]
    lam = (jnp.exp(jnp.sum(lp[0:1, :] * lp[1:2, :], axis=-1, keepdims=True))
           - jnp.exp(jnp.sum(lp[2:3, :] * lp[3:4, :], axis=-1, keepdims=True))
           + LAMBDA_INIT)
    lane = lax.broadcasted_iota(jnp.int32, (tq, LANE), 1)
    first = lane < DIFF_QK

    def scores(item):
        j, r0 = item
        q = _cols(q_ref, j, LANE)(r0, r0 + tq)
        k_rows = _cols(k_ref, j, LANE)
        q1 = jnp.where(first, q, jnp.zeros_like(q))
        q2 = jnp.where(first, jnp.zeros_like(q), q)
        return _scores(q1, k_rows, r0, tq, keep), _scores(q2, k_rows, r0, tq, keep)

    def finish(item, s):
        j, r0 = item
        p1_o, p1_d, l1 = _softmax_pieces(s[0], with_sum=True, subtract_max=subtract_max)
        p2_o, p2_d, l2 = _softmax_pieces(s[1], with_sum=True, subtract_max=subtract_max)
        c = lam * (l1 / l2)
        a_o = None if p1_o is None else p1_o - c * p2_o
        o = _pv(a_o, p1_d - c * p2_d, _cols(v_ref, j, DIFF_V), r0, tq) / l1
        o_ref[r0:r0 + tq, j * DIFF_V:(j + 1) * DIFF_V] = (
            ((o * _inv_rms(o, DIFF_V)) * g) * (1.0 - LAMBDA_INIT)).astype(BF16)

    _pipelined(_work_items(seq, tq, heads, descending=not subtract_max), scores, finish)


def _head_spec(seq, width, heads):
    return pl.BlockSpec((seq, heads * width), lambda b, h: (b, h))


_SCALAR_SPEC = pl.BlockSpec(memory_space=pltpu.SMEM)


def _score_bound(gq, gk, d):
    return 1.01 * LOG2E * math.sqrt(d) * jnp.max(jnp.abs(gq)) * jnp.max(jnp.abs(gk))


def _mla_attn(bound, qm, km, vm, g, *, batch, seq, tq=256, heads=2):
    t = qm.shape[0]
    return pl.pallas_call(
        functools.partial(_mla_attn_kernel, tq=tq, heads=heads),
        name="mla_attn",
        grid=(batch, MLA_HEADS // heads),
        in_specs=[_SCALAR_SPEC, _head_spec(seq, HEAD_W, heads),
                  pl.BlockSpec((heads * HEAD_W, seq), lambda b, h: (h, b)), _head_spec(seq, HEAD_W, heads),
                  pl.BlockSpec((1, MLA_V), lambda b, h: (0, 0))],
        out_specs=_head_spec(seq, MLA_V, heads),
        out_shape=jax.ShapeDtypeStruct((t, MLA_HEADS * MLA_V), BF16),
        compiler_params=_params(("arbitrary", "arbitrary")),
    )(bound, qm, km, vm, g)


def _diff_attn(bound, qd, kd, vd, g, lam_params, *, batch, seq, tq=256, heads=2):
    t = qd.shape[0]
    return pl.pallas_call(
        functools.partial(_diff_attn_kernel, tq=tq, heads=heads),
        name="diff_attn",
        grid=(batch, DIFF_HEADS // heads),
        in_specs=[_SCALAR_SPEC, _head_spec(seq, LANE, heads), _head_spec(seq, LANE, heads), _head_spec(seq, DIFF_V, heads),
                  pl.BlockSpec((1, DIFF_V), lambda b, h: (0, 0)),
                  pl.BlockSpec((4, DIFF_QK), lambda b, h: (0, 0))],
        out_specs=_head_spec(seq, DIFF_V, heads),
        out_shape=jax.ShapeDtypeStruct((t, DIFF_HEADS * DIFF_V), BF16),
        compiler_params=_params(("arbitrary", "arbitrary")),
    )(bound, qd, kd, vd, g, lam_params)


def _out_proj_kernel(x_ref, om_ref, od_ref, w_ref, o_ref):
    km = om_ref.shape[1]
    acc = jnp.dot(om_ref[...], w_ref[0:km, :], preferred_element_type=F32)
    acc = acc + jnp.dot(od_ref[...], w_ref[km:, :], preferred_element_type=F32)
    o_ref[...] = x_ref[...] + acc


def _out_proj(x2d, om, od, w, *, tm=512):
    t = x2d.shape[0]
    return pl.pallas_call(
        _out_proj_kernel,
        name="out_proj",
        grid=(t // tm,),
        in_specs=[pl.BlockSpec((tm, D_MODEL), lambda i: (i, 0)),
                  pl.BlockSpec((tm, om.shape[1]), lambda i: (i, 0)),
                  pl.BlockSpec((tm, od.shape[1]), lambda i: (i, 0)),
                  _resident(w.shape)],
        out_specs=pl.BlockSpec((tm, D_MODEL), lambda i: (i, 0)),
        out_shape=jax.ShapeDtypeStruct((t, D_MODEL), F32),
        compiler_params=_params(("arbitrary",)),
    )(x2d, om, od, w)


def _ffn_kernel(x_ref, g_ref, wg_ref, wu_ref, wd_ref, o_ref, h_ref, *, out_chunk):
    def step(first):
        if first:
            x = x_ref[...]
            h_ref[...] = ((x * _inv_rms(x, D_MODEL)) * g_ref[...]).astype(BF16)
        h = h_ref[...]
        gate = jnp.dot(h, wg_ref[...], preferred_element_type=F32)
        up = jnp.dot(h, wu_ref[...], preferred_element_type=F32)
        act = (gate * jax.nn.sigmoid(gate) * up).astype(BF16)
        for c0 in range(0, o_ref.shape[1], out_chunk):
            cs = slice(c0, c0 + out_chunk)
            part = jnp.dot(act, wd_ref[:, cs], preferred_element_type=F32)
            o_ref[:, cs] = (x_ref[:, cs] if first else o_ref[:, cs]) + part

    first_block = pl.program_id(1) == 0
    pl.when(first_block)(functools.partial(step, True))
    pl.when(jnp.logical_not(first_block))(functools.partial(step, False))


def _ffn(x1, g, wg, wu, wd, *, tm=1024, tf=512):
    t = x1.shape[0]
    return pl.pallas_call(
        functools.partial(_ffn_kernel, out_chunk=512),
        name="ffn",
        grid=(t // tm, D_FF // tf),
        in_specs=[pl.BlockSpec((tm, D_MODEL), lambda i, f: (i, 0)),
                  pl.BlockSpec((1, D_MODEL), lambda i, f: (0, 0)),
                  pl.BlockSpec((D_MODEL, tf), lambda i, f: (0, f)),
                  pl.BlockSpec((D_MODEL, tf), lambda i, f: (0, f)),
                  pl.BlockSpec((tf, D_MODEL), lambda i, f: (f, 0))],
        out_specs=pl.BlockSpec((tm, D_MODEL), lambda i, f: (i, 0)),
        out_shape=jax.ShapeDtypeStruct((t, D_MODEL), F32),
        scratch_shapes=[pltpu.VMEM((tm, D_MODEL), BF16)],
        compiler_params=_params(("arbitrary", "arbitrary")),
    )(x1, g, wg, wu, wd)


def _rope_tables(seq):
    pos = np.arange(seq, dtype=np.float64)

    def angles(r):
        freqs = 1.0 / (ROPE_THETA ** (np.arange(0, r, 2, dtype=np.float64) / r))
        return pos[:, None] * freqs[None, :]

    am = angles(MLA_ROPE)
    cos, sin = np.cos(am), np.sin(am)
    zeros = np.zeros((seq, LANE - MLA_ROPE))
    cm = np.concatenate([cos, cos, zeros], axis=-1)
    sm = np.concatenate([-sin, sin, zeros], axis=-1)

    ad = angles(DIFF_ROT)
    cos, sin = np.cos(ad), np.sin(ad)
    rest = DIFF_QK - DIFF_ROT
    cd = np.tile(np.concatenate([cos, cos, np.ones((seq, rest))], axis=-1), (1, 2))
    sd = np.tile(np.concatenate([-sin, sin, np.zeros((seq, rest))], axis=-1), (1, 2))
    return tuple(jnp.asarray(t.astype(np.float32)) for t in (cm, sm, cd, sd))


def _with_partner(v, axis=-1):
    return jnp.concatenate([v, jnp.roll(v, MLA_ROPE // 2, axis=axis)], axis=axis)


def kernel(x, attn_norm, w_in, q_latent_norm, w_q_up, kv_latent_norm, w_kv_up, mla_q_norm, mla_k_norm, mla_out_norm, diff_q_norm, diff_k_norm, lambda_q1, lambda_k1, lambda_q2, lambda_k2, diff_out_norm, w_o, ffn_norm, w_gate, w_up, w_down):
    batch, seq, d = x.shape
    assert d == D_MODEL and attn_norm.shape[0] == 1
    t = batch * seq
    x2d = x.reshape(t, d)
    l = 0

    wt = jnp.swapaxes(w_in[l], 0, 1).astype(BF16)
    wkpe = _with_partner(wt[O_KPE:O_DQ], axis=0)
    wq = w_q_up[l].astype(BF16).reshape(Q_LORA, MLA_HEADS, MLA_QK)
    wq = jnp.concatenate([wq[:, :, :MLA_NOPE].reshape(Q_LORA, -1),
                          _with_partner(wq[:, :, MLA_NOPE:]).reshape(Q_LORA, -1)], axis=1)
    wkv = w_kv_up[l].astype(BF16).reshape(KV_LORA, MLA_HEADS, MLA_NOPE + MLA_V)
    wkv = jnp.concatenate([wkv[:, :, :MLA_NOPE].reshape(KV_LORA, -1),
                           wkv[:, :, MLA_NOPE:].reshape(KV_LORA, -1)], axis=1)

    gq, gk = mla_q_norm[l], mla_k_norm[l]
    lane_gains = jnp.stack([gq[:MLA_NOPE], _with_partner(gq[MLA_NOPE:]), gk[:MLA_NOPE], _with_partner(gk[MLA_NOPE:]),
                            jnp.tile(diff_q_norm[l], 2), jnp.tile(diff_k_norm[l], 2),
                            jnp.zeros((LANE,), F32), jnp.zeros((LANE,), F32)])
    bounds = jnp.stack([_score_bound(gq, gk, MLA_QK), _score_bound(diff_q_norm[l], diff_k_norm[l], DIFF_QK)])
    bounds = bounds.astype(F32).reshape(1, 2)
    lam_params = jnp.concatenate([lambda_q1[l:l + 1], lambda_k1[l:l + 1],
                                  lambda_q2[l:l + 1], lambda_k2[l:l + 1]], axis=0)
    cm, sm, cd, sd = _rope_tables(seq)

    (qm, km, vm, qd, kd, vd), (wo16, wg16, wu16, wd16) = _qkv(
        x2d, attn_norm[l:l + 1], wt, wkpe, q_latent_norm[l:l + 1], kv_latent_norm[l:l + 1], wq, wkv,
        lane_gains, cm, sm, cd, sd, (w_o[l], w_gate[l], w_up[l], w_down[l]), seq=seq)
    o_mla = _mla_attn(bounds, qm, km.T, vm, mla_out_norm[l:l + 1], batch=batch, seq=seq)
    o_diff = _diff_attn(bounds, qd, kd, vd, diff_out_norm[l:l + 1], lam_params, batch=batch, seq=seq)
    x1 = _out_proj(x2d, o_mla, o_diff, wo16)
    out = _ffn(x1, ffn_norm[l:l + 1], wg16, wu16, wd16)
    return out.reshape(batch, seq, d)
```

```python
import functools
import math

import jax
import jax.numpy as jnp
import numpy as np
from jax import lax
from jax.experimental import pallas as pl
from jax.experimental.pallas import tpu as pltpu

F32 = jnp.float32
BF16 = jnp.bfloat16

D_MODEL = 2048
MLA_HEADS = 8
MLA_NOPE = 128
MLA_ROPE = 64
MLA_QK = MLA_NOPE + MLA_ROPE
MLA_V = 128
Q_LORA = 512
KV_LORA = 512
DIFF_HEADS = 8
DIFF_QK = 64
DIFF_ROT = 16
DIFF_V = 128
ROPE_THETA = 500000.0
D_FF = 5632
EPS = 1e-6
LAMBDA_INIT = 0.8 - 0.6 * math.exp(-0.3 * 0)
LOG2E = math.log2(math.e)

SMALL_SCORE_BOUND = 32.0
LANE = 128
HEAD_W = 2 * LANE
VMEM_LIMIT = 56 * 1024 * 1024

O_CKV = Q_LORA
O_KPE = O_CKV + KV_LORA
O_DQ = O_KPE + MLA_ROPE
O_DK = O_DQ + DIFF_HEADS * 2 * DIFF_QK
O_DV = O_DK + DIFF_HEADS * 2 * DIFF_QK
IN_COLS = O_DV + DIFF_HEADS * DIFF_V

_NT = (((1,), (1,)), ((), ()))


def _params(sem):
    return pltpu.CompilerParams(dimension_semantics=sem, vmem_limit_bytes=VMEM_LIMIT)


def _resident(shape):
    return pl.BlockSpec(shape, lambda *_: (0,) * len(shape), pipeline_mode=pl.Buffered(1))


def _inv_rms(xf, n):
    return lax.rsqrt(jnp.sum(xf * xf, axis=-1, keepdims=True) * (1.0 / n) + EPS)


def _dot_nt(a, b):
    return lax.dot_general(a, b, _NT, preferred_element_type=F32)


def _qkv_kernel(x_ref, ga_ref, wt_ref, wkpe_ref, glq_ref, glkv_ref, wq_ref, wkv_ref,
                lg_ref,
                cm_ref, sm_ref, cd_ref, sd_ref,
                wo_ref, wg_ref, wu_ref, wd_ref,
                qm_ref, km_ref, vm_ref, qd_ref, kd_ref, vd_ref,
                wo16_ref, wg16_ref, wu16_ref, wd16_ref):
    def cast_slab(src, dst):
        dst[...] = src[...].astype(BF16)

    tm = x_ref.shape[0]
    cm, sm, cd, sd = cm_ref[...], sm_ref[...], cd_ref[...], sd_ref[...]
    x = x_ref[...]
    h = ((x * _inv_rms(x, D_MODEL)) * ga_ref[...]).astype(BF16)

    lane = lax.broadcasted_iota(jnp.int32, (tm, LANE), 1)
    lo = lane < DIFF_QK
    is_x1 = (lane % DIFF_QK) < (DIFF_ROT // 2)

    def prep_diff(xh, g, scale):
        sq = xh * xh
        s_lo = jnp.sum(jnp.where(lo, sq, 0.0), axis=-1, keepdims=True)
        s_hi = jnp.sum(jnp.where(lo, 0.0, sq), axis=-1, keepdims=True)
        inv = lax.rsqrt(jnp.where(lo, s_lo, s_hi) * (1.0 / DIFF_QK) + EPS)
        y = (xh * inv) * g
        partner = jnp.where(is_x1, pltpu.roll(y, LANE - DIFF_ROT // 2, 1), pltpu.roll(y, DIFF_ROT // 2, 1))
        out = y * cd + partner * sd
        return out * scale if scale != 1.0 else out

    def rope_mla(y):
        return y * cm + pltpu.roll(y, MLA_ROPE, 1) * sm

    half = DIFF_HEADS // 2 * LANE
    lg = lg_ref[...]
    gqn, gqr, gkn, gkr, gdq, gdk = (lg[r:r + 1, :] for r in range(6))
    diff_scale = LOG2E / math.sqrt(DIFF_QK)

    def diff_heads(src, g, scale, dst_ref, c0):
        for j in range(DIFF_HEADS // 2):
            dst_ref[:, c0 + j * LANE:c0 + (j + 1) * LANE] = prep_diff(
                src[:, j * LANE:(j + 1) * LANE], g, scale).astype(BF16)

    mla_scale = LOG2E / math.sqrt(MLA_QK)
    nope_w = MLA_HEADS * LANE

    def q_heads(q):
        for hd in range(MLA_HEADS):
            qn = q[:, hd * LANE:(hd + 1) * LANE]
            qr = q[:, nope_w + hd * LANE:nope_w + (hd + 1) * LANE]
            ss = jnp.sum(qn * qn + 0.5 * (qr * qr), axis=-1, keepdims=True)
            sq = lax.rsqrt(ss * (1.0 / MLA_QK) + EPS)
            qm_ref[:, hd * HEAD_W:hd * HEAD_W + LANE] = (((qn * sq) * gqn) * mla_scale).astype(BF16)
            qm_ref[:, hd * HEAD_W + LANE:(hd + 1) * HEAD_W] = (
                rope_mla((qr * sq) * gqr) * mla_scale).astype(BF16)

    def k_heads(kv, kpe):
        kpe_sq = 0.5 * (kpe * kpe)
        kpe_rot = rope_mla(kpe * gkr)
        for hd in range(MLA_HEADS):
            kn = kv[:, hd * LANE:(hd + 1) * LANE]
            ssk = jnp.sum(kn * kn + kpe_sq, axis=-1, keepdims=True)
            sk = lax.rsqrt(ssk * (1.0 / MLA_QK) + EPS)
            km_ref[:, hd * HEAD_W:hd * HEAD_W + LANE] = ((kn * sk) * gkn).astype(BF16)
            km_ref[:, hd * HEAD_W + LANE:(hd + 1) * HEAD_W] = (kpe_rot * sk).astype(BF16)
            vm_ref[:, hd * HEAD_W:hd * HEAD_W + LANE] = (
                kv[:, nope_w + hd * LANE:nope_w + (hd + 1) * LANE].astype(BF16))
            vm_ref[:, hd * HEAD_W + LANE:(hd + 1) * HEAD_W] = jnp.ones((tm, LANE), BF16)

    cq = _dot_nt(h, wt_ref[0:O_CKV, :])
    ckv = _dot_nt(h, wt_ref[O_CKV:O_KPE, :])
    kpe = _dot_nt(h, wkpe_ref[...])
    cast_slab(wg_ref, wg16_ref)
    hq = ((cq * _inv_rms(cq, Q_LORA)) * glq_ref[...]).astype(BF16)
    q = jnp.dot(hq, wq_ref[...], preferred_element_type=F32)
    dq0 = _dot_nt(h, wt_ref[O_DQ:O_DQ + half, :])
    hkv = ((ckv * _inv_rms(ckv, KV_LORA)) * glkv_ref[...]).astype(BF16)
    kv = jnp.dot(hkv, wkv_ref[...], preferred_element_type=F32)
    dq1 = _dot_nt(h, wt_ref[O_DQ + half:O_DK, :])
    cast_slab(wu_ref, wu16_ref)
    q_heads(q)
    dk0 = _dot_nt(h, wt_ref[O_DK:O_DK + half, :])
    k_heads(kv, kpe)
    dk1 = _dot_nt(h, wt_ref[O_DK + half:O_DV, :])
    cast_slab(wd_ref, wd16_ref)
    diff_heads(dq0, gdq, diff_scale, qd_ref, 0)
    vd_ref[:, 0:half] = _dot_nt(h, wt_ref[O_DV:O_DV + half, :]).astype(BF16)
    diff_heads(dq1, gdq, diff_scale, qd_ref, half)
    cast_slab(wo_ref, wo16_ref)
    diff_heads(dk0, gdk, 1.0, kd_ref, 0)
    vd_ref[:, half:] = _dot_nt(h, wt_ref[O_DV + half:IN_COLS, :]).astype(BF16)
    diff_heads(dk1, gdk, 1.0, kd_ref, half)


def _qkv(x2d, ga, wt, wkpe, glq, glkv, wq, wkv, lane_gains, cm, sm, cd, sd,
         later_weights, *, seq, tm=256):
    t = x2d.shape[0]
    steps = t // tm
    spt = seq // tm
    tab = pl.BlockSpec((tm, LANE), lambda i: (i % spt, 0))
    vec = lambda n: _resident((1, n))
    widths = [MLA_HEADS * HEAD_W, MLA_HEADS * HEAD_W, MLA_HEADS * HEAD_W,
              DIFF_HEADS * LANE, DIFF_HEADS * LANE, DIFF_HEADS * DIFF_V]
    slabs = [pl.BlockSpec((w.shape[0] // steps, w.shape[1]), lambda i: (i, 0)) for w in later_weights]
    assert all(w.shape[0] % (16 * steps) == 0 for w in later_weights)
    outs = pl.pallas_call(
        _qkv_kernel,
        name="qkv",
        grid=(steps,),
        in_specs=[pl.BlockSpec((tm, D_MODEL), lambda i: (i, 0)), vec(D_MODEL),
                  _resident(wt.shape), _resident(wkpe.shape),
                  vec(Q_LORA), vec(KV_LORA), _resident(wq.shape), _resident(wkv.shape),
                  _resident(lane_gains.shape),
                  tab, tab, tab, tab] + slabs,
        out_specs=[pl.BlockSpec((tm, w), lambda i: (i, 0)) for w in widths] + slabs,
        out_shape=([jax.ShapeDtypeStruct((t, w), BF16) for w in widths]
                   + [jax.ShapeDtypeStruct(w.shape, BF16) for w in later_weights]),
        compiler_params=_params(("arbitrary",)),
    )(x2d, ga, wt, wkpe, glq, glkv, wq, wkv, lane_gains, cm, sm, cd, sd, *later_weights)
    return outs[:len(widths)], outs[len(widths):]


def _causal_keep(tq):
    row = lax.broadcasted_iota(jnp.int32, (tq, tq), 0)
    col = lax.broadcasted_iota(jnp.int32, (tq, tq), 1)
    return col <= row


def _scores(q, k_rows, r0, tq, keep):
    s_d = _dot_nt(q, k_rows(r0, r0 + tq))
    s_d = jnp.where(keep, s_d, -jnp.inf)
    s_o = None
    if r0 > 0:
        s_o = _dot_nt(q, k_rows(0, r0))
    return s_o, s_d


def _softmax_pieces(s, *, with_sum, subtract_max):
    s_o, s_d = s
    if subtract_max:
        m = jnp.max(s_d, axis=-1, keepdims=True)
        if s_o is not None:
            m = jnp.maximum(m, jnp.max(s_o, axis=-1, keepdims=True))
            s_o = s_o - m
        s_d = s_d - m
    p_d = jnp.exp2(s_d)
    p_o = None if s_o is None else jnp.exp2(s_o)
    if not with_sum:
        return p_o, p_d, None
    l = jnp.sum(p_d, axis=-1, keepdims=True)
    if p_o is not None:
        l = jnp.sum(p_o, axis=-1, keepdims=True) + l
    return p_o, p_d, l


def _pv(p_o, p_d, v_rows, r0, tq):
    o = jnp.dot(p_d.astype(BF16), v_rows(r0, r0 + tq), preferred_element_type=F32)
    if p_o is not None:
        o = o + jnp.dot(p_o.astype(BF16), v_rows(0, r0), preferred_element_type=F32)
    return o


def _work_items(seq, tq, heads, descending):
    starts = list(range(0, seq, tq))
    if descending:
        starts = starts[::-1]
    return [(j, r0) for j in range(heads) for r0 in starts]


def _cols(ref, j, width):
    return lambda a, b: ref[a:b, j * width:(j + 1) * width]


def _pipelined(items, scores, finish):
    s_next = scores(items[0])
    for k, item in enumerate(items):
        s = s_next
        if k + 1 < len(items):
            s_next = scores(items[k + 1])
        finish(item, s)


def _either_softmax(bound, body):
    small = bound <= SMALL_SCORE_BOUND
    pl.when(small)(functools.partial(body, subtract_max=False))
    pl.when(jnp.logical_not(small))(functools.partial(body, subtract_max=True))


def _mla_fns(q_ref, k_ref, v_ref, g_ref, o_ref, *, tq, keep, subtract_max):
    g = g_ref[...]

    def scores(item):
        j, r0 = item
        return _scores(_cols(q_ref, j, HEAD_W)(r0, r0 + tq), _cols(k_ref, j, HEAD_W), r0, tq, keep)

    def finish(item, s):
        j, r0 = item
        p_o, p_d, _ = _softmax_pieces(s, with_sum=False, subtract_max=subtract_max)
        ol = _pv(p_o, p_d, _cols(v_ref, j, HEAD_W), r0, tq)
        o = ol[:, :MLA_V] / ol[:, MLA_V:]
        o_ref[r0:r0 + tq, j * MLA_V:(j + 1) * MLA_V] = ((o * _inv_rms(o, MLA_V)) * g).astype(BF16)

    return scores, finish


def _attn_kernel(bound_ref, qm_ref, km_ref, vm_ref, gm_ref, qd_ref, kd_ref, vd_ref, gd_ref, lam_ref,
                 om_ref, od_ref, *, tq, heads):
    seq = qm_ref.shape[0]

    def body(subtract_max):
        keep = _causal_keep(tq)
        fns = {"mla": _mla_fns(qm_ref, km_ref, vm_ref, gm_ref, om_ref,
                               tq=tq, keep=keep, subtract_max=subtract_max),
               "diff": _diff_fns(qd_ref, kd_ref, vd_ref, gd_ref, lam_ref, od_ref,
                                 tq=tq, keep=keep, subtract_max=subtract_max)}
        pairs = zip(_work_items(seq, tq, heads, descending=True),
                    _work_items(seq, tq, heads, descending=not subtract_max))
        items = [(kind, it) for m_it, d_it in pairs for kind, it in (("mla", m_it), ("diff", d_it))]
        _pipelined(items, lambda it: fns[it[0]][0](it[1]), lambda it, s: fns[it[0]][1](it[1], s))

    _either_softmax(jnp.maximum(bound_ref[0, 0], bound_ref[0, 1]), body)


def _diff_fns(q_ref, k_ref, v_ref, g_ref, lam_ref, o_ref, *, tq, keep, subtract_max):
    g = g_ref[...]
    lp = lam_ref[...]
    lam = (jnp.exp(jnp.sum(lp[0:1, :] * lp[1:2, :], axis=-1, keepdims=True))
           - jnp.exp(jnp.sum(lp[2:3, :] * lp[3:4, :], axis=-1, keepdims=True))
           + LAMBDA_INIT)
    lane = lax.broadcasted_iota(jnp.int32, (tq, LANE), 1)
    first = lane < DIFF_QK

    def scores(item):
        j, r0 = item
        q = _cols(q_ref, j, LANE)(r0, r0 + tq)
        k_rows = _cols(k_ref, j, LANE)
        q1 = jnp.where(first, q, jnp.zeros_like(q))
        q2 = jnp.where(first, jnp.zeros_like(q), q)
        return _scores(q1, k_rows, r0, tq, keep), _scores(q2, k_rows, r0, tq, keep)

    def finish(item, s):
        j, r0 = item
        p1_o, p1_d, l1 = _softmax_pieces(s[0], with_sum=True, subtract_max=subtract_max)
        p2_o, p2_d, l2 = _softmax_pieces(s[1], with_sum=True, subtract_max=subtract_max)
        c = lam * (l1 / l2)
        a_o = None if p1_o is None else p1_o - c * p2_o
        o = _pv(a_o, p1_d - c * p2_d, _cols(v_ref, j, DIFF_V), r0, tq) / l1
        o_ref[r0:r0 + tq, j * DIFF_V:(j + 1) * DIFF_V] = (
            ((o * _inv_rms(o, DIFF_V)) * g) * (1.0 - LAMBDA_INIT)).astype(BF16)

    return scores, finish


def _head_spec(seq, width, heads):
    return pl.BlockSpec((seq, heads * width), lambda b, h: (b, h))


_SCALAR_SPEC = pl.BlockSpec(memory_space=pltpu.SMEM)


def _score_bound(gq, gk, d):
    return 1.01 * LOG2E * math.sqrt(d) * jnp.max(jnp.abs(gq)) * jnp.max(jnp.abs(gk))


def _attn(bounds, qm, km, vm, gm, qd, kd, vd, gd, lam_params, *, batch, seq, tq=256, heads=2):
    t = qm.shape[0]
    assert MLA_HEADS == DIFF_HEADS
    small = lambda shape: pl.BlockSpec(shape, lambda b, h: (0, 0))
    return pl.pallas_call(
        functools.partial(_attn_kernel, tq=tq, heads=heads),
        name="attn",
        grid=(batch, MLA_HEADS // heads),
        in_specs=[_SCALAR_SPEC,
                  _head_spec(seq, HEAD_W, heads), _head_spec(seq, HEAD_W, heads), _head_spec(seq, HEAD_W, heads),
                  small((1, MLA_V)),
                  _head_spec(seq, LANE, heads), _head_spec(seq, LANE, heads), _head_spec(seq, DIFF_V, heads),
                  small((1, DIFF_V)), small((4, DIFF_QK))],
        out_specs=[_head_spec(seq, MLA_V, heads), _head_spec(seq, DIFF_V, heads)],
        out_shape=[jax.ShapeDtypeStruct((t, MLA_HEADS * MLA_V), BF16),
                   jax.ShapeDtypeStruct((t, DIFF_HEADS * DIFF_V), BF16)],
        compiler_params=_params(("arbitrary", "arbitrary")),
    )(bounds, qm, km, vm, gm, qd, kd, vd, gd, lam_params)


def _out_proj_kernel(x_ref, om_ref, od_ref, w_ref, o_ref):
    km = om_ref.shape[1]
    acc = jnp.dot(om_ref[...], w_ref[0:km, :], preferred_element_type=F32)
    acc = acc + jnp.dot(od_ref[...], w_ref[km:, :], preferred_element_type=F32)
    o_ref[...] = x_ref[...] + acc


def _out_proj(x2d, om, od, w, *, tm=512):
    t = x2d.shape[0]
    return pl.pallas_call(
        _out_proj_kernel,
        name="out_proj",
        grid=(t // tm,),
        in_specs=[pl.BlockSpec((tm, D_MODEL), lambda i: (i, 0)),
                  pl.BlockSpec((tm, om.shape[1]), lambda i: (i, 0)),
                  pl.BlockSpec((tm, od.shape[1]), lambda i: (i, 0)),
                  _resident(w.shape)],
        out_specs=pl.BlockSpec((tm, D_MODEL), lambda i: (i, 0)),
        out_shape=jax.ShapeDtypeStruct((t, D_MODEL), F32),
        compiler_params=_params(("arbitrary",)),
    )(x2d, om, od, w)


def _ffn_kernel(x_ref, g_ref, wg_ref, wu_ref, wd_ref, o_ref, h_ref, *, out_chunk):
    def step(first):
        if first:
            x = x_ref[...]
            h_ref[...] = ((x * _inv_rms(x, D_MODEL)) * g_ref[...]).astype(BF16)
        h = h_ref[...]
        gate = jnp.dot(h, wg_ref[...], preferred_element_type=F32)
        up = jnp.dot(h, wu_ref[...], preferred_element_type=F32)
        act = (gate * jax.nn.sigmoid(gate) * up).astype(BF16)
        for c0 in range(0, o_ref.shape[1], out_chunk):
            cs = slice(c0, c0 + out_chunk)
            part = jnp.dot(act, wd_ref[:, cs], preferred_element_type=F32)
            o_ref[:, cs] = (x_ref[:, cs] if first else o_ref[:, cs]) + part

    first_block = pl.program_id(1) == 0
    pl.when(first_block)(functools.partial(step, True))
    pl.when(jnp.logical_not(first_block))(functools.partial(step, False))


def _ffn(x1, g, wg, wu, wd, *, tm=1024, tf=512):
    t = x1.shape[0]
    return pl.pallas_call(
        functools.partial(_ffn_kernel, out_chunk=512),
        name="ffn",
        grid=(t // tm, D_FF // tf),
        in_specs=[pl.BlockSpec((tm, D_MODEL), lambda i, f: (i, 0)),
                  pl.BlockSpec((1, D_MODEL), lambda i, f: (0, 0)),
                  pl.BlockSpec((D_MODEL, tf), lambda i, f: (0, f)),
                  pl.BlockSpec((D_MODEL, tf), lambda i, f: (0, f)),
                  pl.BlockSpec((tf, D_MODEL), lambda i, f: (f, 0))],
        out_specs=pl.BlockSpec((tm, D_MODEL), lambda i, f: (i, 0)),
        out_shape=jax.ShapeDtypeStruct((t, D_MODEL), F32),
        scratch_shapes=[pltpu.VMEM((tm, D_MODEL), BF16)],
        compiler_params=_params(("arbitrary", "arbitrary")),
    )(x1, g, wg, wu, wd)


def _rope_tables(seq):
    pos = np.arange(seq, dtype=np.float64)

    def angles(r):
        freqs = 1.0 / (ROPE_THETA ** (np.arange(0, r, 2, dtype=np.float64) / r))
        return pos[:, None] * freqs[None, :]

    am = angles(MLA_ROPE)
    cos, sin = np.cos(am), np.sin(am)
    zeros = np.zeros((seq, LANE - MLA_ROPE))
    cm = np.concatenate([cos, cos, zeros], axis=-1)
    sm = np.concatenate([-sin, sin, zeros], axis=-1)

    ad = angles(DIFF_ROT)
    cos, sin = np.cos(ad), np.sin(ad)
    rest = DIFF_QK - DIFF_ROT
    cd = np.tile(np.concatenate([cos, cos, np.ones((seq, rest))], axis=-1), (1, 2))
    sd = np.tile(np.concatenate([-sin, sin, np.zeros((seq, rest))], axis=-1), (1, 2))
    return tuple(jnp.asarray(t.astype(np.float32)) for t in (cm, sm, cd, sd))


def _with_partner(v, axis=-1):
    return jnp.concatenate([v, jnp.roll(v, MLA_ROPE // 2, axis=axis)], axis=axis)


def kernel(x, attn_norm, w_in, q_latent_norm, w_q_up, kv_latent_norm, w_kv_up, mla_q_norm, mla_k_norm, mla_out_norm, diff_q_norm, diff_k_norm, lambda_q1, lambda_k1, lambda_q2, lambda_k2, diff_out_norm, w_o, ffn_norm, w_gate, w_up, w_down):
    batch, seq, d = x.shape
    assert d == D_MODEL and attn_norm.shape[0] == 1
    t = batch * seq
    x2d = x.reshape(t, d)
    l = 0

    wt = jnp.swapaxes(w_in[l], 0, 1).astype(BF16)
    wkpe = _with_partner(wt[O_KPE:O_DQ], axis=0)
    wq = w_q_up[l].astype(BF16).reshape(Q_LORA, MLA_HEADS, MLA_QK)
    wq = jnp.concatenate([wq[:, :, :MLA_NOPE].reshape(Q_LORA, -1),
                          _with_partner(wq[:, :, MLA_NOPE:]).reshape(Q_LORA, -1)], axis=1)
    wkv = w_kv_up[l].astype(BF16).reshape(KV_LORA, MLA_HEADS, MLA_NOPE + MLA_V)
    wkv = jnp.concatenate([wkv[:, :, :MLA_NOPE].reshape(KV_LORA, -1),
                           wkv[:, :, MLA_NOPE:].reshape(KV_LORA, -1)], axis=1)

    gq, gk = mla_q_norm[l], mla_k_norm[l]
    lane_gains = jnp.stack([gq[:MLA_NOPE], _with_partner(gq[MLA_NOPE:]), gk[:MLA_NOPE], _with_partner(gk[MLA_NOPE:]),
                            jnp.tile(diff_q_norm[l], 2), jnp.tile(diff_k_norm[l], 2),
                            jnp.zeros((LANE,), F32), jnp.zeros((LANE,), F32)])
    bounds = jnp.stack([_score_bound(gq, gk, MLA_QK), _score_bound(diff_q_norm[l], diff_k_norm[l], DIFF_QK)])
    bounds = bounds.astype(F32).reshape(1, 2)
    lam_params = jnp.concatenate([lambda_q1[l:l + 1], lambda_k1[l:l + 1],
                                  lambda_q2[l:l + 1], lambda_k2[l:l + 1]], axis=0)
    cm, sm, cd, sd = _rope_tables(seq)

    (qm, km, vm, qd, kd, vd), (wo16, wg16, wu16, wd16) = _qkv(
        x2d, attn_norm[l:l + 1], wt, wkpe, q_latent_norm[l:l + 1], kv_latent_norm[l:l + 1], wq, wkv,
        lane_gains, cm, sm, cd, sd, (w_o[l], w_gate[l], w_up[l], w_down[l]), seq=seq)
    o_mla, o_diff = _attn(bounds, qm, km, vm, mla_out_norm[l:l + 1], qd, kd, vd, diff_out_norm[l:l + 1],
                          lam_params, batch=batch, seq=seq)
    x1 = _out_proj(x2d, o_mla, o_diff, wo16)
    out = _ffn(x1, ffn_norm[l:l + 1], wg16, wu16, wd16)
    return out.reshape(batch, seq, d)
```

```python
import functools
import math

import jax
import jax.numpy as jnp
import numpy as np
from jax import lax
from jax.experimental import pallas as pl
from jax.experimental.pallas import tpu as pltpu

F32 = jnp.float32
BF16 = jnp.bfloat16

D_MODEL = 2048
MLA_HEADS = 8
MLA_NOPE = 128
MLA_ROPE = 64
MLA_QK = MLA_NOPE + MLA_ROPE
MLA_V = 128
Q_LORA = 512
KV_LORA = 512
DIFF_HEADS = 8
DIFF_QK = 64
DIFF_ROT = 16
DIFF_V = 128
ROPE_THETA = 500000.0
D_FF = 5632
EPS = 1e-6
LAMBDA_INIT = 0.8 - 0.6 * math.exp(-0.3 * 0)
LOG2E = math.log2(math.e)

SMALL_SCORE_BOUND = 32.0
LANE = 128
HEAD_W = 2 * LANE
VMEM_LIMIT = 56 * 1024 * 1024

O_CKV = Q_LORA
O_KPE = O_CKV + KV_LORA
O_DQ = O_KPE + MLA_ROPE
O_DK = O_DQ + DIFF_HEADS * 2 * DIFF_QK
O_DV = O_DK + DIFF_HEADS * 2 * DIFF_QK
IN_COLS = O_DV + DIFF_HEADS * DIFF_V

_NT = (((1,), (1,)), ((), ()))


def _params(sem):
    return pltpu.CompilerParams(dimension_semantics=sem, vmem_limit_bytes=VMEM_LIMIT)


def _resident(shape):
    return pl.BlockSpec(shape, lambda *_: (0,) * len(shape), pipeline_mode=pl.Buffered(1))


def _inv_rms(xf, n):
    return lax.rsqrt(jnp.sum(xf * xf, axis=-1, keepdims=True) * (1.0 / n) + EPS)


def _dot_nt(a, b):
    return lax.dot_general(a, b, _NT, preferred_element_type=F32)


def _qkv_kernel(x_ref, ga_ref, wt_ref, wkpe_ref, glq_ref, glkv_ref, wq_ref, wkv_ref,
                lg_ref,
                cm_ref, sm_ref, cd_ref, sd_ref,
                wo_ref, wg_ref, wu_ref, wd_ref,
                qm_ref, km_ref, vm_ref, qd_ref, kd_ref, vd_ref,
                wo16_ref, wg16_ref, wu16_ref, wd16_ref):
    def cast_slab(src, dst):
        dst[...] = src[...].astype(BF16)

    tm = x_ref.shape[0]
    cm, sm, cd, sd = cm_ref[...], sm_ref[...], cd_ref[...], sd_ref[...]
    x = x_ref[...]
    h = ((x * _inv_rms(x, D_MODEL)) * ga_ref[...]).astype(BF16)

    lane = lax.broadcasted_iota(jnp.int32, (tm, LANE), 1)
    lo = lane < DIFF_QK
    is_x1 = (lane % DIFF_QK) < (DIFF_ROT // 2)

    def prep_diff(xh, g, scale):
        sq = xh * xh
        s_lo = jnp.sum(jnp.where(lo, sq, 0.0), axis=-1, keepdims=True)
        s_hi = jnp.sum(jnp.where(lo, 0.0, sq), axis=-1, keepdims=True)
        inv = lax.rsqrt(jnp.where(lo, s_lo, s_hi) * (1.0 / DIFF_QK) + EPS)
        y = (xh * inv) * g
        partner = jnp.where(is_x1, pltpu.roll(y, LANE - DIFF_ROT // 2, 1), pltpu.roll(y, DIFF_ROT // 2, 1))
        out = y * cd + partner * sd
        return out * scale if scale != 1.0 else out

    def rope_mla(y):
        return y * cm + pltpu.roll(y, MLA_ROPE, 1) * sm

    half = DIFF_HEADS // 2 * LANE
    lg = lg_ref[...]
    gqn, gqr, gkn, gkr, gdq, gdk = (lg[r:r + 1, :] for r in range(6))
    diff_scale = LOG2E / math.sqrt(DIFF_QK)

    def diff_heads(src, g, scale, dst_ref, c0):
        for j in range(DIFF_HEADS // 2):
            dst_ref[:, c0 + j * LANE:c0 + (j + 1) * LANE] = prep_diff(
                src[:, j * LANE:(j + 1) * LANE], g, scale).astype(BF16)

    mla_scale = LOG2E / math.sqrt(MLA_QK)
    nope_w = MLA_HEADS * LANE

    def q_heads(q):
        for hd in range(MLA_HEADS):
            qn = q[:, hd * LANE:(hd + 1) * LANE]
            qr = q[:, nope_w + hd * LANE:nope_w + (hd + 1) * LANE]
            ss = jnp.sum(qn * qn + 0.5 * (qr * qr), axis=-1, keepdims=True)
            sq = lax.rsqrt(ss * (1.0 / MLA_QK) + EPS)
            qm_ref[:, hd * HEAD_W:hd * HEAD_W + LANE] = (((qn * sq) * gqn) * mla_scale).astype(BF16)
            qm_ref[:, hd * HEAD_W + LANE:(hd + 1) * HEAD_W] = (
                rope_mla((qr * sq) * gqr) * mla_scale).astype(BF16)

    def k_heads(kv, kpe):
        kpe_sq = 0.5 * (kpe * kpe)
        kpe_rot = rope_mla(kpe * gkr)
        for hd in range(MLA_HEADS):
            kn = kv[:, hd * LANE:(hd + 1) * LANE]
            ssk = jnp.sum(kn * kn + kpe_sq, axis=-1, keepdims=True)
            sk = lax.rsqrt(ssk * (1.0 / MLA_QK) + EPS)
            km_ref[:, hd * HEAD_W:hd * HEAD_W + LANE] = ((kn * sk) * gkn).astype(BF16)
            km_ref[:, hd * HEAD_W + LANE:(hd + 1) * HEAD_W] = (kpe_rot * sk).astype(BF16)
            vm_ref[:, hd * HEAD_W:hd * HEAD_W + LANE] = (
                kv[:, nope_w + hd * LANE:nope_w + (hd + 1) * LANE].astype(BF16))
            vm_ref[:, hd * HEAD_W + LANE:(hd + 1) * HEAD_W] = jnp.ones((tm, LANE), BF16)

    cq = _dot_nt(h, wt_ref[0:O_CKV, :])
    ckv = _dot_nt(h, wt_ref[O_CKV:O_KPE, :])
    kpe = _dot_nt(h, wkpe_ref[...])
    cast_slab(wg_ref, wg16_ref)
    hq = ((cq * _inv_rms(cq, Q_LORA)) * glq_ref[...]).astype(BF16)
    q = jnp.dot(hq, wq_ref[...], preferred_element_type=F32)
    dq0 = _dot_nt(h, wt_ref[O_DQ:O_DQ + half, :])
    hkv = ((ckv * _inv_rms(ckv, KV_LORA)) * glkv_ref[...]).astype(BF16)
    kv = jnp.dot(hkv, wkv_ref[...], preferred_element_type=F32)
    dq1 = _dot_nt(h, wt_ref[O_DQ + half:O_DK, :])
    cast_slab(wu_ref, wu16_ref)
    q_heads(q)
    dk0 = _dot_nt(h, wt_ref[O_DK:O_DK + half, :])
    k_heads(kv, kpe)
    dk1 = _dot_nt(h, wt_ref[O_DK + half:O_DV, :])
    cast_slab(wd_ref, wd16_ref)
    diff_heads(dq0, gdq, diff_scale, qd_ref, 0)
    vd_ref[:, 0:half] = _dot_nt(h, wt_ref[O_DV:O_DV + half, :]).astype(BF16)
    diff_heads(dq1, gdq, diff_scale, qd_ref, half)
    cast_slab(wo_ref, wo16_ref)
    diff_heads(dk0, gdk, 1.0, kd_ref, 0)
    vd_ref[:, half:] = _dot_nt(h, wt_ref[O_DV + half:IN_COLS, :]).astype(BF16)
    diff_heads(dk1, gdk, 1.0, kd_ref, half)


def _qkv(x2d, ga, wt, wkpe, glq, glkv, wq, wkv, lane_gains, cm, sm, cd, sd,
         later_weights, *, seq, tm=256):
    t = x2d.shape[0]
    steps = t // tm
    spt = seq // tm
    tab = pl.BlockSpec((tm, LANE), lambda i: (i % spt, 0))
    vec = lambda n: _resident((1, n))
    widths = [MLA_HEADS * HEAD_W, MLA_HEADS * HEAD_W, MLA_HEADS * HEAD_W,
              DIFF_HEADS * LANE, DIFF_HEADS * LANE, DIFF_HEADS * DIFF_V]
    slabs = [pl.BlockSpec((w.shape[0] // steps, w.shape[1]), lambda i: (i, 0)) for w in later_weights]
    assert all(w.shape[0] % (16 * steps) == 0 for w in later_weights)
    outs = pl.pallas_call(
        _qkv_kernel,
        name="qkv",
        grid=(steps,),
        in_specs=[pl.BlockSpec((tm, D_MODEL), lambda i: (i, 0)), vec(D_MODEL),
                  _resident(wt.shape), _resident(wkpe.shape),
                  vec(Q_LORA), vec(KV_LORA), _resident(wq.shape), _resident(wkv.shape),
                  _resident(lane_gains.shape),
                  tab, tab, tab, tab] + slabs,
        out_specs=[pl.BlockSpec((tm, w), lambda i: (i, 0)) for w in widths] + slabs,
        out_shape=([jax.ShapeDtypeStruct((t, w), BF16) for w in widths]
                   + [jax.ShapeDtypeStruct(w.shape, BF16) for w in later_weights]),
        compiler_params=_params(("arbitrary",)),
    )(x2d, ga, wt, wkpe, glq, glkv, wq, wkv, lane_gains, cm, sm, cd, sd, *later_weights)
    return outs[:len(widths)], outs[len(widths):]


def _causal_keep(tq):
    row = lax.broadcasted_iota(jnp.int32, (tq, tq), 0)
    col = lax.broadcasted_iota(jnp.int32, (tq, tq), 1)
    return col <= row


def _scores(q, k_rows, r0, tq, keep):
    s_d = _dot_nt(q, k_rows(r0, r0 + tq))
    s_d = jnp.where(keep, s_d, -jnp.inf)
    s_o = None
    if r0 > 0:
        s_o = _dot_nt(q, k_rows(0, r0))
    return s_o, s_d


def _softmax_pieces(s, *, with_sum, subtract_max):
    s_o, s_d = s
    if subtract_max:
        m = jnp.max(s_d, axis=-1, keepdims=True)
        if s_o is not None:
            m = jnp.maximum(m, jnp.max(s_o, axis=-1, keepdims=True))
            s_o = s_o - m
        s_d = s_d - m
    p_d = jnp.exp2(s_d)
    p_o = None if s_o is None else jnp.exp2(s_o)
    if not with_sum:
        return p_o, p_d, None
    l = jnp.sum(p_d, axis=-1, keepdims=True)
    if p_o is not None:
        l = jnp.sum(p_o, axis=-1, keepdims=True) + l
    return p_o, p_d, l


def _pv(p_o, p_d, v_rows, r0, tq):
    o = jnp.dot(p_d.astype(BF16), v_rows(r0, r0 + tq), preferred_element_type=F32)
    if p_o is not None:
        o = o + jnp.dot(p_o.astype(BF16), v_rows(0, r0), preferred_element_type=F32)
    return o


def _work_items(seq, tq, heads, descending):
    starts = list(range(0, seq, tq))
    if descending:
        starts = starts[::-1]
    return [(j, r0) for j in range(heads) for r0 in starts]


def _cols(ref, j, width):
    return lambda a, b: ref[a:b, j * width:(j + 1) * width]


def _pipelined(items, scores, finish):
    s_next = scores(items[0])
    for k, item in enumerate(items):
        s = s_next
        if k + 1 < len(items):
            s_next = scores(items[k + 1])
        finish(item, s)


def _either_softmax(bound, body):
    small = bound <= SMALL_SCORE_BOUND
    pl.when(small)(functools.partial(body, subtract_max=False))
    pl.when(jnp.logical_not(small))(functools.partial(body, subtract_max=True))


def _mla_attn_kernel(bound_ref, q_ref, k_ref, v_ref, g_ref, o_ref, *, tq, heads):
    seq = q_ref.shape[0]

    def body(subtract_max):
        keep = _causal_keep(tq)
        g = g_ref[...]

        def scores(item):
            j, r0 = item
            return _scores(_cols(q_ref, j, HEAD_W)(r0, r0 + tq), _cols(k_ref, j, HEAD_W), r0, tq, keep)

        def finish(item, s):
            j, r0 = item
            p_o, p_d, _ = _softmax_pieces(s, with_sum=False, subtract_max=subtract_max)
            ol = _pv(p_o, p_d, _cols(v_ref, j, HEAD_W), r0, tq)
            o = ol[:, :MLA_V] / ol[:, MLA_V:]
            o_ref[r0:r0 + tq, j * MLA_V:(j + 1) * MLA_V] = ((o * _inv_rms(o, MLA_V)) * g).astype(BF16)

        _pipelined(_work_items(seq, tq, heads, descending=True), scores, finish)

    _either_softmax(bound_ref[0, 0], body)


def _diff_attn_kernel(bound_ref, q_ref, k_ref, v_ref, g_ref, lam_ref, mla_out_ref, o_ref, *, tq, heads):
    del mla_out_ref
    _either_softmax(bound_ref[0, 1], functools.partial(_diff_attn_body, q_ref, k_ref, v_ref, g_ref, lam_ref, o_ref,
                                                 tq=tq, heads=heads))


def _diff_attn_body(q_ref, k_ref, v_ref, g_ref, lam_ref, o_ref, *, tq, heads, subtract_max):
    seq = q_ref.shape[0]
    keep = _causal_keep(tq)
    g = g_ref[...]
    lp = lam_ref[...]
    lam = (jnp.exp(jnp.sum(lp[0:1, :] * lp[1:2, :], axis=-1, keepdims=True))
           - jnp.exp(jnp.sum(lp[2:3, :] * lp[3:4, :], axis=-1, keepdims=True))
           + LAMBDA_INIT)
    lane = lax.broadcasted_iota(jnp.int32, (tq, LANE), 1)
    first = lane < DIFF_QK

    def scores(item):
        j, r0 = item
        q = _cols(q_ref, j, LANE)(r0, r0 + tq)
        k_rows = _cols(k_ref, j, LANE)
        q1 = jnp.where(first, q, jnp.zeros_like(q))
        q2 = jnp.where(first, jnp.zeros_like(q), q)
        return _scores(q1, k_rows, r0, tq, keep), _scores(q2, k_rows, r0, tq, keep)

    def finish(item, s):
        j, r0 = item
        p1_o, p1_d, l1 = _softmax_pieces(s[0], with_sum=True, subtract_max=subtract_max)
        p2_o, p2_d, l2 = _softmax_pieces(s[1], with_sum=True, subtract_max=subtract_max)
        c = lam * (l1 / l2)
        a_o = None if p1_o is None else p1_o - c * p2_o
        o = _pv(a_o, p1_d - c * p2_d, _cols(v_ref, j, DIFF_V), r0, tq) / l1
        o_ref[r0:r0 + tq, j * DIFF_V:(j + 1) * DIFF_V] = (
            ((o * _inv_rms(o, DIFF_V)) * g) * (1.0 - LAMBDA_INIT)).astype(BF16)

    _pipelined(_work_items(seq, tq, heads, descending=not subtract_max), scores, finish)


def _head_spec(seq, width, heads):
    return pl.BlockSpec((seq, heads * width), lambda b, h: (b, h))


_SCALAR_SPEC = pl.BlockSpec(memory_space=pltpu.SMEM)


def _score_bound(gq, gk, d):
    return 1.01 * LOG2E * math.sqrt(d) * jnp.max(jnp.abs(gq)) * jnp.max(jnp.abs(gk))


def _mla_attn(bound, qm, km, vm, g, *, batch, seq, tq=256, heads=2):
    t = qm.shape[0]
    return pl.pallas_call(
        functools.partial(_mla_attn_kernel, tq=tq, heads=heads),
        name="mla_attn",
        grid=(batch, MLA_HEADS // heads),
        in_specs=[_SCALAR_SPEC, _head_spec(seq, HEAD_W, heads), _head_spec(seq, HEAD_W, heads), _head_spec(seq, HEAD_W, heads),
                  pl.BlockSpec((1, MLA_V), lambda b, h: (0, 0))],
        out_specs=_head_spec(seq, MLA_V, heads),
        out_shape=jax.ShapeDtypeStruct((t, MLA_HEADS * MLA_V + DIFF_HEADS * DIFF_V), BF16),
        compiler_params=_params(("arbitrary", "arbitrary")),
    )(bound, qm, km, vm, g)


def _diff_attn(bound, qd, kd, vd, g, lam_params, attn, *, batch, seq, tq=256, heads=2):
    t = qd.shape[0]
    return pl.pallas_call(
        functools.partial(_diff_attn_kernel, tq=tq, heads=heads),
        name="diff_attn",
        grid=(batch, DIFF_HEADS // heads),
        in_specs=[_SCALAR_SPEC, _head_spec(seq, LANE, heads), _head_spec(seq, LANE, heads), _head_spec(seq, DIFF_V, heads),
                  pl.BlockSpec((1, DIFF_V), lambda b, h: (0, 0)),
                  pl.BlockSpec((4, DIFF_QK), lambda b, h: (0, 0)),
                  pl.BlockSpec(memory_space=pl.ANY)],
        out_specs=pl.BlockSpec((seq, heads * DIFF_V), lambda b, h: (b, MLA_HEADS // heads + h)),
        out_shape=jax.ShapeDtypeStruct(attn.shape, BF16),
        input_output_aliases={6: 0},
        compiler_params=_params(("arbitrary", "arbitrary")),
    )(bound, qd, kd, vd, g, lam_params, attn)


def _out_proj_kernel(x_ref, a_ref, w_ref, o_ref):
    o_ref[...] = x_ref[...] + jnp.dot(a_ref[...], w_ref[...], preferred_element_type=F32)


def _out_proj(x2d, attn, w, *, tm=512):
    t = x2d.shape[0]
    return pl.pallas_call(
        _out_proj_kernel,
        name="out_proj",
        grid=(t // tm,),
        in_specs=[pl.BlockSpec((tm, D_MODEL), lambda i: (i, 0)),
                  pl.BlockSpec((tm, attn.shape[1]), lambda i: (i, 0)),
                  _resident(w.shape)],
        out_specs=pl.BlockSpec((tm, D_MODEL), lambda i: (i, 0)),
        out_shape=jax.ShapeDtypeStruct((t, D_MODEL), F32),
        compiler_params=_params(("arbitrary",)),
    )(x2d, attn, w)


def _ffn_kernel(x_ref, g_ref, wg_ref, wu_ref, wd_ref, o_ref, h_ref, *, out_chunk):
    def step(first):
        if first:
            x = x_ref[...]
            h_ref[...] = ((x * _inv_rms(x, D_MODEL)) * g_ref[...]).astype(BF16)
        h = h_ref[...]
        gate = jnp.dot(h, wg_ref[...], preferred_element_type=F32)
        up = jnp.dot(h, wu_ref[...], preferred_element_type=F32)
        act = (gate * jax.nn.sigmoid(gate) * up).astype(BF16)
        for c0 in range(0, o_ref.shape[1], out_chunk):
            cs = slice(c0, c0 + out_chunk)
            part = jnp.dot(act, wd_ref[:, cs], preferred_element_type=F32)
            o_ref[:, cs] = (x_ref[:, cs] if first else o_ref[:, cs]) + part

    first_block = pl.program_id(1) == 0
    pl.when(first_block)(functools.partial(step, True))
    pl.when(jnp.logical_not(first_block))(functools.partial(step, False))


def _ffn(x1, g, wg, wu, wd, *, tm=1024, tf=512):
    t = x1.shape[0]
    return pl.pallas_call(
        functools.partial(_ffn_kernel, out_chunk=512),
        name="ffn",
        grid=(t // tm, D_FF // tf),
        in_specs=[pl.BlockSpec((tm, D_MODEL), lambda i, f: (i, 0)),
                  pl.BlockSpec((1, D_MODEL), lambda i, f: (0, 0)),
                  pl.BlockSpec((D_MODEL, tf), lambda i, f: (0, f)),
                  pl.BlockSpec((D_MODEL, tf), lambda i, f: (0, f)),
                  pl.BlockSpec((tf, D_MODEL), lambda i, f: (f, 0))],
        out_specs=pl.BlockSpec((tm, D_MODEL), lambda i, f: (i, 0)),
        out_shape=jax.ShapeDtypeStruct((t, D_MODEL), F32),
        scratch_shapes=[pltpu.VMEM((tm, D_MODEL), BF16)],
        compiler_params=_params(("arbitrary", "arbitrary")),
    )(x1, g, wg, wu, wd)


def _rope_tables(seq):
    pos = np.arange(seq, dtype=np.float64)

    def angles(r):
        freqs = 1.0 / (ROPE_THETA ** (np.arange(0, r, 2, dtype=np.float64) / r))
        return pos[:, None] * freqs[None, :]

    am = angles(MLA_ROPE)
    cos, sin = np.cos(am), np.sin(am)
    zeros = np.zeros((seq, LANE - MLA_ROPE))
    cm = np.concatenate([cos, cos, zeros], axis=-1)
    sm = np.concatenate([-sin, sin, zeros], axis=-1)

    ad = angles(DIFF_ROT)
    cos, sin = np.cos(ad), np.sin(ad)
    rest = DIFF_QK - DIFF_ROT
    cd = np.tile(np.concatenate([cos, cos, np.ones((seq, rest))], axis=-1), (1, 2))
    sd = np.tile(np.concatenate([-sin, sin, np.zeros((seq, rest))], axis=-1), (1, 2))
    return tuple(jnp.asarray(t.astype(np.float32)) for t in (cm, sm, cd, sd))


def _with_partner(v, axis=-1):
    return jnp.concatenate([v, jnp.roll(v, MLA_ROPE // 2, axis=axis)], axis=axis)


def kernel(x, attn_norm, w_in, q_latent_norm, w_q_up, kv_latent_norm, w_kv_up, mla_q_norm, mla_k_norm, mla_out_norm, diff_q_norm, diff_k_norm, lambda_q1, lambda_k1, lambda_q2, lambda_k2, diff_out_norm, w_o, ffn_norm, w_gate, w_up, w_down):
    batch, seq, d = x.shape
    assert d == D_MODEL and attn_norm.shape[0] == 1
    t = batch * seq
    x2d = x.reshape(t, d)
    l = 0

    wt = jnp.swapaxes(w_in[l], 0, 1).astype(BF16)
    wkpe = _with_partner(wt[O_KPE:O_DQ], axis=0)
    wq = w_q_up[l].astype(BF16).reshape(Q_LORA, MLA_HEADS, MLA_QK)
    wq = jnp.concatenate([wq[:, :, :MLA_NOPE].reshape(Q_LORA, -1),
                          _with_partner(wq[:, :, MLA_NOPE:]).reshape(Q_LORA, -1)], axis=1)
    wkv = w_kv_up[l].astype(BF16).reshape(KV_LORA, MLA_HEADS, MLA_NOPE + MLA_V)
    wkv = jnp.concatenate([wkv[:, :, :MLA_NOPE].reshape(KV_LORA, -1),
                           wkv[:, :, MLA_NOPE:].reshape(KV_LORA, -1)], axis=1)

    gq, gk = mla_q_norm[l], mla_k_norm[l]
    lane_gains = jnp.stack([gq[:MLA_NOPE], _with_partner(gq[MLA_NOPE:]), gk[:MLA_NOPE], _with_partner(gk[MLA_NOPE:]),
                            jnp.tile(diff_q_norm[l], 2), jnp.tile(diff_k_norm[l], 2),
                            jnp.zeros((LANE,), F32), jnp.zeros((LANE,), F32)])
    bounds = jnp.stack([_score_bound(gq, gk, MLA_QK), _score_bound(diff_q_norm[l], diff_k_norm[l], DIFF_QK)])
    bounds = bounds.astype(F32).reshape(1, 2)
    lam_params = jnp.concatenate([lambda_q1[l:l + 1], lambda_k1[l:l + 1],
                                  lambda_q2[l:l + 1], lambda_k2[l:l + 1]], axis=0)
    cm, sm, cd, sd = _rope_tables(seq)

    (qm, km, vm, qd, kd, vd), (wo16, wg16, wu16, wd16) = _qkv(
        x2d, attn_norm[l:l + 1], wt, wkpe, q_latent_norm[l:l + 1], kv_latent_norm[l:l + 1], wq, wkv,
        lane_gains, cm, sm, cd, sd, (w_o[l], w_gate[l], w_up[l], w_down[l]), seq=seq)
    o_mla = _mla_attn(bounds, qm, km, vm, mla_out_norm[l:l + 1], batch=batch, seq=seq)
    attn = _diff_attn(bounds, qd, kd, vd, diff_out_norm[l:l + 1], lam_params, o_mla, batch=batch, seq=seq)
    x1 = _out_proj(x2d, attn, wo16)
    out = _ffn(x1, ffn_norm[l:l + 1], wg16, wu16, wd16)
    return out.reshape(batch, seq, d)
```
